```python
import math, functools
import jax
import jax.numpy as jnp
from jax import lax
import numpy as np

D_MODEL = 4096
BATCH = 1
SEQ = 8192
DEPTH = 1
DEC_BATCH = 32
DEC_SEQ = 4
PAST_LEN = 8192
PAGE_SIZE = 128

D_HEAD = 128
H_GDN = 16
H_ATT = 16
D_GDN = H_GDN * D_HEAD
D_ATT = H_ATT * D_HEAD
CONV_W = 4
CONV_DIM = 3 * D_GDN
GDN_CHUNK = 64
H_IDX = 32
D_IDX = 128
TOPK_ATT_MAX = 256
Q_BLOCK = 128
ROPE_THETA = 10000.0
LN_EPS = 1e-5
RMS_EPS = 1e-6
PEER_HEADS = 8
PEER_NKEYS = 128
PEER_EXPERTS = PEER_NKEYS * PEER_NKEYS
PEER_DQ = 256
PEER_TOPK = 16
PEER_BLOCK = 128
DEEPNORM_ALPHA = (2.0 * DEPTH) ** 0.25
DEEPNORM_BETA = (8.0 * DEPTH) ** -0.25
IN_SPLITS = (CONV_DIM, D_GDN, H_GDN, H_GDN, D_ATT, D_ATT, D_ATT, H_IDX * D_IDX, H_IDX, D_IDX, D_MODEL, D_MODEL)
IN_DIM = sum(IN_SPLITS)
IN_OFFSETS = tuple(int(o) for o in np.cumsum(IN_SPLITS)[:-1])

kernel_name = 'hybrid_gdn_dsa_peer_step'


def _layernorm(x, g, b):
    xf = x.astype(jnp.float32)
    mu = jnp.mean(xf, -1, keepdims=True)
    var = jnp.mean(jnp.square(xf - mu), -1, keepdims=True)
    return ((xf - mu) * lax.rsqrt(var + LN_EPS) * g.astype(jnp.float32) + b.astype(jnp.float32)).astype(x.dtype)


def _rope(x, pos):
    half = x.shape[-1] // 2
    inv = ROPE_THETA ** (-jnp.arange(half, dtype=jnp.float32) / half)
    ang = pos.astype(jnp.float32)[:, None] * inv[None, :]
    cos = jnp.cos(ang)[None, :, None, :]
    sin = jnp.sin(ang)[None, :, None, :]
    x1 = x[..., :half].astype(jnp.float32)
    x2 = x[..., half:].astype(jnp.float32)
    return jnp.concatenate([x1 * cos - x2 * sin, x2 * cos + x1 * sin], -1).astype(x.dtype)


def _l2n(a):
    return a * lax.rsqrt(jnp.sum(a * a, -1, keepdims=True) + RMS_EPS)


def _causal_conv(u, buf, conv_w):
    t = u.shape[1]
    up = jnp.concatenate([buf.astype(u.dtype), u], axis=1)
    out = sum(up[:, j:j + t] * conv_w[j] for j in range(CONV_W))
    return jax.nn.silu(out), up[:, t:]


def _gdn_qkv(conv_out):
    b, t, _ = conv_out.shape
    q, k, v = (a.reshape(b, t, H_GDN, D_HEAD).astype(jnp.float32) for a in jnp.split(conv_out, 3, axis=-1))
    return _l2n(q) * D_HEAD ** -0.5, _l2n(k), v


def _gdn_gates(b_raw, a_raw, a_log, dt_bias):
    beta = jax.nn.sigmoid(b_raw.astype(jnp.float32))
    g = -jnp.exp(a_log.astype(jnp.float32)) * jax.nn.softplus(a_raw.astype(jnp.float32) + dt_bias.astype(jnp.float32))
    return g, beta


def _gdn_chunked(q, k, v, g, beta, s0):
    b, t, h, _ = q.shape
    c = GDN_CHUNK
    n = t // c

    def ch(a):
        return jnp.swapaxes(jnp.moveaxis(a.reshape(b, n, c, h, *a.shape[3:]), 1, 0), 2, 3)

    qc, kc, vc, gc, bc = ch(q), ch(k), ch(v), ch(g), ch(beta)
    gam = jnp.cumsum(gc, axis=-1)
    incl = jnp.tril(jnp.ones((c, c), bool))
    strict = jnp.tril(jnp.ones((c, c), bool), -1)
    decay = jnp.exp(jnp.where(incl, gam[..., :, None] - gam[..., None, :], -jnp.inf))
    kb = kc * bc[..., None]
    a_mat = jnp.where(strict, jnp.einsum('nbhid,nbhjd->nbhij', kb, kc) * decay, 0.0) + jnp.eye(c, dtype=jnp.float32)
    solve = functools.partial(lax.linalg.triangular_solve, left_side=True, lower=True, unit_diagonal=True)
    u = solve(a_mat, vc * bc[..., None])
    w = solve(a_mat, kb * jnp.exp(gam)[..., None])
    qk = jnp.einsum('nbhid,nbhjd->nbhij', qc, kc) * decay

    def step(s, inp):
        q_c, k_c, u_c, w_c, g_c, qk_c = inp
        v_new = u_c - jnp.einsum('bhcd,bhde->bhce', w_c, s)
        o = jnp.einsum('bhcd,bhde->bhce', q_c * jnp.exp(g_c)[..., None], s) + jnp.einsum('bhij,bhje->bhie', qk_c, v_new)
        g_last = g_c[..., -1:]
        s = s * jnp.exp(g_last)[..., None] + jnp.einsum('bhcd,bhce->bhde', k_c * jnp.exp(g_last - g_c)[..., None], v_new)
        return s, o

    s, o = lax.scan(step, s0, (qc, kc, u, w, gam, qk))
    o = jnp.moveaxis(jnp.swapaxes(o, 2, 3), 0, 1).reshape(b, t, h, -1)
    return o, s


def _gdn_recurrent(q, k, v, g, beta, s0):
    def step(s, inp):
        q_t, k_t, v_t, g_t, b_t = inp
        s = s * jnp.exp(g_t)[..., None, None]
        ks = jnp.einsum('bhk,bhkv->bhv', k_t, s)
        s = s + jnp.einsum('bhk,bhv->bhkv', k_t * b_t[..., None], v_t - ks)
        return s, jnp.einsum('bhk,bhkv->bhv', q_t, s)

    tm = lambda a: jnp.moveaxis(a, 1, 0)
    s, o = lax.scan(step, s0, (tm(q), tm(k), tm(v), tm(g), tm(beta)))
    return jnp.moveaxis(o, 0, 1), s


def _gdn_out(o, z, norm_w):
    b, t = z.shape[:2]
    o = o * lax.rsqrt(jnp.mean(o * o, -1, keepdims=True) + RMS_EPS) * norm_w.astype(jnp.float32)
    return (o.reshape(b, t, D_GDN) * jax.nn.silu(z.astype(jnp.float32))).astype(z.dtype)


def _att_heads(q_att, k_att, v_att, q_idx, w_idx, k_idx, pos):
    b, t, _ = q_att.shape
    qa = _rope(q_att.reshape(b, t, H_ATT, D_HEAD), pos)
    ka = _rope(k_att.reshape(b, t, H_ATT, D_HEAD), pos)
    va = v_att.reshape(b, t, H_ATT, D_HEAD)
    qi = _rope(q_idx.reshape(b, t, H_IDX, D_IDX), pos)
    ki = _rope(k_idx[:, :, None, :], pos)[:, :, 0]
    wi = w_idx * (H_IDX ** -0.5 * D_IDX ** -0.5)
    return qa, ka, va, qi, wi, ki


def _indexer_topk(qi, wi, ki, q_pos, n_sel):
    rel = jax.nn.relu(jnp.einsum('bthd,bsd->bths', qi, ki))
    score = jnp.einsum('bths,bth->bts', rel, wi).astype(jnp.float32)
    key_pos = jnp.arange(ki.shape[1])
    score = jnp.where(key_pos[None, None, :] <= q_pos[None, :, None], score, -jnp.inf)
    _, sel = lax.top_k(score, n_sel)
    return sel


def _sparse_attend(q, k_sel, v_sel, sel, q_pos):
    logits = jnp.einsum('bthd,btkhd->bthk', q, k_sel).astype(jnp.float32) * D_HEAD ** -0.5
    valid = (sel <= q_pos[None, :, None])[:, :, None, :]
    p = jax.nn.softmax(jnp.where(valid, logits, -jnp.inf), axis=-1)
    return jnp.einsum('bthk,btkhd->bthd', p.astype(v_sel.dtype), v_sel)


def _dsa_prompt(q, k, v, qi, wi, ki):
    b, s = q.shape[:2]
    n = s // Q_BLOCK
    n_sel = min(TOPK_ATT_MAX, s // 4)
    gather = jax.vmap(lambda a, i: a[i])

    def block(args):
        qb, qib, wib, posb = args
        sel = _indexer_topk(qib, wib, ki, posb, n_sel)
        return _sparse_attend(qb, gather(k, sel), gather(v, sel), sel, posb)

    blocks = lambda a: jnp.swapaxes(a.reshape(b, n, Q_BLOCK, *a.shape[2:]), 0, 1)
    out = lax.map(block, (blocks(q), blocks(qi), blocks(wi), jnp.arange(s).reshape(n, Q_BLOCK)))
    return jnp.swapaxes(out, 0, 1).reshape(b, s, D_ATT)


def _dsa_sample(q, k_new, v_new, qi, wi, ki_new, cache_k, cache_v, cache_idx_k, page_table, layer):
    db, t = q.shape[:2]
    n_sel = min(TOPK_ATT_MAX, (PAST_LEN + t) // 4)
    ki_past = cache_idx_k[layer, page_table].reshape(db, PAST_LEN, D_IDX)
    ki_all = jnp.concatenate([ki_past, ki_new.astype(ki_past.dtype)], axis=1)
    q_pos = PAST_LEN + jnp.arange(t)
    sel = _indexer_topk(qi, wi, ki_all, q_pos, n_sel)
    past_i = jnp.minimum(sel, PAST_LEN - 1)
    phys = jnp.take_along_axis(page_table, (past_i // PAGE_SIZE).reshape(db, -1), axis=1).reshape(sel.shape)
    slot = past_i % PAGE_SIZE
    new_i = jnp.clip(sel - PAST_LEN, 0, t - 1)
    is_new = (sel >= PAST_LEN)[..., None, None]
    gather = jax.vmap(lambda a, i: a[i])
    k_sel = jnp.where(is_new, gather(k_new, new_i).astype(cache_k.dtype), cache_k[layer, phys, slot])
    v_sel = jnp.where(is_new, gather(v_new, new_i).astype(cache_v.dtype), cache_v[layer, phys, slot])
    return _sparse_attend(q, k_sel, v_sel, sel, q_pos).reshape(db, t, D_ATT)


def _peer(x, wq, k1, k2, u, v):
    shape = x.shape
    xf = x.reshape(-1, D_MODEL)
    n = xf.shape[0]
    pad = (-n) % PEER_BLOCK
    xf = jnp.pad(xf, ((0, pad), (0, 0)))

    def block(xb):
        q = (xb @ wq).reshape(-1, PEER_HEADS, 2, PEER_DQ // 2)
        s1 = jnp.einsum('phd,hkd->phk', q[:, :, 0], k1).astype(jnp.float32)
        s2 = jnp.einsum('phd,hkd->phk', q[:, :, 1], k2).astype(jnp.float32)
        v1, i1 = lax.top_k(s1, PEER_TOPK)
        v2, i2 = lax.top_k(s2, PEER_TOPK)
        cand = (v1[..., :, None] + v2[..., None, :]).reshape(-1, PEER_HEADS, PEER_TOPK * PEER_TOPK)
        cidx = (i1[..., :, None] * PEER_NKEYS + i2[..., None, :]).reshape(-1, PEER_HEADS, PEER_TOPK * PEER_TOPK)
        top_s, pos = lax.top_k(cand, PEER_TOPK)
        experts = jnp.take_along_axis(cidx, pos, axis=-1)
        gate = jax.nn.softmax(top_s, axis=-1)
        act = jax.nn.gelu(jnp.einsum('phkd,pd->phk', u[experts], xb).astype(jnp.float32), approximate=False)
        return jnp.einsum('phk,phkd->pd', (gate * act).astype(xb.dtype), v[experts])

    out = lax.map(block, xf.reshape(-1, PEER_BLOCK, D_MODEL))
    return out.reshape(-1, D_MODEL)[:n].reshape(shape)


def _group(x, pos, conv_buf, ssm0, chunked, attend, w_in, conv_w, a_log, dt_bias, gdn_norm_w,
           w_br_gdn, w_br_att, w_out, ln1_g, ln1_b, peer_wq, peer_k1, peer_k2, peer_u, peer_v, ln2_g, ln2_b):
    (qkv, z, b_raw, a_raw, q_att, k_att, v_att, q_idx, w_idx, k_idx, gate_gdn, gate_att) = jnp.split(x @ w_in, IN_OFFSETS, axis=-1)
    conv_out, conv_new = _causal_conv(qkv, conv_buf, conv_w)
    q, k, v = _gdn_qkv(conv_out)
    g, beta = _gdn_gates(b_raw, a_raw, a_log, dt_bias)
    gdn = _gdn_chunked if chunked else _gdn_recurrent
    o, ssm_new = gdn(q, k, v, g, beta, ssm0.astype(jnp.float32))
    o_gdn = _gdn_out(o, z, gdn_norm_w)
    qa, ka, va, qi, wi, ki = _att_heads(q_att, k_att, v_att, q_idx, w_idx, k_idx, pos)
    o_att = attend(qa, ka, va, qi, wi, ki)
    mix = (jax.nn.sigmoid(gate_gdn) * (o_gdn @ w_br_gdn) + jax.nn.sigmoid(gate_att) * (o_att @ w_br_att)) @ w_out
    h = _layernorm(DEEPNORM_ALPHA * x + mix, ln1_g, ln1_b)
    y = _layernorm(DEEPNORM_ALPHA * h + _peer(h, peer_wq, peer_k1, peer_k2, peer_u, peer_v), ln2_g, ln2_b)
    return y, ka, va, ki, conv_new, ssm_new.astype(x.dtype)


def setup_inputs(seed: int = 0) -> dict:
    key = jax.random.key(seed)
    ks = jax.random.split(key, 32)
    f32 = jnp.float32
    n_pages = PAST_LEN // PAGE_SIZE
    n_pool = (5 * DEC_BATCH * n_pages) // 4
    nrm = lambda k, shape, s: jax.random.normal(k, shape, f32) * s
    dt = jnp.exp(jax.random.uniform(ks[11], (DEPTH, H_GDN), f32, math.log(1e-3), math.log(1e-1)))
    return {
        'x_prompt': nrm(ks[0], (BATCH, SEQ, D_MODEL), 1.0),
        'x_sample': nrm(ks[1], (DEC_BATCH, DEC_SEQ, D_MODEL), 1.0),
        'cache_k': nrm(ks[2], (DEPTH, n_pool, PAGE_SIZE, H_ATT, D_HEAD), 1.0),
        'cache_v': nrm(ks[3], (DEPTH, n_pool, PAGE_SIZE, H_ATT, D_HEAD), 1.0),
        'cache_idx_k': nrm(ks[4], (DEPTH, n_pool, PAGE_SIZE, D_IDX), 1.0),
        'state_conv': nrm(ks[5], (DEPTH, DEC_BATCH, CONV_W - 1, CONV_DIM), 1.0),
        'state_ssm': nrm(ks[6], (DEPTH, DEC_BATCH, H_GDN, D_HEAD, D_HEAD), 0.1),
        'page_table': jax.random.permutation(ks[7], n_pool)[:DEC_BATCH * n_pages].reshape(DEC_BATCH, n_pages).astype(jnp.int32),
        'w_in': nrm(ks[8], (DEPTH, D_MODEL, IN_DIM), D_MODEL ** -0.5),
        'conv_w': nrm(ks[9], (DEPTH, CONV_W, CONV_DIM), CONV_W ** -0.5),
        'a_log': jnp.log(jax.random.uniform(ks[10], (DEPTH, H_GDN), f32, 1.0, 16.0)),
        'dt_bias': dt + jnp.log(-jnp.expm1(-dt)),
        'gdn_norm_w': 1.0 + nrm(ks[12], (DEPTH, D_HEAD), 0.02),
        'w_br_gdn': nrm(ks[13], (DEPTH, D_GDN, D_MODEL), D_GDN ** -0.5),
        'w_br_att': nrm(ks[14], (DEPTH, D_ATT, D_MODEL), D_ATT ** -0.5),
        'w_out': nrm(ks[15], (DEPTH, D_MODEL, D_MODEL), D_MODEL ** -0.5 * DEEPNORM_BETA),
        'ln1_g': 1.0 + nrm(ks[16], (DEPTH, D_MODEL), 0.02),
        'ln1_b': nrm(ks[17], (DEPTH, D_MODEL), 0.02),
        'peer_wq': nrm(ks[18], (DEPTH, D_MODEL, PEER_HEADS * PEER_DQ), D_MODEL ** -0.5),
        'peer_k1': nrm(ks[19], (DEPTH, PEER_HEADS, PEER_NKEYS, PEER_DQ // 2), (PEER_DQ // 2) ** -0.5),
        'peer_k2': nrm(ks[20], (DEPTH, PEER_HEADS, PEER_NKEYS, PEER_DQ // 2), (PEER_DQ // 2) ** -0.5),
        'peer_u': nrm(ks[21], (DEPTH, PEER_EXPERTS, D_MODEL), D_MODEL ** -0.5),
        'peer_v': nrm(ks[22], (DEPTH, PEER_EXPERTS, D_MODEL), DEEPNORM_BETA),
        'ln2_g': 1.0 + nrm(ks[23], (DEPTH, D_MODEL), 0.02),
        'ln2_b': nrm(ks[24], (DEPTH, D_MODEL), 0.02),
    }


def reference(x_prompt, x_sample, cache_k, cache_v, cache_idx_k, state_conv, state_ssm, page_table,
              w_in, conv_w, a_log, dt_bias, gdn_norm_w, w_br_gdn, w_br_att, w_out, ln1_g, ln1_b,
              peer_wq, peer_k1, peer_k2, peer_u, peer_v, ln2_g, ln2_b):
    bp, sp = x_prompt.shape[:2]
    ds = x_sample.shape[1]
    pos_p = jnp.arange(sp)
    pos_s = PAST_LEN + jnp.arange(ds)
    new = []
    for l in range(DEPTH):
        wts = (w_in[l], conv_w[l], a_log[l], dt_bias[l], gdn_norm_w[l], w_br_gdn[l], w_br_att[l], w_out[l],
               ln1_g[l], ln1_b[l], peer_wq[l], peer_k1[l], peer_k2[l], peer_u[l], peer_v[l], ln2_g[l], ln2_b[l])
        x_prompt, kp, vp, ip, cp, ssp = _group(
            x_prompt, pos_p, jnp.zeros((bp, CONV_W - 1, CONV_DIM), x_prompt.dtype),
            jnp.zeros((bp, H_GDN, D_HEAD, D_HEAD), jnp.float32), True, _dsa_prompt, *wts)
        attend_s = functools.partial(_dsa_sample, cache_k=cache_k, cache_v=cache_v, cache_idx_k=cache_idx_k,
                                     page_table=page_table, layer=l)
        x_sample, kss, vss, iss, css, sss = _group(
            x_sample, pos_s, state_conv[l], state_ssm[l], False, attend_s, *wts)
        new.append((kp, vp, ip, cp, ssp, kss, vss, iss, css, sss))
    (k_p, v_p, ik_p, conv_p, ssm_p, k_s, v_s, ik_s, conv_s, ssm_s) = (jnp.stack(a) for a in zip(*new))
    return (x_prompt, x_sample, k_p, v_p, ik_p, conv_p, ssm_p, k_s, v_s, ik_s, conv_s, ssm_s)
```

```python
import functools
import math

import jax
import jax.numpy as jnp
import numpy as np
from jax import lax
from jax.experimental import pallas as pl
from jax.experimental.pallas import tpu as pltpu

D_MODEL = 4096
PAGE_SIZE = 128
D_HEAD = 128
H_GDN = 16
H_ATT = 16
D_GDN = H_GDN * D_HEAD
D_ATT = H_ATT * D_HEAD
CONV_W = 4
CONV_DIM = 3 * D_GDN
GDN_CHUNK = 64
H_IDX = 32
D_IDX = 128
TOPK_ATT_MAX = 256
Q_BLOCK = 128
ROPE_THETA = 10000.0
LN_EPS = 1e-5
RMS_EPS = 1e-6
PEER_HEADS = 8
PEER_NKEYS = 128
PEER_DQ = 256
PEER_TOPK = 16
PEER_BLOCK = 128
DEPTH = 1
DEEPNORM_ALPHA = (2.0 * DEPTH) ** 0.25
IN_SPLITS = (CONV_DIM, D_GDN, H_GDN, H_GDN, D_ATT, D_ATT, D_ATT, H_IDX * D_IDX, H_IDX, D_IDX, D_MODEL, D_MODEL)
IN_OFFSETS = tuple(int(o) for o in np.cumsum(IN_SPLITS)[:-1])

VMEM_LIMIT = 48 * 1024 * 1024


def _mm_kernel(x_ref, w_ref, o_ref, acc_ref):
    @pl.when(pl.program_id(2) == 0)
    def _():
        acc_ref[...] = jnp.zeros_like(acc_ref)

    acc_ref[...] += jnp.dot(x_ref[...], w_ref[...], preferred_element_type=jnp.float32)

    @pl.when(pl.program_id(2) == pl.num_programs(2) - 1)
    def _():
        o_ref[...] = acc_ref[...]


def _pick(n, cands):
    for c in cands:
        if n % c == 0:
            return c
    return n


def _mm(x, w):
    m, k = x.shape
    n = w.shape[1]
    tm = _pick(m, (1024, 512, 256, 128))
    tn = _pick(n, (1280, 1024, 768, 512, 256, 128))
    tk = _pick(k, (1024, 512, 256, 128))
    return pl.pallas_call(
        _mm_kernel,
        grid=(m // tm, n // tn, k // tk),
        in_specs=[pl.BlockSpec((tm, tk), lambda i, j, l: (i, l)),
                  pl.BlockSpec((tk, tn), lambda i, j, l: (l, j))],
        out_specs=pl.BlockSpec((tm, tn), lambda i, j, l: (i, j)),
        out_shape=jax.ShapeDtypeStruct((m, n), jnp.float32),
        scratch_shapes=[pltpu.VMEM((tm, tn), jnp.float32)],
        compiler_params=pltpu.CompilerParams(
            dimension_semantics=("parallel", "parallel", "arbitrary"),
            vmem_limit_bytes=VMEM_LIMIT),
        name="mm",
    )(x, w)


def _mmf(x, w):
    shp = x.shape
    n = w.shape[1]
    wb = jnp.pad(w.astype(jnp.bfloat16), ((0, 0), (0, (-n) % 128)))
    y = _mm(x.reshape(-1, shp[-1]).astype(jnp.bfloat16), wb)[:, :n]
    return y.reshape(*shp[:-1], n)


def _layernorm(x, g, b):
    mu = jnp.mean(x, -1, keepdims=True)
    var = jnp.mean(jnp.square(x - mu), -1, keepdims=True)
    return (x - mu) * lax.rsqrt(var + LN_EPS) * g + b


def _rope(x, pos):
    half = x.shape[-1] // 2
    inv = ROPE_THETA ** (-jnp.arange(half, dtype=jnp.float32) / half)
    ang = pos.astype(jnp.float32)[:, None] * inv[None, :]
    cos = jnp.cos(ang)[None, :, None, :]
    sin = jnp.sin(ang)[None, :, None, :]
    x1 = x[..., :half]
    x2 = x[..., half:]
    return jnp.concatenate([x1 * cos - x2 * sin, x2 * cos + x1 * sin], -1)


def _l2n(a):
    return a * lax.rsqrt(jnp.sum(a * a, -1, keepdims=True) + RMS_EPS)


def _causal_conv(u, buf, conv_w):
    t = u.shape[1]
    up = jnp.concatenate([buf.astype(u.dtype), u], axis=1)
    out = sum(up[:, j:j + t] * conv_w[j] for j in range(CONV_W))
    return jax.nn.silu(out), up[:, t:]


def _gdn_qkv(conv_out):
    b, t, _ = conv_out.shape
    q, k, v = (a.reshape(b, t, H_GDN, D_HEAD) for a in jnp.split(conv_out, 3, axis=-1))
    return _l2n(q) * D_HEAD ** -0.5, _l2n(k), v


def _gdn_gates(b_raw, a_raw, a_log, dt_bias):
    beta = jax.nn.sigmoid(b_raw)
    g = -jnp.exp(a_log) * jax.nn.softplus(a_raw + dt_bias)
    return g, beta


def _gdn_chunked(q, k, v, g, beta, s0):
    b, t, h, _ = q.shape
    c = GDN_CHUNK
    n = t // c

    def ch(a):
        return jnp.swapaxes(jnp.moveaxis(a.reshape(b, n, c, h, *a.shape[3:]), 1, 0), 2, 3)

    qc, kc, vc, gc, bc = ch(q), ch(k), ch(v), ch(g), ch(beta)
    gam = jnp.cumsum(gc, axis=-1)
    incl = jnp.tril(jnp.ones((c, c), bool))
    strict = jnp.tril(jnp.ones((c, c), bool), -1)
    decay = jnp.exp(jnp.where(incl, gam[..., :, None] - gam[..., None, :], -jnp.inf))
    kb = kc * bc[..., None]
    a_mat = jnp.where(strict, jnp.einsum('nbhid,nbhjd->nbhij', kb, kc) * decay, 0.0) + jnp.eye(c, dtype=jnp.float32)
    solve = functools.partial(lax.linalg.triangular_solve, left_side=True, lower=True, unit_diagonal=True)
    u = solve(a_mat, vc * bc[..., None])
    w = solve(a_mat, kb * jnp.exp(gam)[..., None])
    qk = jnp.einsum('nbhid,nbhjd->nbhij', qc, kc) * decay

    def step(s, inp):
        q_c, k_c, u_c, w_c, g_c, qk_c = inp
        v_new = u_c - jnp.einsum('bhcd,bhde->bhce', w_c, s)
        o = jnp.einsum('bhcd,bhde->bhce', q_c * jnp.exp(g_c)[..., None], s) + jnp.einsum('bhij,bhje->bhie', qk_c, v_new)
        g_last = g_c[..., -1:]
        s = s * jnp.exp(g_last)[..., None] + jnp.einsum('bhcd,bhce->bhde', k_c * jnp.exp(g_last - g_c)[..., None], v_new)
        return s, o

    s, o = lax.scan(step, s0, (qc, kc, u, w, gam, qk))
    o = jnp.moveaxis(jnp.swapaxes(o, 2, 3), 0, 1).reshape(b, t, h, -1)
    return o, s


def _gdn_recurrent(q, k, v, g, beta, s0):
    def step(s, inp):
        q_t, k_t, v_t, g_t, b_t = inp
        s = s * jnp.exp(g_t)[..., None, None]
        ks = jnp.einsum('bhk,bhkv->bhv', k_t, s)
        s = s + jnp.einsum('bhk,bhv->bhkv', k_t * b_t[..., None], v_t - ks)
        return s, jnp.einsum('bhk,bhkv->bhv', q_t, s)

    tm = lambda a: jnp.moveaxis(a, 1, 0)
    s, o = lax.scan(step, s0, (tm(q), tm(k), tm(v), tm(g), tm(beta)))
    return jnp.moveaxis(o, 0, 1), s


def _gdn_out(o, z, norm_w):
    b, t = z.shape[:2]
    o = o * lax.rsqrt(jnp.mean(o * o, -1, keepdims=True) + RMS_EPS) * norm_w
    return o.reshape(b, t, D_GDN) * jax.nn.silu(z)


def _att_heads(q_att, k_att, v_att, q_idx, w_idx, k_idx, pos):
    b, t, _ = q_att.shape
    qa = _rope(q_att.reshape(b, t, H_ATT, D_HEAD), pos)
    ka = _rope(k_att.reshape(b, t, H_ATT, D_HEAD), pos)
    va = v_att.reshape(b, t, H_ATT, D_HEAD)
    qi = _rope(q_idx.reshape(b, t, H_IDX, D_IDX), pos)
    ki = _rope(k_idx[:, :, None, :], pos)[:, :, 0]
    wi = w_idx * (H_IDX ** -0.5 * D_IDX ** -0.5)
    return qa, ka, va, qi, wi, ki


def _indexer_topk(qi, wi, ki, q_pos, n_sel):
    rel = jax.nn.relu(jnp.einsum('bthd,bsd->bths', qi, ki))
    score = jnp.einsum('bths,bth->bts', rel, wi).astype(jnp.float32)
    key_pos = jnp.arange(ki.shape[1])
    score = jnp.where(key_pos[None, None, :] <= q_pos[None, :, None], score, -jnp.inf)
    _, sel = lax.top_k(score, n_sel)
    return sel


def _sparse_attend(q, k_sel, v_sel, sel, q_pos):
    logits = jnp.einsum('bthd,btkhd->bthk', q, k_sel).astype(jnp.float32) * D_HEAD ** -0.5
    valid = (sel <= q_pos[None, :, None])[:, :, None, :]
    p = jax.nn.softmax(jnp.where(valid, logits, -jnp.inf), axis=-1)
    return jnp.einsum('bthk,btkhd->bthd', p.astype(v_sel.dtype), v_sel)


def _dsa_prompt(q, k, v, qi, wi, ki):
    b, s = q.shape[:2]
    n = s // Q_BLOCK
    n_sel = min(TOPK_ATT_MAX, s // 4)
    gather = jax.vmap(lambda a, i: a[i])

    def block(args):
        qb, qib, wib, posb = args
        sel = _indexer_topk(qib, wib, ki, posb, n_sel)
        return _sparse_attend(qb, gather(k, sel), gather(v, sel), sel, posb)

    blocks = lambda a: jnp.swapaxes(a.reshape(b, n, Q_BLOCK, *a.shape[2:]), 0, 1)
    out = lax.map(block, (blocks(q), blocks(qi), blocks(wi), jnp.arange(s).reshape(n, Q_BLOCK)))
    return jnp.swapaxes(out, 0, 1).reshape(b, s, D_ATT)


def _dsa_sample(q, k_new, v_new, qi, wi, ki_new, cache_k, cache_v, cache_idx_k, page_table, layer):
    db, t = q.shape[:2]
    past_len = page_table.shape[1] * PAGE_SIZE
    n_sel = min(TOPK_ATT_MAX, (past_len + t) // 4)
    ki_past = cache_idx_k[layer, page_table].reshape(db, past_len, D_IDX)
    ki_all = jnp.concatenate([ki_past, ki_new.astype(ki_past.dtype)], axis=1)
    q_pos = past_len + jnp.arange(t)
    sel = _indexer_topk(qi, wi, ki_all, q_pos, n_sel)
    past_i = jnp.minimum(sel, past_len - 1)
    phys = jnp.take_along_axis(page_table, (past_i // PAGE_SIZE).reshape(db, -1), axis=1).reshape(sel.shape)
    slot = past_i % PAGE_SIZE
    new_i = jnp.clip(sel - past_len, 0, t - 1)
    is_new = (sel >= past_len)[..., None, None]
    gather = jax.vmap(lambda a, i: a[i])
    k_sel = jnp.where(is_new, gather(k_new, new_i).astype(cache_k.dtype), cache_k[layer, phys, slot])
    v_sel = jnp.where(is_new, gather(v_new, new_i).astype(cache_v.dtype), cache_v[layer, phys, slot])
    return _sparse_attend(q, k_sel, v_sel, sel, q_pos).reshape(db, t, D_ATT)


def _peer(x, wq, k1, k2, u, v):
    shape = x.shape
    xf = x.reshape(-1, D_MODEL)
    n = xf.shape[0]
    pad = (-n) % PEER_BLOCK
    xf = jnp.pad(xf, ((0, pad), (0, 0)))
    qall = _mmf(xf, wq)

    def block(args):
        xb, qb = args
        q = qb.reshape(-1, PEER_HEADS, 2, PEER_DQ // 2)
        s1 = jnp.einsum('phd,hkd->phk', q[:, :, 0], k1).astype(jnp.float32)
        s2 = jnp.einsum('phd,hkd->phk', q[:, :, 1], k2).astype(jnp.float32)
        v1, i1 = lax.top_k(s1, PEER_TOPK)
        v2, i2 = lax.top_k(s2, PEER_TOPK)
        cand = (v1[..., :, None] + v2[..., None, :]).reshape(-1, PEER_HEADS, PEER_TOPK * PEER_TOPK)
        cidx = (i1[..., :, None] * PEER_NKEYS + i2[..., None, :]).reshape(-1, PEER_HEADS, PEER_TOPK * PEER_TOPK)
        top_s, pos = lax.top_k(cand, PEER_TOPK)
        experts = jnp.take_along_axis(cidx, pos, axis=-1)
        gate = jax.nn.softmax(top_s, axis=-1)
        act = jax.nn.gelu(jnp.einsum('phkd,pd->phk', u[experts], xb).astype(jnp.float32), approximate=False)
        return jnp.einsum('phk,phkd->pd', (gate * act).astype(xb.dtype), v[experts])

    out = lax.map(block, (xf.reshape(-1, PEER_BLOCK, D_MODEL), qall.reshape(-1, PEER_BLOCK, PEER_HEADS * PEER_DQ)))
    return out.reshape(-1, D_MODEL)[:n].reshape(shape)


def _group(x, pos, conv_buf, ssm0, chunked, attend, w_in, conv_w, a_log, dt_bias, gdn_norm_w,
           w_br_gdn, w_br_att, w_out, ln1_g, ln1_b, peer_wq, peer_k1, peer_k2, peer_u, peer_v, ln2_g, ln2_b):
    (qkv, z, b_raw, a_raw, q_att, k_att, v_att, q_idx, w_idx, k_idx, gate_gdn, gate_att) = jnp.split(
        _mmf(x, w_in), IN_OFFSETS, axis=-1)
    conv_out, conv_new = _causal_conv(qkv, conv_buf, conv_w)
    q, k, v = _gdn_qkv(conv_out)
    g, beta = _gdn_gates(b_raw, a_raw, a_log, dt_bias)
    gdn = _gdn_chunked if chunked else _gdn_recurrent
    o, ssm_new = gdn(q, k, v, g, beta, ssm0.astype(jnp.float32))
    o_gdn = _gdn_out(o, z, gdn_norm_w)
    qa, ka, va, qi, wi, ki = _att_heads(q_att, k_att, v_att, q_idx, w_idx, k_idx, pos)
    o_att = attend(qa, ka, va, qi, wi, ki)
    mix = _mmf(jax.nn.sigmoid(gate_gdn) * _mmf(o_gdn, w_br_gdn) + jax.nn.sigmoid(gate_att) * _mmf(o_att, w_br_att), w_out)
    h = _layernorm(DEEPNORM_ALPHA * x + mix, ln1_g, ln1_b)
    y = _layernorm(DEEPNORM_ALPHA * h + _peer(h, peer_wq, peer_k1, peer_k2, peer_u, peer_v), ln2_g, ln2_b)
    return y, ka, va, ki, conv_new, ssm_new.astype(x.dtype)


def kernel(x_prompt, x_sample, cache_k, cache_v, cache_idx_k, state_conv, state_ssm, page_table, w_in, conv_w, a_log, dt_bias, gdn_norm_w, w_br_gdn, w_br_att, w_out, ln1_g, ln1_b, peer_wq, peer_k1, peer_k2, peer_u, peer_v, ln2_g, ln2_b):
    bp, sp = x_prompt.shape[:2]
    ds = x_sample.shape[1]
    past_len = page_table.shape[1] * PAGE_SIZE
    pos_p = jnp.arange(sp)
    pos_s = past_len + jnp.arange(ds)
    l = 0
    wts = (w_in[l], conv_w[l], a_log[l], dt_bias[l], gdn_norm_w[l], w_br_gdn[l], w_br_att[l], w_out[l],
           ln1_g[l], ln1_b[l], peer_wq[l], peer_k1[l], peer_k2[l], peer_u[l], peer_v[l], ln2_g[l], ln2_b[l])
    y_p, kp, vp, ip, cp, ssp = _group(
        x_prompt, pos_p, jnp.zeros((bp, CONV_W - 1, CONV_DIM), x_prompt.dtype),
        jnp.zeros((bp, H_GDN, D_HEAD, D_HEAD), jnp.float32), True, _dsa_prompt, *wts)
    attend_s = functools.partial(_dsa_sample, cache_k=cache_k, cache_v=cache_v, cache_idx_k=cache_idx_k,
                                 page_table=page_table, layer=l)
    y_s, kss, vss, iss, css, sss = _group(
        x_sample, pos_s, state_conv[l], state_ssm[l], False, attend_s, *wts)
    st = lambda a: a[None]
    return (y_p, y_s, st(kp), st(vp), st(ip), st(cp), st(ssp), st(kss), st(vss), st(iss), st(css), st(sss))
```

```python
import functools
import math

import jax
import jax.numpy as jnp
import numpy as np
from jax import lax
from jax.experimental import pallas as pl
from jax.experimental.pallas import tpu as pltpu

D_MODEL = 4096
PAGE_SIZE = 128
D_HEAD = 128
H_GDN = 16
H_ATT = 16
D_GDN = H_GDN * D_HEAD
D_ATT = H_ATT * D_HEAD
CONV_W = 4
CONV_DIM = 3 * D_GDN
GDN_CHUNK = 64
H_IDX = 32
D_IDX = 128
TOPK_ATT_MAX = 256
Q_BLOCK = 128
ROPE_THETA = 10000.0
LN_EPS = 1e-5
RMS_EPS = 1e-6
PEER_HEADS = 8
PEER_NKEYS = 128
PEER_DQ = 256
PEER_TOPK = 16
PEER_BLOCK = 128
DEPTH = 1
DEEPNORM_ALPHA = (2.0 * DEPTH) ** 0.25
IN_SPLITS = (CONV_DIM, D_GDN, H_GDN, H_GDN, D_ATT, D_ATT, D_ATT, H_IDX * D_IDX, H_IDX, D_IDX, D_MODEL, D_MODEL)
IN_OFFSETS = tuple(int(o) for o in np.cumsum(IN_SPLITS)[:-1])

VMEM_LIMIT = 48 * 1024 * 1024


def _mm_kernel(x_ref, w_ref, o_ref, acc_ref):
    @pl.when(pl.program_id(2) == 0)
    def _():
        acc_ref[...] = jnp.zeros_like(acc_ref)

    acc_ref[...] += jnp.dot(x_ref[...], w_ref[...], preferred_element_type=jnp.float32)

    @pl.when(pl.program_id(2) == pl.num_programs(2) - 1)
    def _():
        o_ref[...] = acc_ref[...]


def _pick(n, cands):
    for c in cands:
        if n % c == 0:
            return c
    return n


def _mm(x, w):
    m, k = x.shape
    n = w.shape[1]
    tm = _pick(m, (1024, 512, 256, 128))
    tn = _pick(n, (1280, 1024, 768, 512, 256, 128))
    tk = _pick(k, (1024, 512, 256, 128))
    return pl.pallas_call(
        _mm_kernel,
        grid=(m // tm, n // tn, k // tk),
        in_specs=[pl.BlockSpec((tm, tk), lambda i, j, l: (i, l)),
                  pl.BlockSpec((tk, tn), lambda i, j, l: (l, j))],
        out_specs=pl.BlockSpec((tm, tn), lambda i, j, l: (i, j)),
        out_shape=jax.ShapeDtypeStruct((m, n), jnp.float32),
        scratch_shapes=[pltpu.VMEM((tm, tn), jnp.float32)],
        compiler_params=pltpu.CompilerParams(
            dimension_semantics=("parallel", "parallel", "arbitrary"),
            vmem_limit_bytes=VMEM_LIMIT),
        name="mm",
    )(x, w)


def _wcast(w):
    return jnp.pad(w.astype(jnp.bfloat16), ((0, 0), (0, (-w.shape[1]) % 128)))


def _mmf(x, wb, n):
    shp = x.shape
    y = _mm(x.reshape(-1, shp[-1]).astype(jnp.bfloat16), wb)[:, :n]
    return y.reshape(*shp[:-1], n)


def _layernorm(x, g, b):
    mu = jnp.mean(x, -1, keepdims=True)
    var = jnp.mean(jnp.square(x - mu), -1, keepdims=True)
    return (x - mu) * lax.rsqrt(var + LN_EPS) * g + b


def _rope(x, pos):
    half = x.shape[-1] // 2
    inv = ROPE_THETA ** (-jnp.arange(half, dtype=jnp.float32) / half)
    ang = pos.astype(jnp.float32)[:, None] * inv[None, :]
    cos = jnp.cos(ang)[None, :, None, :]
    sin = jnp.sin(ang)[None, :, None, :]
    x1 = x[..., :half]
    x2 = x[..., half:]
    return jnp.concatenate([x1 * cos - x2 * sin, x2 * cos + x1 * sin], -1)


def _l2n(a):
    return a * lax.rsqrt(jnp.sum(a * a, -1, keepdims=True) + RMS_EPS)


def _causal_conv(u, buf, conv_w):
    t = u.shape[1]
    up = jnp.concatenate([buf.astype(u.dtype), u], axis=1)
    out = sum(up[:, j:j + t] * conv_w[j] for j in range(CONV_W))
    return jax.nn.silu(out), up[:, t:]


def _gdn_qkv(conv_out):
    b, t, _ = conv_out.shape
    q, k, v = (a.reshape(b, t, H_GDN, D_HEAD) for a in jnp.split(conv_out, 3, axis=-1))
    return _l2n(q) * D_HEAD ** -0.5, _l2n(k), v


def _gdn_gates(b_raw, a_raw, a_log, dt_bias):
    beta = jax.nn.sigmoid(b_raw)
    g = -jnp.exp(a_log) * jax.nn.softplus(a_raw + dt_bias)
    return g, beta


def _gdn_chunked(q, k, v, g, beta, s0):
    b, t, h, _ = q.shape
    c = GDN_CHUNK
    n = t // c

    def ch(a):
        return jnp.swapaxes(jnp.moveaxis(a.reshape(b, n, c, h, *a.shape[3:]), 1, 0), 2, 3)

    qc, kc, vc, gc, bc = ch(q), ch(k), ch(v), ch(g), ch(beta)
    gam = jnp.cumsum(gc, axis=-1)
    incl = jnp.tril(jnp.ones((c, c), bool))
    strict = jnp.tril(jnp.ones((c, c), bool), -1)
    decay = jnp.exp(jnp.where(incl, gam[..., :, None] - gam[..., None, :], -jnp.inf))
    kb = kc * bc[..., None]
    a_mat = jnp.where(strict, jnp.einsum('nbhid,nbhjd->nbhij', kb, kc) * decay, 0.0) + jnp.eye(c, dtype=jnp.float32)
    solve = functools.partial(lax.linalg.triangular_solve, left_side=True, lower=True, unit_diagonal=True)
    u = solve(a_mat, vc * bc[..., None])
    w = solve(a_mat, kb * jnp.exp(gam)[..., None])
    qk = jnp.einsum('nbhid,nbhjd->nbhij', qc, kc) * decay

    def step(s, inp):
        q_c, k_c, u_c, w_c, g_c, qk_c = inp
        v_new = u_c - jnp.einsum('bhcd,bhde->bhce', w_c, s)
        o = jnp.einsum('bhcd,bhde->bhce', q_c * jnp.exp(g_c)[..., None], s) + jnp.einsum('bhij,bhje->bhie', qk_c, v_new)
        g_last = g_c[..., -1:]
        s = s * jnp.exp(g_last)[..., None] + jnp.einsum('bhcd,bhce->bhde', k_c * jnp.exp(g_last - g_c)[..., None], v_new)
        return s, o

    s, o = lax.scan(step, s0, (qc, kc, u, w, gam, qk))
    o = jnp.moveaxis(jnp.swapaxes(o, 2, 3), 0, 1).reshape(b, t, h, -1)
    return o, s


def _gdn_recurrent(q, k, v, g, beta, s0):
    def step(s, inp):
        q_t, k_t, v_t, g_t, b_t = inp
        s = s * jnp.exp(g_t)[..., None, None]
        ks = jnp.einsum('bhk,bhkv->bhv', k_t, s)
        s = s + jnp.einsum('bhk,bhv->bhkv', k_t * b_t[..., None], v_t - ks)
        return s, jnp.einsum('bhk,bhkv->bhv', q_t, s)

    tm = lambda a: jnp.moveaxis(a, 1, 0)
    s, o = lax.scan(step, s0, (tm(q), tm(k), tm(v), tm(g), tm(beta)))
    return jnp.moveaxis(o, 0, 1), s


def _gdn_out(o, z, norm_w):
    b, t = z.shape[:2]
    o = o * lax.rsqrt(jnp.mean(o * o, -1, keepdims=True) + RMS_EPS) * norm_w
    return o.reshape(b, t, D_GDN) * jax.nn.silu(z)


def _att_heads(q_att, k_att, v_att, q_idx, w_idx, k_idx, pos):
    b, t, _ = q_att.shape
    qa = _rope(q_att.reshape(b, t, H_ATT, D_HEAD), pos)
    ka = _rope(k_att.reshape(b, t, H_ATT, D_HEAD), pos)
    va = v_att.reshape(b, t, H_ATT, D_HEAD)
    qi = _rope(q_idx.reshape(b, t, H_IDX, D_IDX), pos)
    ki = _rope(k_idx[:, :, None, :], pos)[:, :, 0]
    wi = w_idx * (H_IDX ** -0.5 * D_IDX ** -0.5)
    return qa, ka, va, qi, wi, ki


def _indexer_topk(qi, wi, ki, q_pos, n_sel):
    rel = jax.nn.relu(jnp.einsum('bthd,bsd->bths', qi, ki))
    score = jnp.einsum('bths,bth->bts', rel, wi).astype(jnp.float32)
    key_pos = jnp.arange(ki.shape[1])
    score = jnp.where(key_pos[None, None, :] <= q_pos[None, :, None], score, -jnp.inf)
    _, sel = lax.top_k(score, n_sel)
    return sel


def _sparse_attend(q, k_sel, v_sel, sel, q_pos):
    logits = jnp.einsum('bthd,btkhd->bthk', q, k_sel).astype(jnp.float32) * D_HEAD ** -0.5
    valid = (sel <= q_pos[None, :, None])[:, :, None, :]
    p = jax.nn.softmax(jnp.where(valid, logits, -jnp.inf), axis=-1)
    return jnp.einsum('bthk,btkhd->bthd', p.astype(v_sel.dtype), v_sel)


DSA_TQ = 128
DSA_TK = 256
INT32_MIN = np.int32(-2 ** 31)


def _dsa_prompt_kernel(ii_ref, jj_ref, qi_ref, w_ref, ki_ref, q_ref, k_ref, vt_ref, o_ref,
                       xs_ref, thr_ref, bias_ref, m_ref, l_ref, acc_ref, *, n_sel):
    s_id = pl.program_id(0)
    i = ii_ref[s_id]
    j = jj_ref[s_id]
    last = (i * DSA_TQ + DSA_TQ - 1) // DSA_TK
    nch = last + 1
    kiota = lax.broadcasted_iota(jnp.int32, (DSA_TK, DSA_TQ), 0)
    qpos = i * DSA_TQ + lax.broadcasted_iota(jnp.int32, (DSA_TK, DSA_TQ), 1)

    @pl.when(j == 0)
    def _index():
        def score_body(c, carry):
            r0 = pl.multiple_of(c * DSA_TK, DSA_TK)
            kchunk = ki_ref[pl.ds(r0, DSA_TK), :]
            sc = jnp.zeros((DSA_TK, DSA_TQ), jnp.float32)
            for hp in range(H_IDX // 2):
                z = jnp.dot(kchunk, qi_ref[0, hp], preferred_element_type=jnp.float32)
                sc = sc + w_ref[0, 2 * hp:2 * hp + 1, :] * jnp.maximum(z[:, :DSA_TQ], 0.0)
                sc = sc + w_ref[0, 2 * hp + 1:2 * hp + 2, :] * jnp.maximum(z[:, DSA_TQ:], 0.0)
            bits = lax.bitcast_convert_type(sc + 0.0, jnp.int32)
            key = bits ^ ((bits >> 31) & jnp.int32(0x7FFFFFFF))
            xs_ref[pl.ds(r0, DSA_TK), :] = jnp.where(r0 + kiota <= qpos, key, INT32_MIN)
            return carry

        lax.fori_loop(0, nch, score_body, 0)

        def count_ge(cand):
            def body(c, acc):
                r0 = pl.multiple_of(c * DSA_TK, DSA_TK)
                ind = jnp.where(xs_ref[pl.ds(r0, DSA_TK), :] >= cand, 1, 0).astype(jnp.int32)
                return acc + jnp.sum(ind.reshape(DSA_TK // 8, 8, DSA_TQ), axis=0)

            acc = lax.fori_loop(0, nch, body, jnp.zeros((8, DSA_TQ), jnp.int32))
            return jnp.sum(acc, axis=0, keepdims=True)

        def bit_cond(st):
            b, _, open_ = st
            return jnp.logical_and(b < 32, jnp.max(open_) > 0)

        def bit_body(st):
            b, thr_u, open_ = st
            cand_u = thr_u | lax.shift_left(jnp.int32(1), 31 - b)
            cnt = count_ge(cand_u ^ INT32_MIN)
            take = jnp.logical_and(cnt >= n_sel, open_ > 0)
            return b + 1, jnp.where(take, cand_u, thr_u), jnp.where(cnt == n_sel, 0, open_)

        _, thr_u, _ = lax.while_loop(
            bit_cond, bit_body,
            (jnp.int32(0), jnp.zeros((1, DSA_TQ), jnp.int32), jnp.ones((1, DSA_TQ), jnp.int32)))
        thr_ref[...] = thr_u ^ INT32_MIN
        m_ref[...] = jnp.full(m_ref.shape, -1e30, jnp.float32)
        l_ref[...] = jnp.zeros(l_ref.shape, jnp.float32)
        acc_ref[...] = jnp.zeros(acc_ref.shape, jnp.float32)

    r0 = pl.multiple_of(j * DSA_TK, DSA_TK)
    sel = jnp.where(xs_ref[pl.ds(r0, DSA_TK), :] >= thr_ref[...], r0 + kiota, jnp.int32(2 ** 30)) <= qpos
    bias_ref[...] = jnp.where(sel, 0.0, -jnp.inf)
    for h in range(H_ATT):
        s = jnp.dot(k_ref[:, h * D_HEAD:(h + 1) * D_HEAD], q_ref[h], preferred_element_type=jnp.float32)
        s = s + bias_ref[...]
        m_old = m_ref[h:h + 1, :]
        m_new = jnp.maximum(m_old, jnp.max(s, axis=0, keepdims=True))
        alpha = jnp.exp(m_old - m_new)
        p = jnp.exp(s - m_new)
        l_ref[h:h + 1, :] = alpha * l_ref[h:h + 1, :] + jnp.sum(p, axis=0, keepdims=True)
        m_ref[h:h + 1, :] = m_new
        acc_ref[h] = alpha * acc_ref[h] + jnp.dot(vt_ref[h], p.astype(jnp.bfloat16),
                                                  preferred_element_type=jnp.float32)

    @pl.when(j == last)
    def _finish():
        for h in range(H_ATT):
            o_ref[:, h * D_HEAD:(h + 1) * D_HEAD] = (acc_ref[h] / l_ref[h:h + 1, :]).T


def _dsa_prompt_one(q, k, v, qi, wi, ki):
    t = q.shape[0]
    assert t % DSA_TK == 0 and DSA_TK % DSA_TQ == 0
    nq = t // DSA_TQ
    n_sel = min(TOPK_ATT_MAX, t // 4)
    bf = jnp.bfloat16
    qit = qi.reshape(nq, DSA_TQ, H_IDX // 2, 2, D_IDX).transpose(0, 2, 4, 3, 1).reshape(
        nq, H_IDX // 2, D_IDX, 2 * DSA_TQ).astype(bf)
    wr = wi.reshape(nq, DSA_TQ, H_IDX).transpose(0, 2, 1)
    qt = (q * D_HEAD ** -0.5).transpose(1, 2, 0).astype(bf)
    kf = k.reshape(t, D_ATT).astype(bf)
    vt = v.transpose(1, 2, 0).astype(bf)
    steps = [(i, j) for i in range(nq) for j in range((i * DSA_TQ + DSA_TQ - 1) // DSA_TK + 1)]
    ii = jnp.asarray([s[0] for s in steps], jnp.int32)
    jj = jnp.asarray([s[1] for s in steps], jnp.int32)
    grid_spec = pltpu.PrefetchScalarGridSpec(
        num_scalar_prefetch=2,
        grid=(len(steps),),
        in_specs=[
            pl.BlockSpec((1, H_IDX // 2, D_IDX, 2 * DSA_TQ), lambda s, ii, jj: (ii[s], 0, 0, 0)),
            pl.BlockSpec((1, H_IDX, DSA_TQ), lambda s, ii, jj: (ii[s], 0, 0)),
            pl.BlockSpec((t, D_IDX), lambda s, ii, jj: (0, 0)),
            pl.BlockSpec((H_ATT, D_HEAD, DSA_TQ), lambda s, ii, jj: (0, 0, ii[s])),
            pl.BlockSpec((DSA_TK, D_ATT), lambda s, ii, jj: (jj[s], 0)),
            pl.BlockSpec((H_ATT, D_HEAD, DSA_TK), lambda s, ii, jj: (0, 0, jj[s])),
        ],
        out_specs=pl.BlockSpec((DSA_TQ, D_ATT), lambda s, ii, jj: (ii[s], 0)),
        scratch_shapes=[
            pltpu.VMEM((t, DSA_TQ), jnp.int32),
            pltpu.VMEM((1, DSA_TQ), jnp.int32),
            pltpu.VMEM((DSA_TK, DSA_TQ), jnp.float32),
            pltpu.VMEM((H_ATT, DSA_TQ), jnp.float32),
            pltpu.VMEM((H_ATT, DSA_TQ), jnp.float32),
            pltpu.VMEM((H_ATT, D_HEAD, DSA_TQ), jnp.float32),
        ])
    return pl.pallas_call(
        functools.partial(_dsa_prompt_kernel, n_sel=n_sel),
        grid_spec=grid_spec,
        out_shape=jax.ShapeDtypeStruct((t, D_ATT), jnp.float32),
        compiler_params=pltpu.CompilerParams(dimension_semantics=("arbitrary",), vmem_limit_bytes=VMEM_LIMIT),
        name="dsa_prompt",
    )(ii, jj, qit, wr, ki.astype(bf), qt, kf, vt)


def _dsa_prompt(q, k, v, qi, wi, ki):
    return jnp.stack([_dsa_prompt_one(q[b], k[b], v[b], qi[b], wi[b], ki[b]) for b in range(q.shape[0])])


def _dsa_sample(q, k_new, v_new, qi, wi, ki_new, cache_k, cache_v, cache_idx_k, page_table, layer):
    db, t = q.shape[:2]
    past_len = page_table.shape[1] * PAGE_SIZE
    n_sel = min(TOPK_ATT_MAX, (past_len + t) // 4)
    ki_past = cache_idx_k[layer, page_table].reshape(db, past_len, D_IDX)
    ki_all = jnp.concatenate([ki_past, ki_new.astype(ki_past.dtype)], axis=1)
    q_pos = past_len + jnp.arange(t)
    sel = _indexer_topk(qi, wi, ki_all, q_pos, n_sel)
    past_i = jnp.minimum(sel, past_len - 1)
    phys = jnp.take_along_axis(page_table, (past_i // PAGE_SIZE).reshape(db, -1), axis=1).reshape(sel.shape)
    slot = past_i % PAGE_SIZE
    new_i = jnp.clip(sel - past_len, 0, t - 1)
    is_new = (sel >= past_len)[..., None, None]
    gather = jax.vmap(lambda a, i: a[i])
    k_sel = jnp.where(is_new, gather(k_new, new_i).astype(cache_k.dtype), cache_k[layer, phys, slot])
    v_sel = jnp.where(is_new, gather(v_new, new_i).astype(cache_v.dtype), cache_v[layer, phys, slot])
    return _sparse_attend(q, k_sel, v_sel, sel, q_pos).reshape(db, t, D_ATT)


def _peer(x, wq, k1, k2, u, v):
    shape = x.shape
    xf = x.reshape(-1, D_MODEL)
    n = xf.shape[0]
    pad = (-n) % PEER_BLOCK
    xf = jnp.pad(xf, ((0, pad), (0, 0)))
    xb16 = xf.astype(jnp.bfloat16)
    q = _mm(xb16, wq).reshape(-1, PEER_HEADS, 2, PEER_DQ // 2)
    s1 = jnp.einsum('phd,hkd->phk', q[:, :, 0], k1).astype(jnp.float32)
    s2 = jnp.einsum('phd,hkd->phk', q[:, :, 1], k2).astype(jnp.float32)
    v1, i1 = lax.top_k(s1, PEER_TOPK)
    v2, i2 = lax.top_k(s2, PEER_TOPK)
    cand = (v1[..., :, None] + v2[..., None, :]).reshape(-1, PEER_HEADS, PEER_TOPK * PEER_TOPK)
    top_s, pos = lax.top_k(cand, PEER_TOPK)
    e1 = jnp.take_along_axis(i1, pos // PEER_TOPK, axis=-1).reshape(-1, PEER_HEADS * PEER_TOPK)
    e2 = jnp.take_along_axis(i2, pos % PEER_TOPK, axis=-1).reshape(-1, PEER_HEADS * PEER_TOPK)
    gate = jax.nn.softmax(top_s, axis=-1).reshape(-1, PEER_HEADS * PEER_TOPK)
    wmat = _peer_gate_matrix(e1.astype(jnp.int32), e2.astype(jnp.int32), gate)
    out = _peer_dense(xb16, u, v, wmat)
    return out[:n].reshape(shape)


def _peer_gate_kernel(i1_ref, i2_ref, g_ref, w_ref):
    iota = lax.broadcasted_iota(jnp.int32, (PEER_NKEYS, PEER_HEADS * PEER_TOPK), 0)

    def body(t, carry):
        a = jnp.where(iota == i1_ref[pl.ds(t, 1), :], g_ref[pl.ds(t, 1), :], 0.0).astype(jnp.bfloat16)
        b = jnp.where(iota == i2_ref[pl.ds(t, 1), :], 1.0, 0.0).astype(jnp.bfloat16)
        w = lax.dot_general(a, b, (((1,), (1,)), ((), ())), preferred_element_type=jnp.float32)
        w_ref[t] = w.astype(w_ref.dtype)
        return carry

    lax.fori_loop(0, i1_ref.shape[0], body, 0, unroll=8)


def _peer_gate_matrix(e1, e2, gate):
    n, p = e1.shape
    tw = _pick(n, (128,))
    spec = pl.BlockSpec((tw, p), lambda i: (i, 0))
    w3 = pl.pallas_call(
        _peer_gate_kernel,
        grid=(n // tw,),
        in_specs=[spec, spec, spec],
        out_specs=pl.BlockSpec((tw, PEER_NKEYS, PEER_NKEYS), lambda i: (i, 0, 0)),
        out_shape=jax.ShapeDtypeStruct((n, PEER_NKEYS, PEER_NKEYS), jnp.bfloat16),
        compiler_params=pltpu.CompilerParams(dimension_semantics=("parallel",), vmem_limit_bytes=VMEM_LIMIT),
        name="peer_gate",
    )(e1, e2, gate)
    return w3.reshape(n, PEER_NKEYS * PEER_NKEYS)


def _peer_dense_kernel(h_ref, ut_ref, w_ref, v_ref, o_ref):
    @pl.when(pl.program_id(1) == 0)
    def _():
        o_ref[...] = jnp.zeros_like(o_ref)

    half = ut_ref.shape[1] // 2
    upd = None
    for s in range(2):
        z = jnp.dot(h_ref[...], ut_ref[:, s * half:(s + 1) * half], preferred_element_type=jnp.float32)
        act = 0.5 * z * (1.0 + lax.erf(z * (2.0 ** -0.5)))
        c = (w_ref[:, s * half:(s + 1) * half].astype(jnp.float32) * act).astype(jnp.bfloat16)
        d = jnp.dot(c, v_ref[s * half:(s + 1) * half, :], preferred_element_type=jnp.float32)
        upd = d if upd is None else upd + d
    o_ref[...] += upd


def _peer_dense(xb16, ut, vb, wmat):
    n = xb16.shape[0]
    e = vb.shape[0]
    tm = _pick(n, (512, 256, 128))
    te = 512
    return pl.pallas_call(
        _peer_dense_kernel,
        grid=(n // tm, e // te),
        in_specs=[pl.BlockSpec((tm, D_MODEL), lambda i, j: (i, 0)),
                  pl.BlockSpec((D_MODEL, te), lambda i, j: (0, j)),
                  pl.BlockSpec((tm, te), lambda i, j: (i, j)),
                  pl.BlockSpec((te, D_MODEL), lambda i, j: (j, 0))],
        out_specs=pl.BlockSpec((tm, D_MODEL), lambda i, j: (i, 0)),
        out_shape=jax.ShapeDtypeStruct((n, D_MODEL), jnp.float32),
        compiler_params=pltpu.CompilerParams(dimension_semantics=("parallel", "arbitrary"),
                                             vmem_limit_bytes=56 * 1024 * 1024),
        name="peer_dense",
    )(xb16, ut, wmat, vb)


def _group(x, pos, conv_buf, ssm0, chunked, attend, w_in, conv_w, a_log, dt_bias, gdn_norm_w,
           w_br_gdn, w_br_att, w_out, ln1_g, ln1_b, peer_wq, peer_k1, peer_k2, peer_u, peer_v, ln2_g, ln2_b):
    (qkv, z, b_raw, a_raw, q_att, k_att, v_att, q_idx, w_idx, k_idx, gate_gdn, gate_att) = jnp.split(
        _mmf(x, w_in, sum(IN_SPLITS)), IN_OFFSETS, axis=-1)
    conv_out, conv_new = _causal_conv(qkv, conv_buf, conv_w)
    q, k, v = _gdn_qkv(conv_out)
    g, beta = _gdn_gates(b_raw, a_raw, a_log, dt_bias)
    gdn = _gdn_chunked if chunked else _gdn_recurrent
    o, ssm_new = gdn(q, k, v, g, beta, ssm0.astype(jnp.float32))
    o_gdn = _gdn_out(o, z, gdn_norm_w)
    qa, ka, va, qi, wi, ki = _att_heads(q_att, k_att, v_att, q_idx, w_idx, k_idx, pos)
    o_att = attend(qa, ka, va, qi, wi, ki)
    mix = _mmf(jax.nn.sigmoid(gate_gdn) * _mmf(o_gdn, w_br_gdn, D_MODEL) + jax.nn.sigmoid(gate_att) * _mmf(o_att, w_br_att, D_MODEL),
               w_out, D_MODEL)
    h = _layernorm(DEEPNORM_ALPHA * x + mix, ln1_g, ln1_b)
    y = _layernorm(DEEPNORM_ALPHA * h + _peer(h, peer_wq, peer_k1, peer_k2, peer_u, peer_v), ln2_g, ln2_b)
    return y, ka, va, ki, conv_new, ssm_new.astype(x.dtype)


def kernel(x_prompt, x_sample, cache_k, cache_v, cache_idx_k, state_conv, state_ssm, page_table, w_in, conv_w, a_log, dt_bias, gdn_norm_w, w_br_gdn, w_br_att, w_out, ln1_g, ln1_b, peer_wq, peer_k1, peer_k2, peer_u, peer_v, ln2_g, ln2_b):
    bp, sp = x_prompt.shape[:2]
    ds = x_sample.shape[1]
    past_len = page_table.shape[1] * PAGE_SIZE
    pos_p = jnp.arange(sp)
    pos_s = past_len + jnp.arange(ds)
    l = 0
    wts = (_wcast(w_in[l]), conv_w[l], a_log[l], dt_bias[l], gdn_norm_w[l], _wcast(w_br_gdn[l]), _wcast(w_br_att[l]),
           _wcast(w_out[l]), ln1_g[l], ln1_b[l], _wcast(peer_wq[l]), peer_k1[l], peer_k2[l],
           peer_u[l].astype(jnp.bfloat16).T, peer_v[l].astype(jnp.bfloat16), ln2_g[l], ln2_b[l])
    y_p, kp, vp, ip, cp, ssp = _group(
        x_prompt, pos_p, jnp.zeros((bp, CONV_W - 1, CONV_DIM), x_prompt.dtype),
        jnp.zeros((bp, H_GDN, D_HEAD, D_HEAD), jnp.float32), True, _dsa_prompt, *wts)
    attend_s = functools.partial(_dsa_sample, cache_k=cache_k, cache_v=cache_v, cache_idx_k=cache_idx_k,
                                 page_table=page_table, layer=l)
    y_s, kss, vss, iss, css, sss = _group(
        x_sample, pos_s, state_conv[l], state_ssm[l], False, attend_s, *wts)
    st = lambda a: a[None]
    return (y_p, y_s, st(kp), st(vp), st(ip), st(cp), st(ssp), st(kss), st(vss), st(iss), st(css), st(sss))
```

```python
import functools
import math

import jax
import jax.numpy as jnp
import numpy as np
from jax import lax
from jax.experimental import pallas as pl
from jax.experimental.pallas import tpu as pltpu

D_MODEL = 4096
PAGE_SIZE = 128
D_HEAD = 128
H_GDN = 16
H_ATT = 16
D_GDN = H_GDN * D_HEAD
D_ATT = H_ATT * D_HEAD
CONV_W = 4
CONV_DIM = 3 * D_GDN
GDN_CHUNK = 64
H_IDX = 32
D_IDX = 128
TOPK_ATT_MAX = 256
Q_BLOCK = 128
ROPE_THETA = 10000.0
LN_EPS = 1e-5
RMS_EPS = 1e-6
PEER_HEADS = 8
PEER_NKEYS = 128
PEER_DQ = 256
PEER_TOPK = 16
PEER_BLOCK = 128
DEPTH = 1
DEEPNORM_ALPHA = (2.0 * DEPTH) ** 0.25
IN_SPLITS = (CONV_DIM, D_GDN, H_GDN, H_GDN, D_ATT, D_ATT, D_ATT, H_IDX * D_IDX, H_IDX, D_IDX, D_MODEL, D_MODEL)
IN_OFFSETS = tuple(int(o) for o in np.cumsum(IN_SPLITS)[:-1])

VMEM_LIMIT = 48 * 1024 * 1024


def _mm_kernel(x_ref, w_ref, o_ref, acc_ref):
    @pl.when(pl.program_id(2) == 0)
    def _():
        acc_ref[...] = jnp.zeros_like(acc_ref)

    acc_ref[...] += jnp.dot(x_ref[...], w_ref[...], preferred_element_type=jnp.float32)

    @pl.when(pl.program_id(2) == pl.num_programs(2) - 1)
    def _():
        o_ref[...] = acc_ref[...]


def _pick(n, cands):
    for c in cands:
        if n % c == 0:
            return c
    return n


def _mm(x, w):
    m, k = x.shape
    n = w.shape[1]
    tm = _pick(m, (1024, 512, 256, 128))
    tn = _pick(n, (1280, 1024, 768, 512, 256, 128))
    tk = _pick(k, (1024, 512, 256, 128))
    return pl.pallas_call(
        _mm_kernel,
        grid=(m // tm, n // tn, k // tk),
        in_specs=[pl.BlockSpec((tm, tk), lambda i, j, l: (i, l)),
                  pl.BlockSpec((tk, tn), lambda i, j, l: (l, j))],
        out_specs=pl.BlockSpec((tm, tn), lambda i, j, l: (i, j)),
        out_shape=jax.ShapeDtypeStruct((m, n), jnp.float32),
        scratch_shapes=[pltpu.VMEM((tm, tn), jnp.float32)],
        compiler_params=pltpu.CompilerParams(
            dimension_semantics=("parallel", "parallel", "arbitrary"),
            vmem_limit_bytes=VMEM_LIMIT),
        name="mm",
    )(x, w)


def _wcast(w):
    return jnp.pad(w.astype(jnp.bfloat16), ((0, 0), (0, (-w.shape[1]) % 128)))


def _mmf(x, wb, n):
    shp = x.shape
    y = _mm(x.reshape(-1, shp[-1]).astype(jnp.bfloat16), wb)[:, :n]
    return y.reshape(*shp[:-1], n)


def _layernorm(x, g, b):
    mu = jnp.mean(x, -1, keepdims=True)
    var = jnp.mean(jnp.square(x - mu), -1, keepdims=True)
    return (x - mu) * lax.rsqrt(var + LN_EPS) * g + b


def _rope(x, pos):
    half = x.shape[-1] // 2
    inv = ROPE_THETA ** (-jnp.arange(half, dtype=jnp.float32) / half)
    ang = pos.astype(jnp.float32)[:, None] * inv[None, :]
    cos = jnp.cos(ang)[None, :, None, :]
    sin = jnp.sin(ang)[None, :, None, :]
    x1 = x[..., :half]
    x2 = x[..., half:]
    return jnp.concatenate([x1 * cos - x2 * sin, x2 * cos + x1 * sin], -1)


def _l2n(a):
    return a * lax.rsqrt(jnp.sum(a * a, -1, keepdims=True) + RMS_EPS)


def _causal_conv(u, buf, conv_w):
    t = u.shape[1]
    up = jnp.concatenate([buf.astype(u.dtype), u], axis=1)
    out = sum(up[:, j:j + t] * conv_w[j] for j in range(CONV_W))
    return jax.nn.silu(out), up[:, t:]


def _gdn_qkv(conv_out):
    b, t, _ = conv_out.shape
    q, k, v = (a.reshape(b, t, H_GDN, D_HEAD) for a in jnp.split(conv_out, 3, axis=-1))
    return _l2n(q) * D_HEAD ** -0.5, _l2n(k), v


def _gdn_gates(b_raw, a_raw, a_log, dt_bias):
    beta = jax.nn.sigmoid(b_raw)
    g = -jnp.exp(a_log) * jax.nn.softplus(a_raw + dt_bias)
    return g, beta


def _gdn_chunked(q, k, v, g, beta, s0):
    b, t, h, _ = q.shape
    c = GDN_CHUNK
    n = t // c

    def ch(a):
        return jnp.swapaxes(jnp.moveaxis(a.reshape(b, n, c, h, *a.shape[3:]), 1, 0), 2, 3)

    qc, kc, vc, gc, bc = ch(q), ch(k), ch(v), ch(g), ch(beta)
    gam = jnp.cumsum(gc, axis=-1)
    incl = jnp.tril(jnp.ones((c, c), bool))
    strict = jnp.tril(jnp.ones((c, c), bool), -1)
    decay = jnp.exp(jnp.where(incl, gam[..., :, None] - gam[..., None, :], -jnp.inf))
    kb = kc * bc[..., None]
    a_mat = jnp.where(strict, jnp.einsum('nbhid,nbhjd->nbhij', kb, kc) * decay, 0.0) + jnp.eye(c, dtype=jnp.float32)
    solve = functools.partial(lax.linalg.triangular_solve, left_side=True, lower=True, unit_diagonal=True)
    u = solve(a_mat, vc * bc[..., None])
    w = solve(a_mat, kb * jnp.exp(gam)[..., None])
    qk = jnp.einsum('nbhid,nbhjd->nbhij', qc, kc) * decay

    def step(s, inp):
        q_c, k_c, u_c, w_c, g_c, qk_c = inp
        v_new = u_c - jnp.einsum('bhcd,bhde->bhce', w_c, s)
        o = jnp.einsum('bhcd,bhde->bhce', q_c * jnp.exp(g_c)[..., None], s) + jnp.einsum('bhij,bhje->bhie', qk_c, v_new)
        g_last = g_c[..., -1:]
        s = s * jnp.exp(g_last)[..., None] + jnp.einsum('bhcd,bhce->bhde', k_c * jnp.exp(g_last - g_c)[..., None], v_new)
        return s, o

    s, o = lax.scan(step, s0, (qc, kc, u, w, gam, qk))
    o = jnp.moveaxis(jnp.swapaxes(o, 2, 3), 0, 1).reshape(b, t, h, -1)
    return o, s


def _gdn_recurrent(q, k, v, g, beta, s0):
    def step(s, inp):
        q_t, k_t, v_t, g_t, b_t = inp
        s = s * jnp.exp(g_t)[..., None, None]
        ks = jnp.einsum('bhk,bhkv->bhv', k_t, s)
        s = s + jnp.einsum('bhk,bhv->bhkv', k_t * b_t[..., None], v_t - ks)
        return s, jnp.einsum('bhk,bhkv->bhv', q_t, s)

    tm = lambda a: jnp.moveaxis(a, 1, 0)
    s, o = lax.scan(step, s0, (tm(q), tm(k), tm(v), tm(g), tm(beta)))
    return jnp.moveaxis(o, 0, 1), s


def _gdn_out(o, z, norm_w):
    b, t = z.shape[:2]
    o = o * lax.rsqrt(jnp.mean(o * o, -1, keepdims=True) + RMS_EPS) * norm_w
    return o.reshape(b, t, D_GDN) * jax.nn.silu(z)


def _att_heads(q_att, k_att, v_att, q_idx, w_idx, k_idx, pos):
    b, t, _ = q_att.shape
    qa = _rope(q_att.reshape(b, t, H_ATT, D_HEAD), pos)
    ka = _rope(k_att.reshape(b, t, H_ATT, D_HEAD), pos)
    va = v_att.reshape(b, t, H_ATT, D_HEAD)
    qi = _rope(q_idx.reshape(b, t, H_IDX, D_IDX), pos)
    ki = _rope(k_idx[:, :, None, :], pos)[:, :, 0]
    wi = w_idx * (H_IDX ** -0.5 * D_IDX ** -0.5)
    return qa, ka, va, qi, wi, ki


def _indexer_topk(qi, wi, ki, q_pos, n_sel):
    rel = jax.nn.relu(jnp.einsum('bthd,bsd->bths', qi, ki))
    score = jnp.einsum('bths,bth->bts', rel, wi).astype(jnp.float32)
    key_pos = jnp.arange(ki.shape[1])
    score = jnp.where(key_pos[None, None, :] <= q_pos[None, :, None], score, -jnp.inf)
    _, sel = lax.top_k(score, n_sel)
    return sel


def _sparse_attend(q, k_sel, v_sel, sel, q_pos):
    logits = jnp.einsum('bthd,btkhd->bthk', q, k_sel).astype(jnp.float32) * D_HEAD ** -0.5
    valid = (sel <= q_pos[None, :, None])[:, :, None, :]
    p = jax.nn.softmax(jnp.where(valid, logits, -jnp.inf), axis=-1)
    return jnp.einsum('bthk,btkhd->bthd', p.astype(v_sel.dtype), v_sel)


DSA_TQ = 128
DSA_TK = 256
INT32_MIN = np.int32(-2 ** 31)


def _float_to_key(x):
    bits = lax.bitcast_convert_type(x + 0.0, jnp.int32)
    return bits ^ ((bits >> 31) & jnp.int32(0x7FFFFFFF))


def _kth_largest_key(xs_ref, nch, ch, n_sel):
    lanes = xs_ref.shape[1]

    def count_ge(cand):
        def body(c, acc):
            r0 = pl.multiple_of(c * ch, ch)
            ind = jnp.where(xs_ref[pl.ds(r0, ch), :] >= cand, 1, 0).astype(jnp.int32)
            return acc + jnp.sum(ind.reshape(ch // 8, 8, lanes), axis=0)

        acc = lax.fori_loop(0, nch, body, jnp.zeros((8, lanes), jnp.int32))
        return jnp.sum(acc, axis=0, keepdims=True)

    def bit_cond(st):
        b, _, open_ = st
        return jnp.logical_and(b < 32, jnp.max(open_) > 0)

    def bit_body(st):
        b, thr_u, open_ = st
        cand_u = thr_u | lax.shift_left(jnp.int32(1), 31 - b)
        cnt = count_ge(cand_u ^ INT32_MIN)
        take = jnp.logical_and(cnt >= n_sel, open_ > 0)
        return b + 1, jnp.where(take, cand_u, thr_u), jnp.where(cnt == n_sel, 0, open_)

    _, thr_u, _ = lax.while_loop(
        bit_cond, bit_body,
        (jnp.int32(0), jnp.zeros((1, lanes), jnp.int32), jnp.ones((1, lanes), jnp.int32)))
    return thr_u ^ INT32_MIN


def _dsa_prompt_kernel(ii_ref, jj_ref, qi_ref, w_ref, ki_ref, q_ref, k_ref, vt_ref, o_ref,
                       xs_ref, thr_ref, bias_ref, m_ref, l_ref, acc_ref, *, n_sel):
    s_id = pl.program_id(0)
    i = ii_ref[s_id]
    j = jj_ref[s_id]
    last = (i * DSA_TQ + DSA_TQ - 1) // DSA_TK
    nch = last + 1
    kiota = lax.broadcasted_iota(jnp.int32, (DSA_TK, DSA_TQ), 0)
    qpos = i * DSA_TQ + lax.broadcasted_iota(jnp.int32, (DSA_TK, DSA_TQ), 1)

    @pl.when(j == 0)
    def _index():
        def score_body(c, carry):
            r0 = pl.multiple_of(c * DSA_TK, DSA_TK)
            kchunk = ki_ref[pl.ds(r0, DSA_TK), :]
            sc = jnp.zeros((DSA_TK, DSA_TQ), jnp.float32)
            for hp in range(H_IDX // 2):
                z = jnp.dot(kchunk, qi_ref[0, hp], preferred_element_type=jnp.float32)
                sc = sc + w_ref[0, 2 * hp:2 * hp + 1, :] * jnp.maximum(z[:, :DSA_TQ], 0.0)
                sc = sc + w_ref[0, 2 * hp + 1:2 * hp + 2, :] * jnp.maximum(z[:, DSA_TQ:], 0.0)
            xs_ref[pl.ds(r0, DSA_TK), :] = jnp.where(r0 + kiota <= qpos, _float_to_key(sc), INT32_MIN)
            return carry

        lax.fori_loop(0, nch, score_body, 0)

        thr_ref[...] = _kth_largest_key(xs_ref, nch, DSA_TK, n_sel)
        m_ref[...] = jnp.full(m_ref.shape, -1e30, jnp.float32)
        l_ref[...] = jnp.zeros(l_ref.shape, jnp.float32)
        acc_ref[...] = jnp.zeros(acc_ref.shape, jnp.float32)

    r0 = pl.multiple_of(j * DSA_TK, DSA_TK)
    sel = jnp.where(xs_ref[pl.ds(r0, DSA_TK), :] >= thr_ref[...], r0 + kiota, jnp.int32(2 ** 30)) <= qpos
    bias_ref[...] = jnp.where(sel, 0.0, -jnp.inf)
    for h in range(H_ATT):
        s = jnp.dot(k_ref[:, h * D_HEAD:(h + 1) * D_HEAD], q_ref[h], preferred_element_type=jnp.float32)
        s = s + bias_ref[...]
        m_old = m_ref[h:h + 1, :]
        m_new = jnp.maximum(m_old, jnp.max(s, axis=0, keepdims=True))
        alpha = jnp.exp(m_old - m_new)
        p = jnp.exp(s - m_new)
        l_ref[h:h + 1, :] = alpha * l_ref[h:h + 1, :] + jnp.sum(p, axis=0, keepdims=True)
        m_ref[h:h + 1, :] = m_new
        acc_ref[h] = alpha * acc_ref[h] + jnp.dot(vt_ref[h], p.astype(jnp.bfloat16),
                                                  preferred_element_type=jnp.float32)

    @pl.when(j == last)
    def _finish():
        for h in range(H_ATT):
            o_ref[:, h * D_HEAD:(h + 1) * D_HEAD] = (acc_ref[h] / l_ref[h:h + 1, :]).T


def _dsa_prompt_one(q, k, v, qi, wi, ki):
    t = q.shape[0]
    assert t % DSA_TK == 0 and DSA_TK % DSA_TQ == 0
    nq = t // DSA_TQ
    n_sel = min(TOPK_ATT_MAX, t // 4)
    bf = jnp.bfloat16
    qit = qi.reshape(nq, DSA_TQ, H_IDX // 2, 2, D_IDX).transpose(0, 2, 4, 3, 1).reshape(
        nq, H_IDX // 2, D_IDX, 2 * DSA_TQ).astype(bf)
    wr = wi.reshape(nq, DSA_TQ, H_IDX).transpose(0, 2, 1)
    qt = (q * D_HEAD ** -0.5).transpose(1, 2, 0).astype(bf)
    kf = k.reshape(t, D_ATT).astype(bf)
    vt = v.transpose(1, 2, 0).astype(bf)
    steps = [(i, j) for i in range(nq) for j in range((i * DSA_TQ + DSA_TQ - 1) // DSA_TK + 1)]
    ii = jnp.asarray([s[0] for s in steps], jnp.int32)
    jj = jnp.asarray([s[1] for s in steps], jnp.int32)
    grid_spec = pltpu.PrefetchScalarGridSpec(
        num_scalar_prefetch=2,
        grid=(len(steps),),
        in_specs=[
            pl.BlockSpec((1, H_IDX // 2, D_IDX, 2 * DSA_TQ), lambda s, ii, jj: (ii[s], 0, 0, 0)),
            pl.BlockSpec((1, H_IDX, DSA_TQ), lambda s, ii, jj: (ii[s], 0, 0)),
            pl.BlockSpec((t, D_IDX), lambda s, ii, jj: (0, 0)),
            pl.BlockSpec((H_ATT, D_HEAD, DSA_TQ), lambda s, ii, jj: (0, 0, ii[s])),
            pl.BlockSpec((DSA_TK, D_ATT), lambda s, ii, jj: (jj[s], 0)),
            pl.BlockSpec((H_ATT, D_HEAD, DSA_TK), lambda s, ii, jj: (0, 0, jj[s])),
        ],
        out_specs=pl.BlockSpec((DSA_TQ, D_ATT), lambda s, ii, jj: (ii[s], 0)),
        scratch_shapes=[
            pltpu.VMEM((t, DSA_TQ), jnp.int32),
            pltpu.VMEM((1, DSA_TQ), jnp.int32),
            pltpu.VMEM((DSA_TK, DSA_TQ), jnp.float32),
            pltpu.VMEM((H_ATT, DSA_TQ), jnp.float32),
            pltpu.VMEM((H_ATT, DSA_TQ), jnp.float32),
            pltpu.VMEM((H_ATT, D_HEAD, DSA_TQ), jnp.float32),
        ])
    return pl.pallas_call(
        functools.partial(_dsa_prompt_kernel, n_sel=n_sel),
        grid_spec=grid_spec,
        out_shape=jax.ShapeDtypeStruct((t, D_ATT), jnp.float32),
        compiler_params=pltpu.CompilerParams(dimension_semantics=("arbitrary",), vmem_limit_bytes=VMEM_LIMIT),
        name="dsa_prompt",
    )(ii, jj, qit, wr, ki.astype(bf), qt, kf, vt)


def _dsa_prompt(q, k, v, qi, wi, ki):
    return jnp.stack([_dsa_prompt_one(q[b], k[b], v[b], qi[b], wi[b], ki[b]) for b in range(q.shape[0])])


SMP_IDX_PAGES = 8
SMP_ATT_PAGES = 4
SMP_CH = 128


def _smp_score_kernel(pt_ref, *refs):
    pages = refs[:SMP_IDX_PAGES]
    qi_ref, wb_ref, o_ref = refs[SMP_IDX_PAGES:]
    nq = o_ref.shape[1]
    for g, ki_ref in enumerate(pages):
        z = lax.dot_general(qi_ref[0], ki_ref[0].astype(jnp.bfloat16), (((1,), (1,)), ((), ())),
                            preferred_element_type=jnp.float32)
        r = jnp.maximum(z, 0.0) * wb_ref[0]
        o_ref[0, :, g * PAGE_SIZE:(g + 1) * PAGE_SIZE] = jnp.sum(r.reshape(nq, H_IDX, PAGE_SIZE), axis=1)


def _smp_thr_kernel(s_ref, thr_ref, xs_ref, *, n_sel):
    nch = s_ref.shape[0] // SMP_CH

    def conv(c, carry):
        r0 = pl.multiple_of(c * SMP_CH, SMP_CH)
        xs_ref[pl.ds(r0, SMP_CH), :] = _float_to_key(s_ref[pl.ds(r0, SMP_CH), :])
        return carry

    lax.fori_loop(0, nch, conv, 0)
    key = _kth_largest_key(xs_ref, nch, SMP_CH, n_sel)
    thr_ref[...] = lax.bitcast_convert_type(key ^ ((key >> 31) & jnp.int32(0x7FFFFFFF)), jnp.float32)


def _smp_attn_kernel(pt_ref, *refs, nq):
    kp = refs[:SMP_ATT_PAGES]
    vp = refs[SMP_ATT_PAGES:2 * SMP_ATT_PAGES]
    sc_ref, scn_ref, thr_ref, q_ref, kn_ref, vn_ref, o_ref, m_ref, l_ref, acc_ref = refs[2 * SMP_ATT_PAGES:]
    pp = pl.program_id(1)

    def attend(k, v, sc):
        bias = jnp.where(sc >= thr_ref[0], 0.0, -jnp.inf)
        s = lax.dot_general(q_ref[0], k.astype(jnp.bfloat16), (((1,), (1,)), ((), ())),
                            preferred_element_type=jnp.float32)
        s = s + jnp.concatenate(
            [jnp.broadcast_to(bias[tt:tt + 1, :], (H_ATT, bias.shape[1])) for tt in range(nq)], axis=0)
        m_old = m_ref[...]
        m_new = jnp.maximum(m_old, jnp.max(s, axis=1, keepdims=True))
        alpha = jnp.exp(m_old - m_new)
        p = jnp.exp(s - m_new)
        l_ref[...] = alpha * l_ref[...] + jnp.sum(p, axis=1, keepdims=True)
        m_ref[...] = m_new
        acc_ref[...] = alpha * acc_ref[...] + jnp.dot(p.astype(jnp.bfloat16), v.astype(jnp.bfloat16),
                                                      preferred_element_type=jnp.float32)

    @pl.when(pp == 0)
    def _():
        m_ref[...] = jnp.full(m_ref.shape, -1e30, jnp.float32)
        l_ref[...] = jnp.zeros(l_ref.shape, jnp.float32)
        acc_ref[...] = jnp.zeros(acc_ref.shape, jnp.float32)
        attend(kn_ref[0], vn_ref[0], scn_ref[0])

    for g in range(SMP_ATT_PAGES):
        attend(kp[g][0], vp[g][0], sc_ref[0, :, g * PAGE_SIZE:(g + 1) * PAGE_SIZE])

    @pl.when(pp == pl.num_programs(1) - 1)
    def _():
        for tt in range(nq):
            for h in range(H_ATT):
                row = slice(tt * H_ATT + h, tt * H_ATT + h + 1)
                cols = slice(h * D_HEAD, (h + 1) * D_HEAD)
                o_ref[0, tt:tt + 1, cols] = acc_ref[row, cols] / l_ref[row, :]


def _dsa_sample(q, k_new, v_new, qi, wi, ki_new, cache_k, cache_v, cache_idx_k, page_table, layer):
    db, t = q.shape[:2]
    npg = page_table.shape[1]
    past_len = npg * PAGE_SIZE
    n_sel = min(TOPK_ATT_MAX, (past_len + t) // 4)
    assert npg % SMP_IDX_PAGES == 0 and npg % SMP_ATT_PAGES == 0 and (t * H_IDX) % 8 == 0
    bf = jnp.bfloat16
    pt = page_table.reshape(-1).astype(jnp.int32)
    n_pool = cache_k.shape[1]

    def page_spec(width, per_step, g):
        return pl.BlockSpec((1, PAGE_SIZE, width), lambda b, p, pt: (pt[b * npg + p * per_step + g], 0, 0))

    row_spec = lambda rows, width: pl.BlockSpec((1, rows, width), lambda b, p, pt: (b, 0, 0))
    score_past = pl.pallas_call(
        _smp_score_kernel,
        grid_spec=pltpu.PrefetchScalarGridSpec(
            num_scalar_prefetch=1, grid=(db, npg // SMP_IDX_PAGES),
            in_specs=[page_spec(D_IDX, SMP_IDX_PAGES, g) for g in range(SMP_IDX_PAGES)]
            + [row_spec(t * H_IDX, D_IDX), row_spec(t * H_IDX, PAGE_SIZE)],
            out_specs=pl.BlockSpec((1, t, SMP_IDX_PAGES * PAGE_SIZE), lambda b, p, pt: (b, 0, p))),
        out_shape=jax.ShapeDtypeStruct((db, t, past_len), jnp.float32),
        compiler_params=pltpu.CompilerParams(dimension_semantics=("parallel", "arbitrary"),
                                             vmem_limit_bytes=VMEM_LIMIT),
        name="smp_score",
    )(pt, *([cache_idx_k[layer]] * SMP_IDX_PAGES), qi.reshape(db, t * H_IDX, D_IDX).astype(bf),
      jnp.broadcast_to(wi.reshape(db, t * H_IDX, 1), (db, t * H_IDX, PAGE_SIZE)))

    rel = jax.nn.relu(jnp.einsum('bthd,bsd->bths', qi, ki_new))
    score_new = jnp.einsum('bths,bth->bts', rel, wi)
    score_new = jnp.where(jnp.arange(t)[None, None, :] <= jnp.arange(t)[None, :, None], score_new, -jnp.inf)
    score_new = jnp.pad(score_new, ((0, 0), (0, 0), (0, PAGE_SIZE - t)), constant_values=-jnp.inf)

    score_t = jnp.concatenate([score_past, score_new], axis=2).reshape(db * t, past_len + PAGE_SIZE).T
    thr = pl.pallas_call(
        functools.partial(_smp_thr_kernel, n_sel=n_sel),
        out_shape=jax.ShapeDtypeStruct((1, db * t), jnp.float32),
        scratch_shapes=[pltpu.VMEM(score_t.shape, jnp.int32)],
        compiler_params=pltpu.CompilerParams(vmem_limit_bytes=VMEM_LIMIT),
        name="smp_thr",
    )(score_t)
    thr_b = jnp.broadcast_to(thr.reshape(db, t, 1), (db, t, PAGE_SIZE))

    qbd = ((q * D_HEAD ** -0.5)[:, :, :, None, :] * jnp.eye(H_ATT, dtype=q.dtype)[None, None, :, :, None]).reshape(
        db, t * H_ATT, D_ATT).astype(bf)
    pad_rows = lambda a: jnp.pad(a.reshape(db, t, D_ATT), ((0, 0), (0, PAGE_SIZE - t), (0, 0)))
    out = pl.pallas_call(
        functools.partial(_smp_attn_kernel, nq=t),
        grid_spec=pltpu.PrefetchScalarGridSpec(
            num_scalar_prefetch=1, grid=(db, npg // SMP_ATT_PAGES),
            in_specs=[page_spec(D_ATT, SMP_ATT_PAGES, g) for g in range(SMP_ATT_PAGES)] * 2
            + [pl.BlockSpec((1, t, SMP_ATT_PAGES * PAGE_SIZE), lambda b, p, pt: (b, 0, p)),
               row_spec(t, PAGE_SIZE), row_spec(t, PAGE_SIZE), row_spec(H_ATT * t, D_ATT),
               row_spec(PAGE_SIZE, D_ATT), row_spec(PAGE_SIZE, D_ATT)],
            out_specs=row_spec(t, D_ATT),
            scratch_shapes=[pltpu.VMEM((H_ATT * t, 1), jnp.float32), pltpu.VMEM((H_ATT * t, 1), jnp.float32),
                            pltpu.VMEM((H_ATT * t, D_ATT), jnp.float32)]),
        out_shape=jax.ShapeDtypeStruct((db, t, D_ATT), jnp.float32),
        compiler_params=pltpu.CompilerParams(dimension_semantics=("parallel", "arbitrary"),
                                             vmem_limit_bytes=VMEM_LIMIT),
        name="smp_attn",
    )(pt, *([cache_k[layer].reshape(n_pool, PAGE_SIZE, D_ATT)] * SMP_ATT_PAGES),
      *([cache_v[layer].reshape(n_pool, PAGE_SIZE, D_ATT)] * SMP_ATT_PAGES),
      score_past, score_new, thr_b, qbd, pad_rows(k_new), pad_rows(v_new))
    return out


def _peer(x, wq, k1, k2, u, v):
    shape = x.shape
    xf = x.reshape(-1, D_MODEL)
    n = xf.shape[0]
    pad = (-n) % PEER_BLOCK
    xf = jnp.pad(xf, ((0, pad), (0, 0)))
    xb16 = xf.astype(jnp.bfloat16)
    e1, e2, gate = _peer_route(_mm(xb16, wq), k1, k2)
    wmat = _peer_gate_matrix(e1, e2, gate)
    out = _peer_dense(xb16, u, v, wmat)
    return out[:n].reshape(shape)


PEER_RT = 128
PEER_RH = 2
PEER_TOPK_LOG2 = PEER_TOPK.bit_length() - 1
assert 1 << PEER_TOPK_LOG2 == PEER_TOPK
PEER_PAIRS = tuple((i, j) for i in range(PEER_TOPK) for j in range(PEER_TOPK) if (i + 1) * (j + 1) <= PEER_TOPK)


def _extract_top(s, ids, n):
    big = jnp.int32(2 ** 30)
    vals, sel = [], []
    for _ in range(n):
        m = jnp.max(s, axis=0, keepdims=True)
        pick = jnp.min(jnp.where(s == m, ids, big), axis=0, keepdims=True)
        vals.append(m)
        sel.append(pick)
        s = jnp.where(ids == pick, -jnp.inf, s)
    return vals, sel


def _peer_route_kernel(q_ref, k1_ref, k2_ref, e1_ref, e2_ref, g_ref):
    kio = lax.broadcasted_iota(jnp.int32, (PEER_NKEYS, PEER_RT), 0)
    flat = jnp.concatenate(
        [jnp.full((1, PEER_RT), i * PEER_TOPK + j, jnp.int32) for i, j in PEER_PAIRS], axis=0)
    for hh in range(PEER_RH):
        tops = []
        for side, kref in ((0, k1_ref), (1, k2_ref)):
            lo = (2 * hh + side) * (PEER_DQ // 2)
            qs = q_ref[:, lo:lo + PEER_DQ // 2].astype(jnp.bfloat16)
            s = lax.dot_general(kref[hh], qs, (((1,), (1,)), ((), ())),
                                preferred_element_type=jnp.float32)
            tops.append(_extract_top(s, kio, PEER_TOPK))
        (v1, i1), (v2, i2) = tops
        cand = jnp.concatenate([v1[i] + v2[j] for i, j in PEER_PAIRS], axis=0)
        top_s, top_f = _extract_top(cand, flat, PEER_TOPK)
        ex = [jnp.exp(t - top_s[0]) for t in top_s]
        inv = 1.0 / sum(ex)
        for r in range(PEER_TOPK):
            fi = top_f[r] >> PEER_TOPK_LOG2
            fj = top_f[r] & (PEER_TOPK - 1)
            e1 = sum(jnp.where(fi == i, i1[i], 0) for i in range(PEER_TOPK))
            e2 = sum(jnp.where(fj == j, i2[j], 0) for j in range(PEER_TOPK))
            e1_ref[hh, r:r + 1, :] = e1
            e2_ref[hh, r:r + 1, :] = e2
            g_ref[hh, r:r + 1, :] = ex[r] * inv


def _peer_route(q, k1, k2):
    n = q.shape[0]
    assert n % PEER_RT == 0 and PEER_HEADS % PEER_RH == 0
    kspec = pl.BlockSpec((PEER_RH, PEER_NKEYS, PEER_DQ // 2), lambda i, h: (h, 0, 0))
    ospec = pl.BlockSpec((PEER_RH, PEER_TOPK, PEER_RT), lambda i, h: (h, 0, i))
    oshape = lambda dt: jax.ShapeDtypeStruct((PEER_HEADS, PEER_TOPK, n), dt)
    e1, e2, g = pl.pallas_call(
        _peer_route_kernel,
        grid=(n // PEER_RT, PEER_HEADS // PEER_RH),
        in_specs=[pl.BlockSpec((PEER_RT, PEER_RH * PEER_DQ), lambda i, h: (i, h)), kspec, kspec],
        out_specs=[ospec, ospec, ospec],
        out_shape=[oshape(jnp.int32), oshape(jnp.int32), oshape(jnp.float32)],
        compiler_params=pltpu.CompilerParams(dimension_semantics=("parallel", "parallel"),
                                             vmem_limit_bytes=VMEM_LIMIT),
        name="peer_route",
    )(q, k1.astype(jnp.bfloat16), k2.astype(jnp.bfloat16))
    tok_major = lambda a: a.reshape(PEER_HEADS * PEER_TOPK, n).T
    return tok_major(e1), tok_major(e2), tok_major(g)


def _peer_gate_kernel(i1_ref, i2_ref, g_ref, w_ref):
    iota = lax.broadcasted_iota(jnp.int32, (PEER_NKEYS, PEER_HEADS * PEER_TOPK), 0)

    def body(t, carry):
        a = jnp.where(iota == i1_ref[pl.ds(t, 1), :], g_ref[pl.ds(t, 1), :], 0.0).astype(jnp.bfloat16)
        b = jnp.where(iota == i2_ref[pl.ds(t, 1), :], 1.0, 0.0).astype(jnp.bfloat16)
        w = lax.dot_general(a, b, (((1,), (1,)), ((), ())), preferred_element_type=jnp.float32)
        w_ref[t] = w.astype(w_ref.dtype)
        return carry

    lax.fori_loop(0, i1_ref.shape[0], body, 0, unroll=8)


def _peer_gate_matrix(e1, e2, gate):
    n, p = e1.shape
    tw = _pick(n, (128,))
    spec = pl.BlockSpec((tw, p), lambda i: (i, 0))
    w3 = pl.pallas_call(
        _peer_gate_kernel,
        grid=(n // tw,),
        in_specs=[spec, spec, spec],
        out_specs=pl.BlockSpec((tw, PEER_NKEYS, PEER_NKEYS), lambda i: (i, 0, 0)),
        out_shape=jax.ShapeDtypeStruct((n, PEER_NKEYS, PEER_NKEYS), jnp.bfloat16),
        compiler_params=pltpu.CompilerParams(dimension_semantics=("parallel",), vmem_limit_bytes=VMEM_LIMIT),
        name="peer_gate",
    )(e1, e2, gate)
    return w3.reshape(n, PEER_NKEYS * PEER_NKEYS)


def _peer_dense_kernel(h_ref, ut_ref, w_ref, v_ref, o_ref):
    @pl.when(pl.program_id(1) == 0)
    def _():
        o_ref[...] = jnp.zeros_like(o_ref)

    half = ut_ref.shape[1] // 2
    upd = None
    for s in range(2):
        z = jnp.dot(h_ref[...], ut_ref[:, s * half:(s + 1) * half], preferred_element_type=jnp.float32)
        act = 0.5 * z * (1.0 + lax.erf(z * (2.0 ** -0.5)))
        c = (w_ref[:, s * half:(s + 1) * half].astype(jnp.float32) * act).astype(jnp.bfloat16)
        d = jnp.dot(c, v_ref[s * half:(s + 1) * half, :], preferred_element_type=jnp.float32)
        upd = d if upd is None else upd + d
    o_ref[...] += upd


def _peer_dense(xb16, ut, vb, wmat):
    n = xb16.shape[0]
    e = vb.shape[0]
    tm = _pick(n, (512, 256, 128))
    te = 512
    return pl.pallas_call(
        _peer_dense_kernel,
        grid=(n // tm, e // te),
        in_specs=[pl.BlockSpec((tm, D_MODEL), lambda i, j: (i, 0)),
                  pl.BlockSpec((D_MODEL, te), lambda i, j: (0, j)),
                  pl.BlockSpec((tm, te), lambda i, j: (i, j)),
                  pl.BlockSpec((te, D_MODEL), lambda i, j: (j, 0))],
        out_specs=pl.BlockSpec((tm, D_MODEL), lambda i, j: (i, 0)),
        out_shape=jax.ShapeDtypeStruct((n, D_MODEL), jnp.float32),
        compiler_params=pltpu.CompilerParams(dimension_semantics=("parallel", "arbitrary"),
                                             vmem_limit_bytes=56 * 1024 * 1024),
        name="peer_dense",
    )(xb16, ut, wmat, vb)


def _group(x, pos, conv_buf, ssm0, chunked, attend, w_in, conv_w, a_log, dt_bias, gdn_norm_w,
           w_br_gdn, w_br_att, w_out, ln1_g, ln1_b, peer_wq, peer_k1, peer_k2, peer_u, peer_v, ln2_g, ln2_b):
    (qkv, z, b_raw, a_raw, q_att, k_att, v_att, q_idx, w_idx, k_idx, gate_gdn, gate_att) = jnp.split(
        _mmf(x, w_in, sum(IN_SPLITS)), IN_OFFSETS, axis=-1)
    conv_out, conv_new = _causal_conv(qkv, conv_buf, conv_w)
    q, k, v = _gdn_qkv(conv_out)
    g, beta = _gdn_gates(b_raw, a_raw, a_log, dt_bias)
    gdn = _gdn_chunked if chunked else _gdn_recurrent
    o, ssm_new = gdn(q, k, v, g, beta, ssm0.astype(jnp.float32))
    o_gdn = _gdn_out(o, z, gdn_norm_w)
    qa, ka, va, qi, wi, ki = _att_heads(q_att, k_att, v_att, q_idx, w_idx, k_idx, pos)
    o_att = attend(qa, ka, va, qi, wi, ki)
    mix = _mmf(jax.nn.sigmoid(gate_gdn) * _mmf(o_gdn, w_br_gdn, D_MODEL) + jax.nn.sigmoid(gate_att) * _mmf(o_att, w_br_att, D_MODEL),
               w_out, D_MODEL)
    h = _layernorm(DEEPNORM_ALPHA * x + mix, ln1_g, ln1_b)
    y = _layernorm(DEEPNORM_ALPHA * h + _peer(h, peer_wq, peer_k1, peer_k2, peer_u, peer_v), ln2_g, ln2_b)
    return y, ka, va, ki, conv_new, ssm_new.astype(x.dtype)


def kernel(x_prompt, x_sample, cache_k, cache_v, cache_idx_k, state_conv, state_ssm, page_table, w_in, conv_w, a_log, dt_bias, gdn_norm_w, w_br_gdn, w_br_att, w_out, ln1_g, ln1_b, peer_wq, peer_k1, peer_k2, peer_u, peer_v, ln2_g, ln2_b):
    bp, sp = x_prompt.shape[:2]
    ds = x_sample.shape[1]
    past_len = page_table.shape[1] * PAGE_SIZE
    pos_p = jnp.arange(sp)
    pos_s = past_len + jnp.arange(ds)
    l = 0
    wts = (_wcast(w_in[l]), conv_w[l], a_log[l], dt_bias[l], gdn_norm_w[l], _wcast(w_br_gdn[l]), _wcast(w_br_att[l]),
           _wcast(w_out[l]), ln1_g[l], ln1_b[l], _wcast(peer_wq[l]), peer_k1[l], peer_k2[l],
           peer_u[l].astype(jnp.bfloat16).T, peer_v[l].astype(jnp.bfloat16), ln2_g[l], ln2_b[l])
    y_p, kp, vp, ip, cp, ssp = _group(
        x_prompt, pos_p, jnp.zeros((bp, CONV_W - 1, CONV_DIM), x_prompt.dtype),
        jnp.zeros((bp, H_GDN, D_HEAD, D_HEAD), jnp.float32), True, _dsa_prompt, *wts)
    attend_s = functools.partial(_dsa_sample, cache_k=cache_k, cache_v=cache_v, cache_idx_k=cache_idx_k,
                                 page_table=page_table, layer=l)
    y_s, kss, vss, iss, css, sss = _group(
        x_sample, pos_s, state_conv[l], state_ssm[l], False, attend_s, *wts)
    st = lambda a: a[None]
    return (y_p, y_s, st(kp), st(vp), st(ip), st(cp), st(ssp), st(kss), st(vss), st(iss), st(css), st(sss))
```

```python
import functools
import math

import jax
import jax.numpy as jnp
import numpy as np
from jax import lax
from jax.experimental import pallas as pl
from jax.experimental.pallas import tpu as pltpu

D_MODEL = 4096
PAGE_SIZE = 128
D_HEAD = 128
H_GDN = 16
H_ATT = 16
D_GDN = H_GDN * D_HEAD
D_ATT = H_ATT * D_HEAD
CONV_W = 4
CONV_DIM = 3 * D_GDN
GDN_CHUNK = 64
H_IDX = 32
D_IDX = 128
TOPK_ATT_MAX = 256
Q_BLOCK = 128
ROPE_THETA = 10000.0
LN_EPS = 1e-5
RMS_EPS = 1e-6
PEER_HEADS = 8
PEER_NKEYS = 128
PEER_DQ = 256
PEER_TOPK = 16
PEER_BLOCK = 128
DEPTH = 1
DEEPNORM_ALPHA = (2.0 * DEPTH) ** 0.25
IN_SPLITS = (CONV_DIM, D_GDN, H_GDN, H_GDN, D_ATT, D_ATT, D_ATT, H_IDX * D_IDX, H_IDX, D_IDX, D_MODEL, D_MODEL)
IN_OFFSETS = tuple(int(o) for o in np.cumsum(IN_SPLITS)[:-1])

VMEM_LIMIT = 48 * 1024 * 1024


def _mm_kernel(x_ref, w_ref, o_ref):
    o_ref[...] = jnp.dot(x_ref[...], w_ref[...], preferred_element_type=jnp.float32)


def _pick(n, cands):
    for c in cands:
        if n % c == 0:
            return c
    return n


def _mm(x, w):
    m, k = x.shape
    n = w.shape[1]
    tm = _pick(m, (512, 256, 128))
    tn = _pick(n, (1024, 512, 256, 128))
    return pl.pallas_call(
        _mm_kernel,
        grid=(m // tm, n // tn),
        in_specs=[pl.BlockSpec((tm, k), lambda i, j: (i, 0)),
                  pl.BlockSpec((k, tn), lambda i, j: (0, j))],
        out_specs=pl.BlockSpec((tm, tn), lambda i, j: (i, j)),
        out_shape=jax.ShapeDtypeStruct((m, n), jnp.float32),
        compiler_params=pltpu.CompilerParams(
            dimension_semantics=("parallel", "parallel"),
            vmem_limit_bytes=VMEM_LIMIT),
        name="mm",
    )(x, w)


def _wcast(w):
    return jnp.pad(w.astype(jnp.bfloat16), ((0, 0), (0, (-w.shape[1]) % 128)))


def _in_proj_weights(w):
    names = ("qkv", "z", "b_raw", "a_raw", "q_att", "k_att", "v_att", "q_idx", "w_idx", "k_idx", "gate_gdn", "gate_att")
    bounds = (0,) + IN_OFFSETS + (sum(IN_SPLITS),)
    col = {n: w[:, bounds[i]:bounds[i + 1]].astype(jnp.bfloat16) for i, n in enumerate(names)}
    out = {n: col[n] for n in names if col[n].shape[1] % 128 == 0}
    out["small"] = _wcast(jnp.concatenate([col["b_raw"], col["a_raw"], col["w_idx"]], axis=1))
    return out


def _mmf(x, wb, n):
    shp = x.shape
    y = _mm(x.reshape(-1, shp[-1]).astype(jnp.bfloat16), wb)[:, :n]
    return y.reshape(*shp[:-1], n)


def _layernorm(x, g, b):
    mu = jnp.mean(x, -1, keepdims=True)
    var = jnp.mean(jnp.square(x - mu), -1, keepdims=True)
    return (x - mu) * lax.rsqrt(var + LN_EPS) * g + b


def _rope(x, pos):
    half = x.shape[-1] // 2
    inv = ROPE_THETA ** (-jnp.arange(half, dtype=jnp.float32) / half)
    ang = pos.astype(jnp.float32)[:, None] * inv[None, :]
    cos = jnp.cos(ang)[None, :, None, :]
    sin = jnp.sin(ang)[None, :, None, :]
    x1 = x[..., :half]
    x2 = x[..., half:]
    return jnp.concatenate([x1 * cos - x2 * sin, x2 * cos + x1 * sin], -1)


def _l2n(a):
    return a * lax.rsqrt(jnp.sum(a * a, -1, keepdims=True) + RMS_EPS)


def _causal_conv(u, buf, conv_w):
    t = u.shape[1]
    up = jnp.concatenate([buf.astype(u.dtype), u], axis=1)
    out = sum(up[:, j:j + t] * conv_w[j] for j in range(CONV_W))
    return jax.nn.silu(out), up[:, t:]


def _gdn_qkv(conv_out):
    b, t, _ = conv_out.shape
    q, k, v = (a.reshape(b, t, H_GDN, D_HEAD) for a in jnp.split(conv_out, 3, axis=-1))
    return _l2n(q) * D_HEAD ** -0.5, _l2n(k), v


def _gdn_gates(b_raw, a_raw, a_log, dt_bias):
    beta = jax.nn.sigmoid(b_raw)
    g = -jnp.exp(a_log) * jax.nn.softplus(a_raw + dt_bias)
    return g, beta


def _gdn_chunked(q, k, v, g, beta, s0):
    b, t, h, _ = q.shape
    c = GDN_CHUNK
    n = t // c

    def ch(a):
        return jnp.swapaxes(jnp.moveaxis(a.reshape(b, n, c, h, *a.shape[3:]), 1, 0), 2, 3)

    qc, kc, vc, gc, bc = ch(q), ch(k), ch(v), ch(g), ch(beta)
    gam = jnp.cumsum(gc, axis=-1)
    incl = jnp.tril(jnp.ones((c, c), bool))
    strict = jnp.tril(jnp.ones((c, c), bool), -1)
    decay = jnp.exp(jnp.where(incl, gam[..., :, None] - gam[..., None, :], -jnp.inf))
    kb = kc * bc[..., None]
    a_mat = jnp.where(strict, jnp.einsum('nbhid,nbhjd->nbhij', kb, kc) * decay, 0.0) + jnp.eye(c, dtype=jnp.float32)
    solve = functools.partial(lax.linalg.triangular_solve, left_side=True, lower=True, unit_diagonal=True)
    u = solve(a_mat, vc * bc[..., None])
    w = solve(a_mat, kb * jnp.exp(gam)[..., None])
    qk = jnp.einsum('nbhid,nbhjd->nbhij', qc, kc) * decay

    def step(s, inp):
        q_c, k_c, u_c, w_c, g_c, qk_c = inp
        v_new = u_c - jnp.einsum('bhcd,bhde->bhce', w_c, s)
        o = jnp.einsum('bhcd,bhde->bhce', q_c * jnp.exp(g_c)[..., None], s) + jnp.einsum('bhij,bhje->bhie', qk_c, v_new)
        g_last = g_c[..., -1:]
        s = s * jnp.exp(g_last)[..., None] + jnp.einsum('bhcd,bhce->bhde', k_c * jnp.exp(g_last - g_c)[..., None], v_new)
        return s, o

    s, o = lax.scan(step, s0, (qc, kc, u, w, gam, qk))
    o = jnp.moveaxis(jnp.swapaxes(o, 2, 3), 0, 1).reshape(b, t, h, -1)
    return o, s


def _gdn_recurrent(q, k, v, g, beta, s0):
    def step(s, inp):
        q_t, k_t, v_t, g_t, b_t = inp
        s = s * jnp.exp(g_t)[..., None, None]
        ks = jnp.einsum('bhk,bhkv->bhv', k_t, s)
        s = s + jnp.einsum('bhk,bhv->bhkv', k_t * b_t[..., None], v_t - ks)
        return s, jnp.einsum('bhk,bhkv->bhv', q_t, s)

    tm = lambda a: jnp.moveaxis(a, 1, 0)
    s, o = lax.scan(step, s0, (tm(q), tm(k), tm(v), tm(g), tm(beta)))
    return jnp.moveaxis(o, 0, 1), s


def _gdn_out(o, z, norm_w):
    b, t = z.shape[:2]
    o = o * lax.rsqrt(jnp.mean(o * o, -1, keepdims=True) + RMS_EPS) * norm_w
    return o.reshape(b, t, D_GDN) * jax.nn.silu(z)


def _att_heads(q_att, k_att, v_att, q_idx, w_idx, k_idx, pos):
    b, t, _ = q_att.shape
    qa = _rope(q_att.reshape(b, t, H_ATT, D_HEAD), pos)
    ka = _rope(k_att.reshape(b, t, H_ATT, D_HEAD), pos)
    va = v_att.reshape(b, t, H_ATT, D_HEAD)
    qi = _rope(q_idx.reshape(b, t, H_IDX, D_IDX), pos)
    ki = _rope(k_idx[:, :, None, :], pos)[:, :, 0]
    wi = w_idx * (H_IDX ** -0.5 * D_IDX ** -0.5)
    return qa, ka, va, qi, wi, ki


def _indexer_topk(qi, wi, ki, q_pos, n_sel):
    rel = jax.nn.relu(jnp.einsum('bthd,bsd->bths', qi, ki))
    score = jnp.einsum('bths,bth->bts', rel, wi).astype(jnp.float32)
    key_pos = jnp.arange(ki.shape[1])
    score = jnp.where(key_pos[None, None, :] <= q_pos[None, :, None], score, -jnp.inf)
    _, sel = lax.top_k(score, n_sel)
    return sel


def _sparse_attend(q, k_sel, v_sel, sel, q_pos):
    logits = jnp.einsum('bthd,btkhd->bthk', q, k_sel).astype(jnp.float32) * D_HEAD ** -0.5
    valid = (sel <= q_pos[None, :, None])[:, :, None, :]
    p = jax.nn.softmax(jnp.where(valid, logits, -jnp.inf), axis=-1)
    return jnp.einsum('bthk,btkhd->bthd', p.astype(v_sel.dtype), v_sel)


DSA_TQ = 128
DSA_TK = 256
INT32_MIN = np.int32(-2 ** 31)


def _float_to_key(x):
    bits = lax.bitcast_convert_type(x + 0.0, jnp.int32)
    return bits ^ ((bits >> 31) & jnp.int32(0x7FFFFFFF))


def _kth_largest_key(xs_ref, nch, ch, n_sel):
    lanes = xs_ref.shape[1]

    def count_ge(cand):
        def body(c, acc):
            r0 = pl.multiple_of(c * ch, ch)
            ind = jnp.where(xs_ref[pl.ds(r0, ch), :] >= cand, 1, 0).astype(jnp.int32)
            return acc + jnp.sum(ind.reshape(ch // 8, 8, lanes), axis=0)

        acc = lax.fori_loop(0, nch, body, jnp.zeros((8, lanes), jnp.int32))
        return jnp.sum(acc, axis=0, keepdims=True)

    def bit_cond(st):
        b, _, open_ = st
        return jnp.logical_and(b < 32, jnp.max(open_) > 0)

    def bit_body(st):
        b, thr_u, open_ = st
        cand_u = thr_u | lax.shift_left(jnp.int32(1), 31 - b)
        cnt = count_ge(cand_u ^ INT32_MIN)
        take = jnp.logical_and(cnt >= n_sel, open_ > 0)
        return b + 1, jnp.where(take, cand_u, thr_u), jnp.where(cnt == n_sel, 0, open_)

    _, thr_u, _ = lax.while_loop(
        bit_cond, bit_body,
        (jnp.int32(0), jnp.zeros((1, lanes), jnp.int32), jnp.ones((1, lanes), jnp.int32)))
    return thr_u ^ INT32_MIN


def _dsa_prompt_kernel(ii_ref, jj_ref, qi_ref, w_ref, ki_ref, q_ref, k_ref, vt_ref, o_ref,
                       xs_ref, thr_ref, bias_ref, m_ref, l_ref, acc_ref, *, n_sel):
    s_id = pl.program_id(0)
    i = ii_ref[s_id]
    j = jj_ref[s_id]
    last = (i * DSA_TQ + DSA_TQ - 1) // DSA_TK
    nch = last + 1
    kiota = lax.broadcasted_iota(jnp.int32, (DSA_TK, DSA_TQ), 0)
    qpos = i * DSA_TQ + lax.broadcasted_iota(jnp.int32, (DSA_TK, DSA_TQ), 1)

    @pl.when(j == 0)
    def _index():
        def score_body(c, carry):
            r0 = pl.multiple_of(c * DSA_TK, DSA_TK)
            kchunk = ki_ref[pl.ds(r0, DSA_TK), :]
            sc = jnp.zeros((DSA_TK, DSA_TQ), jnp.float32)
            for hp in range(H_IDX // 2):
                z = jnp.dot(kchunk, qi_ref[0, hp], preferred_element_type=jnp.float32)
                sc = sc + w_ref[0, 2 * hp:2 * hp + 1, :] * jnp.maximum(z[:, :DSA_TQ], 0.0)
                sc = sc + w_ref[0, 2 * hp + 1:2 * hp + 2, :] * jnp.maximum(z[:, DSA_TQ:], 0.0)
            xs_ref[pl.ds(r0, DSA_TK), :] = jnp.where(r0 + kiota <= qpos, _float_to_key(sc), INT32_MIN)
            return carry

        lax.fori_loop(0, nch, score_body, 0)

        thr_ref[...] = _kth_largest_key(xs_ref, nch, DSA_TK, n_sel)
        m_ref[...] = jnp.full(m_ref.shape, -1e30, jnp.float32)
        l_ref[...] = jnp.zeros(l_ref.shape, jnp.float32)
        acc_ref[...] = jnp.zeros(acc_ref.shape, jnp.float32)

    r0 = pl.multiple_of(j * DSA_TK, DSA_TK)
    sel = jnp.where(xs_ref[pl.ds(r0, DSA_TK), :] >= thr_ref[...], r0 + kiota, jnp.int32(2 ** 30)) <= qpos
    bias_ref[...] = jnp.where(sel, 0.0, -jnp.inf)
    for h in range(H_ATT):
        s = jnp.dot(k_ref[:, h * D_HEAD:(h + 1) * D_HEAD], q_ref[h], preferred_element_type=jnp.float32)
        s = s + bias_ref[...]
        m_old = m_ref[h:h + 1, :]
        m_new = jnp.maximum(m_old, jnp.max(s, axis=0, keepdims=True))
        alpha = jnp.exp(m_old - m_new)
        p = jnp.exp(s - m_new)
        l_ref[h:h + 1, :] = alpha * l_ref[h:h + 1, :] + jnp.sum(p, axis=0, keepdims=True)
        m_ref[h:h + 1, :] = m_new
        acc_ref[h] = alpha * acc_ref[h] + jnp.dot(vt_ref[h], p.astype(jnp.bfloat16),
                                                  preferred_element_type=jnp.float32)

    @pl.when(j == last)
    def _finish():
        for h in range(H_ATT):
            o_ref[:, h * D_HEAD:(h + 1) * D_HEAD] = (acc_ref[h] / l_ref[h:h + 1, :]).T


def _dsa_prompt_one(q, k, v, qi, wi, ki):
    t = q.shape[0]
    assert t % DSA_TK == 0 and DSA_TK % DSA_TQ == 0
    nq = t // DSA_TQ
    n_sel = min(TOPK_ATT_MAX, t // 4)
    bf = jnp.bfloat16
    qit = qi.reshape(nq, DSA_TQ, H_IDX // 2, 2, D_IDX).transpose(0, 2, 4, 3, 1).reshape(
        nq, H_IDX // 2, D_IDX, 2 * DSA_TQ).astype(bf)
    wr = wi.reshape(nq, DSA_TQ, H_IDX).transpose(0, 2, 1)
    qt = (q * D_HEAD ** -0.5).transpose(1, 2, 0).astype(bf)
    kf = k.reshape(t, D_ATT).astype(bf)
    vt = v.transpose(1, 2, 0).astype(bf)
    steps = [(i, j) for i in range(nq) for j in range((i * DSA_TQ + DSA_TQ - 1) // DSA_TK + 1)]
    ii = jnp.asarray([s[0] for s in steps], jnp.int32)
    jj = jnp.asarray([s[1] for s in steps], jnp.int32)
    grid_spec = pltpu.PrefetchScalarGridSpec(
        num_scalar_prefetch=2,
        grid=(len(steps),),
        in_specs=[
            pl.BlockSpec((1, H_IDX // 2, D_IDX, 2 * DSA_TQ), lambda s, ii, jj: (ii[s], 0, 0, 0)),
            pl.BlockSpec((1, H_IDX, DSA_TQ), lambda s, ii, jj: (ii[s], 0, 0)),
            pl.BlockSpec((t, D_IDX), lambda s, ii, jj: (0, 0)),
            pl.BlockSpec((H_ATT, D_HEAD, DSA_TQ), lambda s, ii, jj: (0, 0, ii[s])),
            pl.BlockSpec((DSA_TK, D_ATT), lambda s, ii, jj: (jj[s], 0)),
            pl.BlockSpec((H_ATT, D_HEAD, DSA_TK), lambda s, ii, jj: (0, 0, jj[s])),
        ],
        out_specs=pl.BlockSpec((DSA_TQ, D_ATT), lambda s, ii, jj: (ii[s], 0)),
        scratch_shapes=[
            pltpu.VMEM((t, DSA_TQ), jnp.int32),
            pltpu.VMEM((1, DSA_TQ), jnp.int32),
            pltpu.VMEM((DSA_TK, DSA_TQ), jnp.float32),
            pltpu.VMEM((H_ATT, DSA_TQ), jnp.float32),
            pltpu.VMEM((H_ATT, DSA_TQ), jnp.float32),
            pltpu.VMEM((H_ATT, D_HEAD, DSA_TQ), jnp.float32),
        ])
    return pl.pallas_call(
        functools.partial(_dsa_prompt_kernel, n_sel=n_sel),
        grid_spec=grid_spec,
        out_shape=jax.ShapeDtypeStruct((t, D_ATT), jnp.float32),
        compiler_params=pltpu.CompilerParams(dimension_semantics=("arbitrary",), vmem_limit_bytes=VMEM_LIMIT),
        name="dsa_prompt",
    )(ii, jj, qit, wr, ki.astype(bf), qt, kf, vt)


def _dsa_prompt(q, k, v, qi, wi, ki):
    return jnp.stack([_dsa_prompt_one(q[b], k[b], v[b], qi[b], wi[b], ki[b]) for b in range(q.shape[0])])


SMP_IDX_PAGES = 8
SMP_ATT_PAGES = 4
SMP_CH = 128


def _smp_score_kernel(pt_ref, *refs):
    pages = refs[:SMP_IDX_PAGES]
    qi_ref, wb_ref, o_ref = refs[SMP_IDX_PAGES:]
    nq = o_ref.shape[1]
    for g, ki_ref in enumerate(pages):
        z = lax.dot_general(qi_ref[0], ki_ref[0, 0].astype(jnp.bfloat16), (((1,), (1,)), ((), ())),
                            preferred_element_type=jnp.float32)
        r = jnp.maximum(z, 0.0) * wb_ref[0]
        o_ref[0, :, g * PAGE_SIZE:(g + 1) * PAGE_SIZE] = jnp.sum(r.reshape(nq, H_IDX, PAGE_SIZE), axis=1)


def _smp_thr_kernel(s_ref, thr_ref, xs_ref, *, n_sel):
    nch = s_ref.shape[0] // SMP_CH

    def conv(c, carry):
        r0 = pl.multiple_of(c * SMP_CH, SMP_CH)
        xs_ref[pl.ds(r0, SMP_CH), :] = _float_to_key(s_ref[pl.ds(r0, SMP_CH), :])
        return carry

    lax.fori_loop(0, nch, conv, 0)
    key = _kth_largest_key(xs_ref, nch, SMP_CH, n_sel)
    thr_ref[...] = lax.bitcast_convert_type(key ^ ((key >> 31) & jnp.int32(0x7FFFFFFF)), jnp.float32)


def _smp_attn_kernel(pt_ref, *refs, nq):
    kp = refs[:SMP_ATT_PAGES]
    vp = refs[SMP_ATT_PAGES:2 * SMP_ATT_PAGES]
    (sc_ref, scn_ref, thr_ref, q_ref, kn_ref, vn_ref, ex_ref, hm_ref,
     o_ref, m_ref, l_ref, acc_ref) = refs[2 * SMP_ATT_PAGES:]
    pp = pl.program_id(1)
    rows = PAGE_SIZE * H_ATT

    def attend(k, v, sc):
        sel = jnp.where(sc >= thr_ref[0], 1.0, 0.0)
        sel = jnp.concatenate(
            [jnp.broadcast_to(sel[tt:tt + 1, :], (H_ATT, PAGE_SIZE)) for tt in range(nq)], axis=0)
        valid = jnp.dot(sel.astype(jnp.bfloat16), ex_ref[...], preferred_element_type=jnp.float32) * hm_ref[...]
        s = lax.dot_general(q_ref[0], k.reshape(rows, D_HEAD).astype(jnp.bfloat16), (((1,), (1,)), ((), ())),
                            preferred_element_type=jnp.float32)
        s = jnp.where(valid > 0.5, s, -jnp.inf)
        m_old = m_ref[...]
        m_new = jnp.maximum(m_old, jnp.max(s, axis=1, keepdims=True))
        alpha = jnp.exp(m_old - m_new)
        p = jnp.exp(s - m_new)
        l_ref[...] = alpha * l_ref[...] + jnp.sum(p, axis=1, keepdims=True)
        m_ref[...] = m_new
        acc_ref[...] = alpha * acc_ref[...] + jnp.dot(
            p.astype(jnp.bfloat16), v.reshape(rows, D_HEAD).astype(jnp.bfloat16), preferred_element_type=jnp.float32)

    @pl.when(pp == 0)
    def _():
        m_ref[...] = jnp.full(m_ref.shape, -1e30, jnp.float32)
        l_ref[...] = jnp.zeros(l_ref.shape, jnp.float32)
        acc_ref[...] = jnp.zeros(acc_ref.shape, jnp.float32)
        attend(kn_ref[0], vn_ref[0], scn_ref[0])

    for g in range(SMP_ATT_PAGES):
        attend(kp[g][0, 0], vp[g][0, 0], sc_ref[0, :, g * PAGE_SIZE:(g + 1) * PAGE_SIZE])

    @pl.when(pp == pl.num_programs(1) - 1)
    def _():
        o_ref[0] = acc_ref[...] / l_ref[...]


def _dsa_sample(q, k_new, v_new, qi, wi, ki_new, cache_k, cache_v, cache_idx_k, page_table, layer):
    db, t = q.shape[:2]
    npg = page_table.shape[1]
    past_len = npg * PAGE_SIZE
    n_sel = min(TOPK_ATT_MAX, (past_len + t) // 4)
    assert npg % SMP_IDX_PAGES == 0 and npg % SMP_ATT_PAGES == 0 and (t * H_IDX) % 8 == 0
    bf = jnp.bfloat16
    pt = page_table.reshape(-1).astype(jnp.int32)
    n_pool = cache_k.shape[1]

    def page_spec(width, per_step, g):
        return pl.BlockSpec((1, 1, PAGE_SIZE, width),
                            lambda b, p, pt: (layer, pt[b * npg + p * per_step + g], 0, 0))

    row_spec = lambda rows, width: pl.BlockSpec((1, rows, width), lambda b, p, pt: (b, 0, 0))
    score_past = pl.pallas_call(
        _smp_score_kernel,
        grid_spec=pltpu.PrefetchScalarGridSpec(
            num_scalar_prefetch=1, grid=(db, npg // SMP_IDX_PAGES),
            in_specs=[page_spec(D_IDX, SMP_IDX_PAGES, g) for g in range(SMP_IDX_PAGES)]
            + [row_spec(t * H_IDX, D_IDX), row_spec(t * H_IDX, PAGE_SIZE)],
            out_specs=pl.BlockSpec((1, t, SMP_IDX_PAGES * PAGE_SIZE), lambda b, p, pt: (b, 0, p))),
        out_shape=jax.ShapeDtypeStruct((db, t, past_len), jnp.float32),
        compiler_params=pltpu.CompilerParams(dimension_semantics=("parallel", "arbitrary"),
                                             vmem_limit_bytes=VMEM_LIMIT),
        name="smp_score",
    )(pt, *([cache_idx_k] * SMP_IDX_PAGES), qi.reshape(db, t * H_IDX, D_IDX).astype(bf),
      jnp.broadcast_to(wi.reshape(db, t * H_IDX, 1), (db, t * H_IDX, PAGE_SIZE)))

    rel = jax.nn.relu(jnp.einsum('bthd,bsd->bths', qi, ki_new))
    score_new = jnp.einsum('bths,bth->bts', rel, wi)
    score_new = jnp.where(jnp.arange(t)[None, None, :] <= jnp.arange(t)[None, :, None], score_new, -jnp.inf)
    score_new = jnp.pad(score_new, ((0, 0), (0, 0), (0, PAGE_SIZE - t)), constant_values=-jnp.inf)

    score_t = jnp.concatenate([score_past, score_new], axis=2).reshape(db * t, past_len + PAGE_SIZE).T
    thr = pl.pallas_call(
        functools.partial(_smp_thr_kernel, n_sel=n_sel),
        out_shape=jax.ShapeDtypeStruct((1, db * t), jnp.float32),
        scratch_shapes=[pltpu.VMEM(score_t.shape, jnp.int32)],
        compiler_params=pltpu.CompilerParams(vmem_limit_bytes=VMEM_LIMIT),
        name="smp_thr",
    )(score_t)
    thr_b = jnp.broadcast_to(thr.reshape(db, t, 1), (db, t, PAGE_SIZE))

    nrow = t * H_ATT
    cols = PAGE_SIZE * H_ATT
    q_rows = (q * D_HEAD ** -0.5).reshape(db, nrow, D_HEAD).astype(bf)
    expand = jnp.repeat(jnp.eye(PAGE_SIZE, dtype=bf), H_ATT, axis=1)
    head_match = (jnp.arange(cols)[None, :] % H_ATT == jnp.arange(nrow)[:, None] % H_ATT).astype(jnp.float32)
    pad_page = lambda a: jnp.pad(a, ((0, 0), (0, PAGE_SIZE - t), (0, 0), (0, 0)))
    cache_spec = lambda g: pl.BlockSpec(
        (1, 1, PAGE_SIZE, H_ATT, D_HEAD),
        lambda b, p, pt: (layer, pt[b * npg + p * SMP_ATT_PAGES + g], 0, 0, 0))
    const_spec = lambda shape: pl.BlockSpec(shape, lambda b, p, pt: (0,) * len(shape))
    new_spec = pl.BlockSpec((1, PAGE_SIZE, H_ATT, D_HEAD), lambda b, p, pt: (b, 0, 0, 0))
    out = pl.pallas_call(
        functools.partial(_smp_attn_kernel, nq=t),
        grid_spec=pltpu.PrefetchScalarGridSpec(
            num_scalar_prefetch=1, grid=(db, npg // SMP_ATT_PAGES),
            in_specs=[cache_spec(g) for g in range(SMP_ATT_PAGES)] * 2
            + [pl.BlockSpec((1, t, SMP_ATT_PAGES * PAGE_SIZE), lambda b, p, pt: (b, 0, p)),
               row_spec(t, PAGE_SIZE), row_spec(t, PAGE_SIZE), row_spec(nrow, D_HEAD),
               new_spec, new_spec, const_spec((PAGE_SIZE, cols)), const_spec((nrow, cols))],
            out_specs=row_spec(nrow, D_HEAD),
            scratch_shapes=[pltpu.VMEM((nrow, 1), jnp.float32), pltpu.VMEM((nrow, 1), jnp.float32),
                            pltpu.VMEM((nrow, D_HEAD), jnp.float32)]),
        out_shape=jax.ShapeDtypeStruct((db, nrow, D_HEAD), jnp.float32),
        compiler_params=pltpu.CompilerParams(dimension_semantics=("parallel", "arbitrary"),
                                             vmem_limit_bytes=VMEM_LIMIT),
        name="smp_attn",
    )(pt, *([cache_k] * SMP_ATT_PAGES), *([cache_v] * SMP_ATT_PAGES),
      score_past, score_new, thr_b, q_rows, pad_page(k_new), pad_page(v_new), expand, head_match)
    return out.reshape(db, t, D_ATT)


def _peer(x, wq, k1, k2, u, v):
    shape = x.shape
    xf = x.reshape(-1, D_MODEL)
    n = xf.shape[0]
    pad = (-n) % PEER_BLOCK
    xf = jnp.pad(xf, ((0, pad), (0, 0)))
    xb16 = xf.astype(jnp.bfloat16)
    e1, e2, gate = _peer_route(_mm(xb16, wq), k1, k2)
    wmat = _peer_gate_matrix(e1, e2, gate)
    out = _peer_dense(xb16, u, v, wmat)
    return out[:n].reshape(shape)


PEER_RT = 128
PEER_RH = 2
PEER_TOPK_LOG2 = PEER_TOPK.bit_length() - 1
assert 1 << PEER_TOPK_LOG2 == PEER_TOPK
PEER_PAIRS = tuple((i, j) for i in range(PEER_TOPK) for j in range(PEER_TOPK) if (i + 1) * (j + 1) <= PEER_TOPK)


def _extract_top(s, ids, n):
    big = jnp.int32(2 ** 30)
    vals, sel = [], []
    for _ in range(n):
        m = jnp.max(s, axis=0, keepdims=True)
        pick = jnp.min(jnp.where(s == m, ids, big), axis=0, keepdims=True)
        vals.append(m)
        sel.append(pick)
        s = jnp.where(ids == pick, -jnp.inf, s)
    return vals, sel


def _peer_route_kernel(q_ref, k1_ref, k2_ref, e1_ref, e2_ref, g_ref):
    kio = lax.broadcasted_iota(jnp.int32, (PEER_NKEYS, PEER_RT), 0)
    flat = jnp.concatenate(
        [jnp.full((1, PEER_RT), i * PEER_TOPK + j, jnp.int32) for i, j in PEER_PAIRS], axis=0)
    for hh in range(PEER_RH):
        tops = []
        for side, kref in ((0, k1_ref), (1, k2_ref)):
            lo = (2 * hh + side) * (PEER_DQ // 2)
            qs = q_ref[:, lo:lo + PEER_DQ // 2].astype(jnp.bfloat16)
            s = lax.dot_general(kref[hh], qs, (((1,), (1,)), ((), ())),
                                preferred_element_type=jnp.float32)
            tops.append(_extract_top(s, kio, PEER_TOPK))
        (v1, i1), (v2, i2) = tops
        cand = jnp.concatenate([v1[i] + v2[j] for i, j in PEER_PAIRS], axis=0)
        top_s, top_f = _extract_top(cand, flat, PEER_TOPK)
        ex = [jnp.exp(t - top_s[0]) for t in top_s]
        inv = 1.0 / sum(ex)
        for r in range(PEER_TOPK):
            fi = top_f[r] >> PEER_TOPK_LOG2
            fj = top_f[r] & (PEER_TOPK - 1)
            e1 = sum(jnp.where(fi == i, i1[i], 0) for i in range(PEER_TOPK))
            e2 = sum(jnp.where(fj == j, i2[j], 0) for j in range(PEER_TOPK))
            e1_ref[hh, r:r + 1, :] = e1
            e2_ref[hh, r:r + 1, :] = e2
            g_ref[hh, r:r + 1, :] = ex[r] * inv


def _peer_route(q, k1, k2):
    n = q.shape[0]
    assert n % PEER_RT == 0 and PEER_HEADS % PEER_RH == 0
    kspec = pl.BlockSpec((PEER_RH, PEER_NKEYS, PEER_DQ // 2), lambda i, h: (h, 0, 0))
    ospec = pl.BlockSpec((PEER_RH, PEER_TOPK, PEER_RT), lambda i, h: (h, 0, i))
    oshape = lambda dt: jax.ShapeDtypeStruct((PEER_HEADS, PEER_TOPK, n), dt)
    e1, e2, g = pl.pallas_call(
        _peer_route_kernel,
        grid=(n // PEER_RT, PEER_HEADS // PEER_RH),
        in_specs=[pl.BlockSpec((PEER_RT, PEER_RH * PEER_DQ), lambda i, h: (i, h)), kspec, kspec],
        out_specs=[ospec, ospec, ospec],
        out_shape=[oshape(jnp.int32), oshape(jnp.int32), oshape(jnp.float32)],
        compiler_params=pltpu.CompilerParams(dimension_semantics=("parallel", "parallel"),
                                             vmem_limit_bytes=VMEM_LIMIT),
        name="peer_route",
    )(q, k1.astype(jnp.bfloat16), k2.astype(jnp.bfloat16))
    tok_major = lambda a: a.reshape(PEER_HEADS * PEER_TOPK, n).T
    return tok_major(e1), tok_major(e2), tok_major(g)


def _peer_gate_kernel(i1_ref, i2_ref, g_ref, w_ref):
    iota = lax.broadcasted_iota(jnp.int32, (PEER_NKEYS, PEER_HEADS * PEER_TOPK), 0)

    def body(t, carry):
        a = jnp.where(iota == i1_ref[pl.ds(t, 1), :], g_ref[pl.ds(t, 1), :], 0.0).astype(jnp.bfloat16)
        b = jnp.where(iota == i2_ref[pl.ds(t, 1), :], 1.0, 0.0).astype(jnp.bfloat16)
        w = lax.dot_general(a, b, (((1,), (1,)), ((), ())), preferred_element_type=jnp.float32)
        w_ref[t] = w.astype(w_ref.dtype)
        return carry

    lax.fori_loop(0, i1_ref.shape[0], body, 0, unroll=8)


def _peer_gate_matrix(e1, e2, gate):
    n, p = e1.shape
    tw = _pick(n, (128,))
    spec = pl.BlockSpec((tw, p), lambda i: (i, 0))
    w3 = pl.pallas_call(
        _peer_gate_kernel,
        grid=(n // tw,),
        in_specs=[spec, spec, spec],
        out_specs=pl.BlockSpec((tw, PEER_NKEYS, PEER_NKEYS), lambda i: (i, 0, 0)),
        out_shape=jax.ShapeDtypeStruct((n, PEER_NKEYS, PEER_NKEYS), jnp.bfloat16),
        compiler_params=pltpu.CompilerParams(dimension_semantics=("parallel",), vmem_limit_bytes=VMEM_LIMIT),
        name="peer_gate",
    )(e1, e2, gate)
    return w3.reshape(n, PEER_NKEYS * PEER_NKEYS)


def _peer_dense_kernel(h_ref, ut_ref, w_ref, v_ref, o_ref):
    @pl.when(pl.program_id(1) == 0)
    def _():
        o_ref[...] = jnp.zeros_like(o_ref)

    half = ut_ref.shape[1] // 2
    upd = None
    for s in range(2):
        z = jnp.dot(h_ref[...], ut_ref[:, s * half:(s + 1) * half], preferred_element_type=jnp.float32)
        act = 0.5 * z * (1.0 + lax.erf(z * (2.0 ** -0.5)))
        c = (w_ref[:, s * half:(s + 1) * half].astype(jnp.float32) * act).astype(jnp.bfloat16)
        d = jnp.dot(c, v_ref[s * half:(s + 1) * half, :], preferred_element_type=jnp.float32)
        upd = d if upd is None else upd + d
    o_ref[...] += upd


def _peer_dense(xb16, ut, vb, wmat):
    n = xb16.shape[0]
    e = vb.shape[0]
    tm = _pick(n, (512, 256, 128))
    te = 512
    return pl.pallas_call(
        _peer_dense_kernel,
        grid=(n // tm, e // te),
        in_specs=[pl.BlockSpec((tm, D_MODEL), lambda i, j: (i, 0)),
                  pl.BlockSpec((D_MODEL, te), lambda i, j: (0, j)),
                  pl.BlockSpec((tm, te), lambda i, j: (i, j)),
                  pl.BlockSpec((te, D_MODEL), lambda i, j: (j, 0))],
        out_specs=pl.BlockSpec((tm, D_MODEL), lambda i, j: (i, 0)),
        out_shape=jax.ShapeDtypeStruct((n, D_MODEL), jnp.float32),
        compiler_params=pltpu.CompilerParams(dimension_semantics=("parallel", "arbitrary"),
                                             vmem_limit_bytes=56 * 1024 * 1024),
        name="peer_dense",
    )(xb16, ut, wmat, vb)


def _group(x, pos, conv_buf, ssm0, chunked, attend, w_in, conv_w, a_log, dt_bias, gdn_norm_w,
           w_br_gdn, w_br_att, w_out, ln1_g, ln1_b, peer_wq, peer_k1, peer_k2, peer_u, peer_v, ln2_g, ln2_b):
    xb = x.reshape(-1, D_MODEL).astype(jnp.bfloat16)
    proj = lambda name: _mm(xb, w_in[name]).reshape(*x.shape[:-1], -1)
    qkv, z, q_att, k_att, v_att, q_idx, k_idx, gate_gdn, gate_att = (
        proj(n) for n in ("qkv", "z", "q_att", "k_att", "v_att", "q_idx", "k_idx", "gate_gdn", "gate_att"))
    small = proj("small")
    b_raw, a_raw, w_idx = small[..., :H_GDN], small[..., H_GDN:2 * H_GDN], small[..., 2 * H_GDN:2 * H_GDN + H_IDX]
    conv_out, conv_new = _causal_conv(qkv, conv_buf, conv_w)
    q, k, v = _gdn_qkv(conv_out)
    g, beta = _gdn_gates(b_raw, a_raw, a_log, dt_bias)
    gdn = _gdn_chunked if chunked else _gdn_recurrent
    o, ssm_new = gdn(q, k, v, g, beta, ssm0.astype(jnp.float32))
    o_gdn = _gdn_out(o, z, gdn_norm_w)
    qa, ka, va, qi, wi, ki = _att_heads(q_att, k_att, v_att, q_idx, w_idx, k_idx, pos)
    o_att = attend(qa, ka, va, qi, wi, ki)
    mix = _mmf(jax.nn.sigmoid(gate_gdn) * _mmf(o_gdn, w_br_gdn, D_MODEL) + jax.nn.sigmoid(gate_att) * _mmf(o_att, w_br_att, D_MODEL),
               w_out, D_MODEL)
    h = _layernorm(DEEPNORM_ALPHA * x + mix, ln1_g, ln1_b)
    y = _layernorm(DEEPNORM_ALPHA * h + _peer(h, peer_wq, peer_k1, peer_k2, peer_u, peer_v), ln2_g, ln2_b)
    return y, ka, va, ki, conv_new, ssm_new.astype(x.dtype)


def kernel(x_prompt, x_sample, cache_k, cache_v, cache_idx_k, state_conv, state_ssm, page_table, w_in, conv_w, a_log, dt_bias, gdn_norm_w, w_br_gdn, w_br_att, w_out, ln1_g, ln1_b, peer_wq, peer_k1, peer_k2, peer_u, peer_v, ln2_g, ln2_b):
    bp, sp = x_prompt.shape[:2]
    ds = x_sample.shape[1]
    past_len = page_table.shape[1] * PAGE_SIZE
    pos_p = jnp.arange(sp)
    pos_s = past_len + jnp.arange(ds)
    l = 0
    wts = (_in_proj_weights(w_in[l]), conv_w[l], a_log[l], dt_bias[l], gdn_norm_w[l], _wcast(w_br_gdn[l]), _wcast(w_br_att[l]),
           _wcast(w_out[l]), ln1_g[l], ln1_b[l], _wcast(peer_wq[l]), peer_k1[l], peer_k2[l],
           peer_u[l].astype(jnp.bfloat16).T, peer_v[l].astype(jnp.bfloat16), ln2_g[l], ln2_b[l])
    y_p, kp, vp, ip, cp, ssp = _group(
        x_prompt, pos_p, jnp.zeros((bp, CONV_W - 1, CONV_DIM), x_prompt.dtype),
        jnp.zeros((bp, H_GDN, D_HEAD, D_HEAD), jnp.float32), True, _dsa_prompt, *wts)
    attend_s = functools.partial(_dsa_sample, cache_k=cache_k, cache_v=cache_v, cache_idx_k=cache_idx_k,
                                 page_table=page_table, layer=l)
    y_s, kss, vss, iss, css, sss = _group(
        x_sample, pos_s, state_conv[l], state_ssm[l], False, attend_s, *wts)
    st = lambda a: a[None]
    return (y_p, y_s, st(kp), st(vp), st(ip), st(cp), st(ssp), st(kss), st(vss), st(iss), st(css), st(sss))
```

```python
import functools
import math

import jax
import jax.numpy as jnp
import numpy as np
from jax import lax
from jax.experimental import pallas as pl
from jax.experimental.pallas import tpu as pltpu

D_MODEL = 4096
PAGE_SIZE = 128
D_HEAD = 128
H_GDN = 16
H_ATT = 16
D_GDN = H_GDN * D_HEAD
D_ATT = H_ATT * D_HEAD
CONV_W = 4
CONV_DIM = 3 * D_GDN
GDN_CHUNK = 64
H_IDX = 32
D_IDX = 128
TOPK_ATT_MAX = 256
Q_BLOCK = 128
ROPE_THETA = 10000.0
LN_EPS = 1e-5
RMS_EPS = 1e-6
PEER_HEADS = 8
PEER_NKEYS = 128
PEER_DQ = 256
PEER_TOPK = 16
PEER_BLOCK = 128
DEPTH = 1
DEEPNORM_ALPHA = (2.0 * DEPTH) ** 0.25
IN_SPLITS = (CONV_DIM, D_GDN, H_GDN, H_GDN, D_ATT, D_ATT, D_ATT, H_IDX * D_IDX, H_IDX, D_IDX, D_MODEL, D_MODEL)
IN_OFFSETS = tuple(int(o) for o in np.cumsum(IN_SPLITS)[:-1])

VMEM_LIMIT = 48 * 1024 * 1024


def _mm_kernel(x_ref, w_ref, o_ref):
    o_ref[...] = jnp.dot(x_ref[...], w_ref[...], preferred_element_type=jnp.float32)


def _pick(n, cands):
    for c in cands:
        if n % c == 0:
            return c
    return n


def _mm(x, w):
    m, k = x.shape
    n = w.shape[1]
    tm = _pick(m, (512, 256, 128))
    tn = _pick(n, (1024, 512, 256, 128))
    return pl.pallas_call(
        _mm_kernel,
        grid=(m // tm, n // tn),
        in_specs=[pl.BlockSpec((tm, k), lambda i, j: (i, 0)),
                  pl.BlockSpec((k, tn), lambda i, j: (0, j))],
        out_specs=pl.BlockSpec((tm, tn), lambda i, j: (i, j)),
        out_shape=jax.ShapeDtypeStruct((m, n), jnp.float32),
        compiler_params=pltpu.CompilerParams(
            dimension_semantics=("parallel", "parallel"),
            vmem_limit_bytes=VMEM_LIMIT),
        name="mm",
    )(x, w)


def _wcast(w):
    return jnp.pad(w.astype(jnp.bfloat16), ((0, 0), (0, (-w.shape[1]) % 128)))


def _in_proj_weights(w):
    names = ("qkv", "z", "b_raw", "a_raw", "q_att", "k_att", "v_att", "q_idx", "w_idx", "k_idx", "gate_gdn", "gate_att")
    bounds = (0,) + IN_OFFSETS + (sum(IN_SPLITS),)
    col = {n: w[:, bounds[i]:bounds[i + 1]].astype(jnp.bfloat16) for i, n in enumerate(names)}
    out = {n: col[n] for n in names if col[n].shape[1] % 128 == 0}
    out["small"] = _wcast(jnp.concatenate([col["b_raw"], col["a_raw"], col["w_idx"]], axis=1))
    return out


def _mmf(x, wb, n):
    shp = x.shape
    y = _mm(x.reshape(-1, shp[-1]).astype(jnp.bfloat16), wb)[:, :n]
    return y.reshape(*shp[:-1], n)


def _layernorm(x, g, b):
    mu = jnp.mean(x, -1, keepdims=True)
    var = jnp.mean(jnp.square(x - mu), -1, keepdims=True)
    return (x - mu) * lax.rsqrt(var + LN_EPS) * g + b


def _rope(x, pos):
    half = x.shape[-1] // 2
    inv = ROPE_THETA ** (-jnp.arange(half, dtype=jnp.float32) / half)
    ang = pos.astype(jnp.float32)[:, None] * inv[None, :]
    cos = jnp.cos(ang)[None, :, None, :]
    sin = jnp.sin(ang)[None, :, None, :]
    x1 = x[..., :half]
    x2 = x[..., half:]
    return jnp.concatenate([x1 * cos - x2 * sin, x2 * cos + x1 * sin], -1)


def _l2n(a):
    return a * lax.rsqrt(jnp.sum(a * a, -1, keepdims=True) + RMS_EPS)


def _causal_conv(u, buf, conv_w):
    t = u.shape[1]
    up = jnp.concatenate([buf.astype(u.dtype), u], axis=1)
    out = sum(up[:, j:j + t] * conv_w[j] for j in range(CONV_W))
    return jax.nn.silu(out), up[:, t:]


def _gdn_qkv(conv_out):
    b, t, _ = conv_out.shape
    q, k, v = (a.reshape(b, t, H_GDN, D_HEAD) for a in jnp.split(conv_out, 3, axis=-1))
    return _l2n(q) * D_HEAD ** -0.5, _l2n(k), v


def _gdn_gates(b_raw, a_raw, a_log, dt_bias):
    beta = jax.nn.sigmoid(b_raw)
    g = -jnp.exp(a_log) * jax.nn.softplus(a_raw + dt_bias)
    return g, beta


def _gdn_chunked(q, k, v, g, beta, s0):
    b, t, h, _ = q.shape
    c = GDN_CHUNK
    n = t // c

    def ch(a):
        return jnp.swapaxes(jnp.moveaxis(a.reshape(b, n, c, h, *a.shape[3:]), 1, 0), 2, 3)

    qc, kc, vc, gc, bc = ch(q), ch(k), ch(v), ch(g), ch(beta)
    gam = jnp.cumsum(gc, axis=-1)
    incl = jnp.tril(jnp.ones((c, c), bool))
    strict = jnp.tril(jnp.ones((c, c), bool), -1)
    decay = jnp.exp(jnp.where(incl, gam[..., :, None] - gam[..., None, :], -jnp.inf))
    kb = kc * bc[..., None]
    a_mat = jnp.where(strict, jnp.einsum('nbhid,nbhjd->nbhij', kb, kc) * decay, 0.0) + jnp.eye(c, dtype=jnp.float32)
    solve = functools.partial(lax.linalg.triangular_solve, left_side=True, lower=True, unit_diagonal=True)
    u = solve(a_mat, vc * bc[..., None])
    w = solve(a_mat, kb * jnp.exp(gam)[..., None])
    qk = jnp.einsum('nbhid,nbhjd->nbhij', qc, kc) * decay

    g_last = gam[..., -1:]
    qg = qc * jnp.exp(gam)[..., None]
    kg = kc * jnp.exp(g_last - gam)[..., None]
    decay_last = jnp.broadcast_to(jnp.exp(g_last)[..., None], (n, b, h, 1, D_HEAD))
    outs = [_gdn_scan(s0[i], *(a[:, i] for a in (u, w, qg, kg, qk, decay_last))) for i in range(b)]
    o = jnp.stack([x[0] for x in outs], axis=1)
    s = jnp.stack([x[1] for x in outs], axis=0)
    o = jnp.moveaxis(jnp.swapaxes(o, 2, 3), 0, 1).reshape(b, t, h, -1)
    return o, s


def _gdn_scan_kernel(s0_ref, u_ref, w_ref, qg_ref, kg_ref, qk_ref, dl_ref, o_ref, s_ref):
    bf = jnp.bfloat16

    @pl.when(pl.program_id(0) == 0)
    def _():
        s_ref[...] = s0_ref[...]

    for hh in range(s_ref.shape[0]):
        s = s_ref[hh]
        sb = s.astype(bf)
        v_new = u_ref[0, hh] - jnp.dot(w_ref[0, hh].astype(bf), sb, preferred_element_type=jnp.float32)
        vb = v_new.astype(bf)
        o_ref[0, hh] = (jnp.dot(qg_ref[0, hh].astype(bf), sb, preferred_element_type=jnp.float32)
                        + jnp.dot(qk_ref[0, hh].astype(bf), vb, preferred_element_type=jnp.float32))
        s_ref[hh] = s * dl_ref[0, hh] + lax.dot_general(
            kg_ref[0, hh].astype(bf), vb, (((0,), (0,)), ((), ())), preferred_element_type=jnp.float32)


def _gdn_scan(s0, u, w, qg, kg, qk, decay_last):
    n, h, c, d = u.shape
    blk = lambda *shape: pl.BlockSpec((1,) + shape, lambda i: (i,) + (0,) * len(shape))
    state = pl.BlockSpec((h, d, d), lambda i: (0, 0, 0))
    o, s = pl.pallas_call(
        _gdn_scan_kernel,
        grid=(n,),
        in_specs=[state, blk(h, c, d), blk(h, c, d), blk(h, c, d), blk(h, c, d), blk(h, c, c), blk(h, 1, d)],
        out_specs=[blk(h, c, d), state],
        out_shape=[jax.ShapeDtypeStruct((n, h, c, d), jnp.float32), jax.ShapeDtypeStruct((h, d, d), jnp.float32)],
        compiler_params=pltpu.CompilerParams(dimension_semantics=("arbitrary",), vmem_limit_bytes=VMEM_LIMIT),
        name="gdn_scan",
    )(s0, u, w, qg, kg, qk, decay_last)
    return o, s


def _gdn_recurrent(q, k, v, g, beta, s0):
    def step(s, inp):
        q_t, k_t, v_t, g_t, b_t = inp
        s = s * jnp.exp(g_t)[..., None, None]
        ks = jnp.einsum('bhk,bhkv->bhv', k_t, s)
        s = s + jnp.einsum('bhk,bhv->bhkv', k_t * b_t[..., None], v_t - ks)
        return s, jnp.einsum('bhk,bhkv->bhv', q_t, s)

    tm = lambda a: jnp.moveaxis(a, 1, 0)
    s, o = lax.scan(step, s0, (tm(q), tm(k), tm(v), tm(g), tm(beta)))
    return jnp.moveaxis(o, 0, 1), s


def _gdn_out(o, z, norm_w):
    b, t = z.shape[:2]
    o = o * lax.rsqrt(jnp.mean(o * o, -1, keepdims=True) + RMS_EPS) * norm_w
    return o.reshape(b, t, D_GDN) * jax.nn.silu(z)


def _att_heads(q_att, k_att, v_att, q_idx, w_idx, k_idx, pos):
    b, t, _ = q_att.shape
    qa = _rope(q_att.reshape(b, t, H_ATT, D_HEAD), pos)
    ka = _rope(k_att.reshape(b, t, H_ATT, D_HEAD), pos)
    va = v_att.reshape(b, t, H_ATT, D_HEAD)
    qi = _rope(q_idx.reshape(b, t, H_IDX, D_IDX), pos)
    ki = _rope(k_idx[:, :, None, :], pos)[:, :, 0]
    wi = w_idx * (H_IDX ** -0.5 * D_IDX ** -0.5)
    return qa, ka, va, qi, wi, ki


def _indexer_topk(qi, wi, ki, q_pos, n_sel):
    rel = jax.nn.relu(jnp.einsum('bthd,bsd->bths', qi, ki))
    score = jnp.einsum('bths,bth->bts', rel, wi).astype(jnp.float32)
    key_pos = jnp.arange(ki.shape[1])
    score = jnp.where(key_pos[None, None, :] <= q_pos[None, :, None], score, -jnp.inf)
    _, sel = lax.top_k(score, n_sel)
    return sel


def _sparse_attend(q, k_sel, v_sel, sel, q_pos):
    logits = jnp.einsum('bthd,btkhd->bthk', q, k_sel).astype(jnp.float32) * D_HEAD ** -0.5
    valid = (sel <= q_pos[None, :, None])[:, :, None, :]
    p = jax.nn.softmax(jnp.where(valid, logits, -jnp.inf), axis=-1)
    return jnp.einsum('bthk,btkhd->bthd', p.astype(v_sel.dtype), v_sel)


MXU_COLS = 256
DSA_TQ = 128
DSA_TK = 256
DSA_HG = max(1, MXU_COLS // DSA_TQ)
INT32_MIN = np.int32(-2 ** 31)


def _float_to_key(x):
    bits = lax.bitcast_convert_type(x + 0.0, jnp.int32)
    return bits ^ ((bits >> 31) & jnp.int32(0x7FFFFFFF))


def _kth_largest_key(xs_ref, nch, ch, n_sel):
    lanes = xs_ref.shape[1]

    def count_ge(cand):
        def body(c, acc):
            r0 = pl.multiple_of(c * ch, ch)
            ind = jnp.where(xs_ref[pl.ds(r0, ch), :] >= cand, 1, 0).astype(jnp.int32)
            return acc + jnp.sum(ind.reshape(ch // 8, 8, lanes), axis=0)

        acc = lax.fori_loop(0, nch, body, jnp.zeros((8, lanes), jnp.int32))
        return jnp.sum(acc, axis=0, keepdims=True)

    def bit_cond(st):
        b, _, open_ = st
        return jnp.logical_and(b < 32, jnp.max(open_) > 0)

    def bit_body(st):
        b, thr_u, open_ = st
        cand_u = thr_u | lax.shift_left(jnp.int32(1), 31 - b)
        cnt = count_ge(cand_u ^ INT32_MIN)
        take = jnp.logical_and(cnt >= n_sel, open_ > 0)
        return b + 1, jnp.where(take, cand_u, thr_u), jnp.where(cnt == n_sel, 0, open_)

    _, thr_u, _ = lax.while_loop(
        bit_cond, bit_body,
        (jnp.int32(0), jnp.zeros((1, lanes), jnp.int32), jnp.ones((1, lanes), jnp.int32)))
    return thr_u ^ INT32_MIN


def _dsa_prompt_kernel(ii_ref, jj_ref, qi_ref, w_ref, ki_ref, q_ref, k_ref, vt_ref, o_ref,
                       xs_ref, thr_ref, bias_ref, m_ref, l_ref, acc_ref, *, n_sel):
    s_id = pl.program_id(0)
    i = ii_ref[s_id]
    j = jj_ref[s_id]
    last = (i * DSA_TQ + DSA_TQ - 1) // DSA_TK
    nch = last + 1
    kiota = lax.broadcasted_iota(jnp.int32, (DSA_TK, DSA_TQ), 0)
    qpos = i * DSA_TQ + lax.broadcasted_iota(jnp.int32, (DSA_TK, DSA_TQ), 1)

    @pl.when(j == 0)
    def _index():
        def score_body(c, carry):
            r0 = pl.multiple_of(c * DSA_TK, DSA_TK)
            kchunk = ki_ref[pl.ds(r0, DSA_TK), :]
            sc = jnp.zeros((DSA_TK, DSA_TQ), jnp.float32)
            for hg in range(H_IDX // DSA_HG):
                z = jnp.dot(kchunk, qi_ref[0, hg], preferred_element_type=jnp.float32)
                for sub in range(DSA_HG):
                    hd = hg * DSA_HG + sub
                    sc = sc + w_ref[0, hd:hd + 1, :] * jnp.maximum(z[:, sub * DSA_TQ:(sub + 1) * DSA_TQ], 0.0)
            xs_ref[pl.ds(r0, DSA_TK), :] = jnp.where(r0 + kiota <= qpos, _float_to_key(sc), INT32_MIN)
            return carry

        lax.fori_loop(0, nch, score_body, 0)

        thr_ref[...] = _kth_largest_key(xs_ref, nch, DSA_TK, n_sel)
        m_ref[...] = jnp.full(m_ref.shape, -1e30, jnp.float32)
        l_ref[...] = jnp.zeros(l_ref.shape, jnp.float32)
        acc_ref[...] = jnp.zeros(acc_ref.shape, jnp.float32)

    r0 = pl.multiple_of(j * DSA_TK, DSA_TK)
    sel = jnp.where(xs_ref[pl.ds(r0, DSA_TK), :] >= thr_ref[...], r0 + kiota, jnp.int32(2 ** 30)) <= qpos
    bias_ref[...] = jnp.where(sel, 0.0, -jnp.inf)
    for h in range(H_ATT):
        s = jnp.dot(k_ref[:, h * D_HEAD:(h + 1) * D_HEAD], q_ref[h], preferred_element_type=jnp.float32)
        s = s + bias_ref[...]
        m_old = m_ref[h:h + 1, :]
        m_new = jnp.maximum(m_old, jnp.max(s, axis=0, keepdims=True))
        alpha = jnp.exp(m_old - m_new)
        p = jnp.exp(s - m_new)
        l_ref[h:h + 1, :] = alpha * l_ref[h:h + 1, :] + jnp.sum(p, axis=0, keepdims=True)
        m_ref[h:h + 1, :] = m_new
        acc_ref[h] = alpha * acc_ref[h] + jnp.dot(vt_ref[h], p.astype(jnp.bfloat16),
                                                  preferred_element_type=jnp.float32)

    @pl.when(j == last)
    def _finish():
        for h in range(H_ATT):
            o_ref[:, h * D_HEAD:(h + 1) * D_HEAD] = (acc_ref[h] / l_ref[h:h + 1, :]).T


def _dsa_prompt_one(q, k, v, qi, wi, ki):
    t = q.shape[0]
    assert t % DSA_TK == 0 and DSA_TK % DSA_TQ == 0
    nq = t // DSA_TQ
    n_sel = min(TOPK_ATT_MAX, t // 4)
    bf = jnp.bfloat16
    qit = qi.reshape(nq, DSA_TQ, H_IDX // DSA_HG, DSA_HG, D_IDX).transpose(0, 2, 4, 3, 1).reshape(
        nq, H_IDX // DSA_HG, D_IDX, DSA_HG * DSA_TQ).astype(bf)
    wr = wi.reshape(nq, DSA_TQ, H_IDX).transpose(0, 2, 1)
    qt = (q * D_HEAD ** -0.5).transpose(1, 2, 0).astype(bf)
    kf = k.reshape(t, D_ATT).astype(bf)
    vt = v.transpose(1, 2, 0).astype(bf)
    steps = [(i, j) for i in range(nq) for j in range((i * DSA_TQ + DSA_TQ - 1) // DSA_TK + 1)]
    ii = jnp.asarray([s[0] for s in steps], jnp.int32)
    jj = jnp.asarray([s[1] for s in steps], jnp.int32)
    grid_spec = pltpu.PrefetchScalarGridSpec(
        num_scalar_prefetch=2,
        grid=(len(steps),),
        in_specs=[
            pl.BlockSpec((1, H_IDX // DSA_HG, D_IDX, DSA_HG * DSA_TQ), lambda s, ii, jj: (ii[s], 0, 0, 0)),
            pl.BlockSpec((1, H_IDX, DSA_TQ), lambda s, ii, jj: (ii[s], 0, 0)),
            pl.BlockSpec((t, D_IDX), lambda s, ii, jj: (0, 0)),
            pl.BlockSpec((H_ATT, D_HEAD, DSA_TQ), lambda s, ii, jj: (0, 0, ii[s])),
            pl.BlockSpec((DSA_TK, D_ATT), lambda s, ii, jj: (jj[s], 0)),
            pl.BlockSpec((H_ATT, D_HEAD, DSA_TK), lambda s, ii, jj: (0, 0, jj[s])),
        ],
        out_specs=pl.BlockSpec((DSA_TQ, D_ATT), lambda s, ii, jj: (ii[s], 0)),
        scratch_shapes=[
            pltpu.VMEM((t, DSA_TQ), jnp.int32),
            pltpu.VMEM((1, DSA_TQ), jnp.int32),
            pltpu.VMEM((DSA_TK, DSA_TQ), jnp.float32),
            pltpu.VMEM((H_ATT, DSA_TQ), jnp.float32),
            pltpu.VMEM((H_ATT, DSA_TQ), jnp.float32),
            pltpu.VMEM((H_ATT, D_HEAD, DSA_TQ), jnp.float32),
        ])
    return pl.pallas_call(
        functools.partial(_dsa_prompt_kernel, n_sel=n_sel),
        grid_spec=grid_spec,
        out_shape=jax.ShapeDtypeStruct((t, D_ATT), jnp.float32),
        compiler_params=pltpu.CompilerParams(dimension_semantics=("arbitrary",), vmem_limit_bytes=VMEM_LIMIT),
        name="dsa_prompt",
    )(ii, jj, qit, wr, ki.astype(bf), qt, kf, vt)


def _dsa_prompt(q, k, v, qi, wi, ki):
    return jnp.stack([_dsa_prompt_one(q[b], k[b], v[b], qi[b], wi[b], ki[b]) for b in range(q.shape[0])])


SMP_IDX_PAGES = 8
SMP_ATT_PAGES = 4
SMP_CH = 128


def _smp_score_kernel(pt_ref, *refs):
    pages = refs[:SMP_IDX_PAGES]
    qi_ref, wb_ref, o_ref = refs[SMP_IDX_PAGES:]
    nq = o_ref.shape[1]
    for g, ki_ref in enumerate(pages):
        z = lax.dot_general(qi_ref[0], ki_ref[0, 0].astype(jnp.bfloat16), (((1,), (1,)), ((), ())),
                            preferred_element_type=jnp.float32)
        r = jnp.maximum(z, 0.0) * wb_ref[0]
        o_ref[0, :, g * PAGE_SIZE:(g + 1) * PAGE_SIZE] = jnp.sum(r.reshape(nq, H_IDX, PAGE_SIZE), axis=1)


def _smp_thr_kernel(s_ref, thr_ref, xs_ref, *, n_sel):
    nch = s_ref.shape[0] // SMP_CH

    def conv(c, carry):
        r0 = pl.multiple_of(c * SMP_CH, SMP_CH)
        xs_ref[pl.ds(r0, SMP_CH), :] = _float_to_key(s_ref[pl.ds(r0, SMP_CH), :])
        return carry

    lax.fori_loop(0, nch, conv, 0)
    key = _kth_largest_key(xs_ref, nch, SMP_CH, n_sel)
    thr_ref[...] = lax.bitcast_convert_type(key ^ ((key >> 31) & jnp.int32(0x7FFFFFFF)), jnp.float32)


def _smp_attn_kernel(pt_ref, *refs, nq):
    kp = refs[:SMP_ATT_PAGES]
    vp = refs[SMP_ATT_PAGES:2 * SMP_ATT_PAGES]
    (sc_ref, scn_ref, thr_ref, q_ref, kn_ref, vn_ref, ex_ref, hm_ref,
     o_ref, m_ref, l_ref, acc_ref) = refs[2 * SMP_ATT_PAGES:]
    pp = pl.program_id(1)
    rows = PAGE_SIZE * H_ATT

    def attend(k, v, sc):
        sel = jnp.where(sc >= thr_ref[0], 1.0, 0.0)
        sel = jnp.concatenate(
            [jnp.broadcast_to(sel[tt:tt + 1, :], (H_ATT, PAGE_SIZE)) for tt in range(nq)], axis=0)
        valid = jnp.dot(sel.astype(jnp.bfloat16), ex_ref[...], preferred_element_type=jnp.float32) * hm_ref[...]
        s = lax.dot_general(q_ref[0], k.reshape(rows, D_HEAD).astype(jnp.bfloat16), (((1,), (1,)), ((), ())),
                            preferred_element_type=jnp.float32)
        s = jnp.where(valid > 0.5, s, -jnp.inf)
        m_old = m_ref[...]
        m_new = jnp.maximum(m_old, jnp.max(s, axis=1, keepdims=True))
        alpha = jnp.exp(m_old - m_new)
        p = jnp.exp(s - m_new)
        l_ref[...] = alpha * l_ref[...] + jnp.sum(p, axis=1, keepdims=True)
        m_ref[...] = m_new
        acc_ref[...] = alpha * acc_ref[...] + jnp.dot(
            p.astype(jnp.bfloat16), v.reshape(rows, D_HEAD).astype(jnp.bfloat16), preferred_element_type=jnp.float32)

    @pl.when(pp == 0)
    def _():
        m_ref[...] = jnp.full(m_ref.shape, -1e30, jnp.float32)
        l_ref[...] = jnp.zeros(l_ref.shape, jnp.float32)
        acc_ref[...] = jnp.zeros(acc_ref.shape, jnp.float32)
        attend(kn_ref[0], vn_ref[0], scn_ref[0])

    for g in range(SMP_ATT_PAGES):
        attend(kp[g][0, 0], vp[g][0, 0], sc_ref[0, :, g * PAGE_SIZE:(g + 1) * PAGE_SIZE])

    @pl.when(pp == pl.num_programs(1) - 1)
    def _():
        o_ref[0] = acc_ref[...] / l_ref[...]


def _dsa_sample(q, k_new, v_new, qi, wi, ki_new, cache_k, cache_v, cache_idx_k, page_table, layer):
    db, t = q.shape[:2]
    npg = page_table.shape[1]
    past_len = npg * PAGE_SIZE
    n_sel = min(TOPK_ATT_MAX, (past_len + t) // 4)
    assert npg % SMP_IDX_PAGES == 0 and npg % SMP_ATT_PAGES == 0 and (t * H_IDX) % 8 == 0
    bf = jnp.bfloat16
    pt = page_table.reshape(-1).astype(jnp.int32)
    n_pool = cache_k.shape[1]

    def page_spec(width, per_step, g):
        return pl.BlockSpec((1, 1, PAGE_SIZE, width),
                            lambda b, p, pt: (layer, pt[b * npg + p * per_step + g], 0, 0))

    row_spec = lambda rows, width: pl.BlockSpec((1, rows, width), lambda b, p, pt: (b, 0, 0))
    score_past = pl.pallas_call(
        _smp_score_kernel,
        grid_spec=pltpu.PrefetchScalarGridSpec(
            num_scalar_prefetch=1, grid=(db, npg // SMP_IDX_PAGES),
            in_specs=[page_spec(D_IDX, SMP_IDX_PAGES, g) for g in range(SMP_IDX_PAGES)]
            + [row_spec(t * H_IDX, D_IDX), row_spec(t * H_IDX, PAGE_SIZE)],
            out_specs=pl.BlockSpec((1, t, SMP_IDX_PAGES * PAGE_SIZE), lambda b, p, pt: (b, 0, p))),
        out_shape=jax.ShapeDtypeStruct((db, t, past_len), jnp.float32),
        compiler_params=pltpu.CompilerParams(dimension_semantics=("parallel", "arbitrary"),
                                             vmem_limit_bytes=VMEM_LIMIT),
        name="smp_score",
    )(pt, *([cache_idx_k] * SMP_IDX_PAGES), qi.reshape(db, t * H_IDX, D_IDX).astype(bf),
      jnp.broadcast_to(wi.reshape(db, t * H_IDX, 1), (db, t * H_IDX, PAGE_SIZE)))

    rel = jax.nn.relu(jnp.einsum('bthd,bsd->bths', qi, ki_new))
    score_new = jnp.einsum('bths,bth->bts', rel, wi)
    score_new = jnp.where(jnp.arange(t)[None, None, :] <= jnp.arange(t)[None, :, None], score_new, -jnp.inf)
    score_new = jnp.pad(score_new, ((0, 0), (0, 0), (0, PAGE_SIZE - t)), constant_values=-jnp.inf)

    score_t = jnp.concatenate([score_past, score_new], axis=2).reshape(db * t, past_len + PAGE_SIZE).T
    thr = pl.pallas_call(
        functools.partial(_smp_thr_kernel, n_sel=n_sel),
        out_shape=jax.ShapeDtypeStruct((1, db * t), jnp.float32),
        scratch_shapes=[pltpu.VMEM(score_t.shape, jnp.int32)],
        compiler_params=pltpu.CompilerParams(vmem_limit_bytes=VMEM_LIMIT),
        name="smp_thr",
    )(score_t)
    thr_b = jnp.broadcast_to(thr.reshape(db, t, 1), (db, t, PAGE_SIZE))

    nrow = t * H_ATT
    cols = PAGE_SIZE * H_ATT
    q_rows = (q * D_HEAD ** -0.5).reshape(db, nrow, D_HEAD).astype(bf)
    expand = jnp.repeat(jnp.eye(PAGE_SIZE, dtype=bf), H_ATT, axis=1)
    head_match = (jnp.arange(cols)[None, :] % H_ATT == jnp.arange(nrow)[:, None] % H_ATT).astype(jnp.float32)
    pad_page = lambda a: jnp.pad(a, ((0, 0), (0, PAGE_SIZE - t), (0, 0), (0, 0)))
    cache_spec = lambda g: pl.BlockSpec(
        (1, 1, PAGE_SIZE, H_ATT, D_HEAD),
        lambda b, p, pt: (layer, pt[b * npg + p * SMP_ATT_PAGES + g], 0, 0, 0))
    const_spec = lambda shape: pl.BlockSpec(shape, lambda b, p, pt: (0,) * len(shape))
    new_spec = pl.BlockSpec((1, PAGE_SIZE, H_ATT, D_HEAD), lambda b, p, pt: (b, 0, 0, 0))
    out = pl.pallas_call(
        functools.partial(_smp_attn_kernel, nq=t),
        grid_spec=pltpu.PrefetchScalarGridSpec(
            num_scalar_prefetch=1, grid=(db, npg // SMP_ATT_PAGES),
            in_specs=[cache_spec(g) for g in range(SMP_ATT_PAGES)] * 2
            + [pl.BlockSpec((1, t, SMP_ATT_PAGES * PAGE_SIZE), lambda b, p, pt: (b, 0, p)),
               row_spec(t, PAGE_SIZE), row_spec(t, PAGE_SIZE), row_spec(nrow, D_HEAD),
               new_spec, new_spec, const_spec((PAGE_SIZE, cols)), const_spec((nrow, cols))],
            out_specs=row_spec(nrow, D_HEAD),
            scratch_shapes=[pltpu.VMEM((nrow, 1), jnp.float32), pltpu.VMEM((nrow, 1), jnp.float32),
                            pltpu.VMEM((nrow, D_HEAD), jnp.float32)]),
        out_shape=jax.ShapeDtypeStruct((db, nrow, D_HEAD), jnp.float32),
        compiler_params=pltpu.CompilerParams(dimension_semantics=("parallel", "arbitrary"),
                                             vmem_limit_bytes=VMEM_LIMIT),
        name="smp_attn",
    )(pt, *([cache_k] * SMP_ATT_PAGES), *([cache_v] * SMP_ATT_PAGES),
      score_past, score_new, thr_b, q_rows, pad_page(k_new), pad_page(v_new), expand, head_match)
    return out.reshape(db, t, D_ATT)


def _peer(x, wq, k1, k2, u, v):
    shape = x.shape
    xf = x.reshape(-1, D_MODEL)
    n = xf.shape[0]
    pad = (-n) % PEER_BLOCK
    xf = jnp.pad(xf, ((0, pad), (0, 0)))
    xb16 = xf.astype(jnp.bfloat16)
    e1, e2, gate = _peer_route(_mm(xb16, wq), k1, k2)
    wmat = _peer_gate_matrix(e1, e2, gate)
    out = _peer_dense(xb16, u, v, wmat)
    return out[:n].reshape(shape)


PEER_RT = 128
PEER_RH = 2
PEER_TOPK_LOG2 = PEER_TOPK.bit_length() - 1
assert 1 << PEER_TOPK_LOG2 == PEER_TOPK
PEER_PAIRS = tuple((i, j) for i in range(PEER_TOPK) for j in range(PEER_TOPK) if (i + 1) * (j + 1) <= PEER_TOPK)


def _extract_top(s, ids, n):
    big = jnp.int32(2 ** 30)
    vals, sel = [], []
    for _ in range(n):
        m = jnp.max(s, axis=0, keepdims=True)
        pick = jnp.min(jnp.where(s == m, ids, big), axis=0, keepdims=True)
        vals.append(m)
        sel.append(pick)
        s = jnp.where(ids == pick, -jnp.inf, s)
    return vals, sel


def _peer_route_kernel(q_ref, k1_ref, k2_ref, e1_ref, e2_ref, g_ref):
    kio = lax.broadcasted_iota(jnp.int32, (PEER_NKEYS, PEER_RT), 0)
    flat = jnp.concatenate(
        [jnp.full((1, PEER_RT), i * PEER_TOPK + j, jnp.int32) for i, j in PEER_PAIRS], axis=0)
    for hh in range(PEER_RH):
        tops = []
        for side, kref in ((0, k1_ref), (1, k2_ref)):
            lo = (2 * hh + side) * (PEER_DQ // 2)
            qs = q_ref[:, lo:lo + PEER_DQ // 2].astype(jnp.bfloat16)
            s = lax.dot_general(kref[hh], qs, (((1,), (1,)), ((), ())),
                                preferred_element_type=jnp.float32)
            tops.append(_extract_top(s, kio, PEER_TOPK))
        (v1, i1), (v2, i2) = tops
        cand = jnp.concatenate([v1[i] + v2[j] for i, j in PEER_PAIRS], axis=0)
        top_s, top_f = _extract_top(cand, flat, PEER_TOPK)
        ex = [jnp.exp(t - top_s[0]) for t in top_s]
        inv = 1.0 / sum(ex)
        for r in range(PEER_TOPK):
            fi = top_f[r] >> PEER_TOPK_LOG2
            fj = top_f[r] & (PEER_TOPK - 1)
            e1 = sum(jnp.where(fi == i, i1[i], 0) for i in range(PEER_TOPK))
            e2 = sum(jnp.where(fj == j, i2[j], 0) for j in range(PEER_TOPK))
            e1_ref[hh, r:r + 1, :] = e1
            e2_ref[hh, r:r + 1, :] = e2
            g_ref[hh, r:r + 1, :] = ex[r] * inv


def _peer_route(q, k1, k2):
    n = q.shape[0]
    assert n % PEER_RT == 0 and PEER_HEADS % PEER_RH == 0
    kspec = pl.BlockSpec((PEER_RH, PEER_NKEYS, PEER_DQ // 2), lambda i, h: (h, 0, 0))
    ospec = pl.BlockSpec((PEER_RH, PEER_TOPK, PEER_RT), lambda i, h: (h, 0, i))
    oshape = lambda dt: jax.ShapeDtypeStruct((PEER_HEADS, PEER_TOPK, n), dt)
    e1, e2, g = pl.pallas_call(
        _peer_route_kernel,
        grid=(n // PEER_RT, PEER_HEADS // PEER_RH),
        in_specs=[pl.BlockSpec((PEER_RT, PEER_RH * PEER_DQ), lambda i, h: (i, h)), kspec, kspec],
        out_specs=[ospec, ospec, ospec],
        out_shape=[oshape(jnp.int32), oshape(jnp.int32), oshape(jnp.float32)],
        compiler_params=pltpu.CompilerParams(dimension_semantics=("parallel", "parallel"),
                                             vmem_limit_bytes=VMEM_LIMIT),
        name="peer_route",
    )(q, k1.astype(jnp.bfloat16), k2.astype(jnp.bfloat16))
    tok_major = lambda a: a.reshape(PEER_HEADS * PEER_TOPK, n).T
    return tok_major(e1), tok_major(e2), tok_major(g)


PEER_GT = 16


def _peer_gate_kernel(i1_ref, i2_ref, g_ref, w_ref, tmp_ref):
    iota = lax.broadcasted_iota(jnp.int32, (PEER_NKEYS, PEER_HEADS * PEER_TOPK), 0)

    def body(gi, carry):
        t0 = pl.multiple_of(gi * PEER_GT, PEER_GT)
        for kk in range(PEER_GT):
            a = jnp.where(iota == i1_ref[pl.ds(t0 + kk, 1), :], g_ref[pl.ds(t0 + kk, 1), :], 0.0).astype(jnp.bfloat16)
            b = jnp.where(iota == i2_ref[pl.ds(t0 + kk, 1), :], 1.0, 0.0).astype(jnp.bfloat16)
            tmp_ref[kk * PEER_NKEYS:(kk + 1) * PEER_NKEYS, :] = lax.dot_general(
                a, b, (((1,), (1,)), ((), ())), preferred_element_type=jnp.float32)
        for aa in range(PEER_NKEYS):
            rows = tmp_ref[pl.ds(aa, PEER_GT, stride=PEER_NKEYS), :]
            w_ref[pl.ds(t0, PEER_GT), aa * PEER_NKEYS:(aa + 1) * PEER_NKEYS] = rows.astype(w_ref.dtype)
        return carry

    lax.fori_loop(0, i1_ref.shape[0] // PEER_GT, body, 0)


def _peer_gate_matrix(e1, e2, gate):
    n, p = e1.shape
    tw = _pick(n, (128,))
    assert tw % PEER_GT == 0
    spec = pl.BlockSpec((tw, p), lambda i: (i, 0))
    return pl.pallas_call(
        _peer_gate_kernel,
        grid=(n // tw,),
        in_specs=[spec, spec, spec],
        out_specs=pl.BlockSpec((tw, PEER_NKEYS * PEER_NKEYS), lambda i: (i, 0)),
        out_shape=jax.ShapeDtypeStruct((n, PEER_NKEYS * PEER_NKEYS), jnp.bfloat16),
        scratch_shapes=[pltpu.VMEM((PEER_GT * PEER_NKEYS, PEER_NKEYS), jnp.float32)],
        compiler_params=pltpu.CompilerParams(dimension_semantics=("parallel",), vmem_limit_bytes=VMEM_LIMIT),
        name="peer_gate",
    )(e1, e2, gate)


def _peer_dense_kernel(h_ref, ut_ref, w_ref, v_ref, o_ref):
    @pl.when(pl.program_id(1) == 0)
    def _():
        o_ref[...] = jnp.zeros_like(o_ref)

    half = ut_ref.shape[1] // 2
    upd = None
    for s in range(2):
        z = jnp.dot(h_ref[...], ut_ref[:, s * half:(s + 1) * half], preferred_element_type=jnp.float32)
        act = 0.5 * z * (1.0 + lax.erf(z * (2.0 ** -0.5)))
        c = (w_ref[:, s * half:(s + 1) * half].astype(jnp.float32) * act).astype(jnp.bfloat16)
        d = jnp.dot(c, v_ref[s * half:(s + 1) * half, :], preferred_element_type=jnp.float32)
        upd = d if upd is None else upd + d
    o_ref[...] += upd


def _peer_dense(xb16, ut, vb, wmat):
    n = xb16.shape[0]
    e = vb.shape[0]
    tm = _pick(n, (512, 256, 128))
    te = 512
    return pl.pallas_call(
        _peer_dense_kernel,
        grid=(n // tm, e // te),
        in_specs=[pl.BlockSpec((tm, D_MODEL), lambda i, j: (i, 0)),
                  pl.BlockSpec((D_MODEL, te), lambda i, j: (0, j)),
                  pl.BlockSpec((tm, te), lambda i, j: (i, j)),
                  pl.BlockSpec((te, D_MODEL), lambda i, j: (j, 0))],
        out_specs=pl.BlockSpec((tm, D_MODEL), lambda i, j: (i, 0)),
        out_shape=jax.ShapeDtypeStruct((n, D_MODEL), jnp.float32),
        compiler_params=pltpu.CompilerParams(dimension_semantics=("parallel", "arbitrary"),
                                             vmem_limit_bytes=56 * 1024 * 1024),
        name="peer_dense",
    )(xb16, ut, wmat, vb)


def _group(x, pos, conv_buf, ssm0, chunked, attend, w_in, conv_w, a_log, dt_bias, gdn_norm_w,
           w_br_gdn, w_br_att, w_out, ln1_g, ln1_b, peer_wq, peer_k1, peer_k2, peer_u, peer_v, ln2_g, ln2_b):
    xb = x.reshape(-1, D_MODEL).astype(jnp.bfloat16)
    proj = lambda name: _mm(xb, w_in[name]).reshape(*x.shape[:-1], -1)
    qkv, z, q_att, k_att, v_att, q_idx, k_idx, gate_gdn, gate_att = (
        proj(n) for n in ("qkv", "z", "q_att", "k_att", "v_att", "q_idx", "k_idx", "gate_gdn", "gate_att"))
    small = proj("small")
    b_raw, a_raw, w_idx = small[..., :H_GDN], small[..., H_GDN:2 * H_GDN], small[..., 2 * H_GDN:2 * H_GDN + H_IDX]
    conv_out, conv_new = _causal_conv(qkv, conv_buf, conv_w)
    q, k, v = _gdn_qkv(conv_out)
    g, beta = _gdn_gates(b_raw, a_raw, a_log, dt_bias)
    gdn = _gdn_chunked if chunked else _gdn_recurrent
    o, ssm_new = gdn(q, k, v, g, beta, ssm0.astype(jnp.float32))
    o_gdn = _gdn_out(o, z, gdn_norm_w)
    qa, ka, va, qi, wi, ki = _att_heads(q_att, k_att, v_att, q_idx, w_idx, k_idx, pos)
    o_att = attend(qa, ka, va, qi, wi, ki)
    mix = _mmf(jax.nn.sigmoid(gate_gdn) * _mmf(o_gdn, w_br_gdn, D_MODEL) + jax.nn.sigmoid(gate_att) * _mmf(o_att, w_br_att, D_MODEL),
               w_out, D_MODEL)
    h = _layernorm(DEEPNORM_ALPHA * x + mix, ln1_g, ln1_b)
    y = _layernorm(DEEPNORM_ALPHA * h + _peer(h, peer_wq, peer_k1, peer_k2, peer_u, peer_v), ln2_g, ln2_b)
    return y, ka, va, ki, conv_new, ssm_new.astype(x.dtype)


def kernel(x_prompt, x_sample, cache_k, cache_v, cache_idx_k, state_conv, state_ssm, page_table, w_in, conv_w, a_log, dt_bias, gdn_norm_w, w_br_gdn, w_br_att, w_out, ln1_g, ln1_b, peer_wq, peer_k1, peer_k2, peer_u, peer_v, ln2_g, ln2_b):
    bp, sp = x_prompt.shape[:2]
    ds = x_sample.shape[1]
    past_len = page_table.shape[1] * PAGE_SIZE
    pos_p = jnp.arange(sp)
    pos_s = past_len + jnp.arange(ds)
    l = 0
    wts = (_in_proj_weights(w_in[l]), conv_w[l], a_log[l], dt_bias[l], gdn_norm_w[l], _wcast(w_br_gdn[l]), _wcast(w_br_att[l]),
           _wcast(w_out[l]), ln1_g[l], ln1_b[l], _wcast(peer_wq[l]), peer_k1[l], peer_k2[l],
           peer_u[l].astype(jnp.bfloat16).T, peer_v[l].astype(jnp.bfloat16), ln2_g[l], ln2_b[l])
    y_p, kp, vp, ip, cp, ssp = _group(
        x_prompt, pos_p, jnp.zeros((bp, CONV_W - 1, CONV_DIM), x_prompt.dtype),
        jnp.zeros((bp, H_GDN, D_HEAD, D_HEAD), jnp.float32), True, _dsa_prompt, *wts)
    attend_s = functools.partial(_dsa_sample, cache_k=cache_k, cache_v=cache_v, cache_idx_k=cache_idx_k,
                                 page_table=page_table, layer=l)
    y_s, kss, vss, iss, css, sss = _group(
        x_sample, pos_s, state_conv[l], state_ssm[l], False, attend_s, *wts)
    st = lambda a: a[None]
    return (y_p, y_s, st(kp), st(vp), st(ip), st(cp), st(ssp), st(kss), st(vss), st(iss), st(css), st(sss))
```

```python
import functools
import math

import jax
import jax.numpy as jnp
import numpy as np
from jax import lax
from jax.experimental import pallas as pl
from jax.experimental.pallas import tpu as pltpu

D_MODEL = 4096
PAGE_SIZE = 128
D_HEAD = 128
H_GDN = 16
H_ATT = 16
D_GDN = H_GDN * D_HEAD
D_ATT = H_ATT * D_HEAD
CONV_W = 4
CONV_DIM = 3 * D_GDN
GDN_CHUNK = 64
H_IDX = 32
D_IDX = 128
TOPK_ATT_MAX = 256
Q_BLOCK = 128
ROPE_THETA = 10000.0
LN_EPS = 1e-5
RMS_EPS = 1e-6
PEER_HEADS = 8
PEER_NKEYS = 128
PEER_DQ = 256
PEER_TOPK = 16
PEER_BLOCK = 128
DEPTH = 1
DEEPNORM_ALPHA = (2.0 * DEPTH) ** 0.25
IN_SPLITS = (CONV_DIM, D_GDN, H_GDN, H_GDN, D_ATT, D_ATT, D_ATT, H_IDX * D_IDX, H_IDX, D_IDX, D_MODEL, D_MODEL)
IN_OFFSETS = tuple(int(o) for o in np.cumsum(IN_SPLITS)[:-1])

VMEM_LIMIT = 48 * 1024 * 1024


def _mm_kernel(x_ref, w_ref, o_ref):
    o_ref[...] = jnp.dot(x_ref[...], w_ref[...], preferred_element_type=jnp.float32)


def _pick(n, cands):
    for c in cands:
        if n % c == 0:
            return c
    return n


def _mm(x, w):
    m, k = x.shape
    n = w.shape[1]
    tm = _pick(m, (512, 256, 128))
    tn = _pick(n, (1024, 512, 256, 128))
    return pl.pallas_call(
        _mm_kernel,
        grid=(m // tm, n // tn),
        in_specs=[pl.BlockSpec((tm, k), lambda i, j: (i, 0)),
                  pl.BlockSpec((k, tn), lambda i, j: (0, j))],
        out_specs=pl.BlockSpec((tm, tn), lambda i, j: (i, j)),
        out_shape=jax.ShapeDtypeStruct((m, n), jnp.float32),
        compiler_params=pltpu.CompilerParams(
            dimension_semantics=("parallel", "parallel"),
            vmem_limit_bytes=VMEM_LIMIT),
        name="mm",
    )(x, w)


def _wcast(w):
    return jnp.pad(w.astype(jnp.bfloat16), ((0, 0), (0, (-w.shape[1]) % 128)))


def _in_proj_weights(w):
    names = ("qkv", "z", "b_raw", "a_raw", "q_att", "k_att", "v_att", "q_idx", "w_idx", "k_idx", "gate_gdn", "gate_att")
    bounds = (0,) + IN_OFFSETS + (sum(IN_SPLITS),)
    col = {n: w[:, bounds[i]:bounds[i + 1]].astype(jnp.bfloat16) for i, n in enumerate(names)}
    out = {n: col[n] for n in names if col[n].shape[1] % 128 == 0}
    out["small"] = _wcast(jnp.concatenate([col["b_raw"], col["a_raw"], col["w_idx"]], axis=1))
    return out


def _mmf(x, wb, n):
    shp = x.shape
    y = _mm(x.reshape(-1, shp[-1]).astype(jnp.bfloat16), wb)[:, :n]
    return y.reshape(*shp[:-1], n)


def _layernorm(x, g, b):
    mu = jnp.mean(x, -1, keepdims=True)
    var = jnp.mean(jnp.square(x - mu), -1, keepdims=True)
    return (x - mu) * lax.rsqrt(var + LN_EPS) * g + b


def _rope(x, pos):
    half = x.shape[-1] // 2
    inv = ROPE_THETA ** (-jnp.arange(half, dtype=jnp.float32) / half)
    ang = pos.astype(jnp.float32)[:, None] * inv[None, :]
    cos = jnp.cos(ang)[None, :, None, :]
    sin = jnp.sin(ang)[None, :, None, :]
    x1 = x[..., :half]
    x2 = x[..., half:]
    return jnp.concatenate([x1 * cos - x2 * sin, x2 * cos + x1 * sin], -1)


def _l2n(a):
    return a * lax.rsqrt(jnp.sum(a * a, -1, keepdims=True) + RMS_EPS)


def _causal_conv(u, buf, conv_w):
    t = u.shape[1]
    up = jnp.concatenate([buf.astype(u.dtype), u], axis=1)
    out = sum(up[:, j:j + t] * conv_w[j] for j in range(CONV_W))
    return jax.nn.silu(out), up[:, t:]


def _gdn_qkv(conv_out):
    b, t, _ = conv_out.shape
    q, k, v = (a.reshape(b, t, H_GDN, D_HEAD) for a in jnp.split(conv_out, 3, axis=-1))
    return _l2n(q) * D_HEAD ** -0.5, _l2n(k), v


def _gdn_gates(b_raw, a_raw, a_log, dt_bias):
    beta = jax.nn.sigmoid(b_raw)
    g = -jnp.exp(a_log) * jax.nn.softplus(a_raw + dt_bias)
    return g, beta


def _gdn_chunked(q, k, v, g, beta, s0):
    b, t, h, _ = q.shape
    c = GDN_CHUNK
    n = t // c

    def ch(a):
        return jnp.swapaxes(jnp.moveaxis(a.reshape(b, n, c, h, *a.shape[3:]), 1, 0), 2, 3)

    qc, kc, vc, gc, bc = ch(q), ch(k), ch(v), ch(g), ch(beta)
    gam = jnp.cumsum(gc, axis=-1)
    incl = jnp.tril(jnp.ones((c, c), bool))
    strict = jnp.tril(jnp.ones((c, c), bool), -1)
    decay = jnp.exp(jnp.where(incl, gam[..., :, None] - gam[..., None, :], -jnp.inf))
    kb = kc * bc[..., None]
    n_mat = jnp.where(strict, jnp.einsum('nbhid,nbhjd->nbhij', kb, kc) * decay, 0.0)
    t_inv = _unit_lower_inverse(n_mat.reshape(-1, c, c)).reshape(n_mat.shape)
    hi = lax.Precision.HIGHEST
    u = jnp.einsum('nbhij,nbhjd->nbhid', t_inv, vc * bc[..., None], precision=hi)
    w = jnp.einsum('nbhij,nbhjd->nbhid', t_inv, kb * jnp.exp(gam)[..., None], precision=hi)
    qk = jnp.einsum('nbhid,nbhjd->nbhij', qc, kc) * decay

    g_last = gam[..., -1:]
    qg = qc * jnp.exp(gam)[..., None]
    kg = kc * jnp.exp(g_last - gam)[..., None]
    decay_last = jnp.broadcast_to(jnp.exp(g_last)[..., None], (n, b, h, 1, D_HEAD))
    outs = [_gdn_scan(s0[i], *(a[:, i] for a in (u, w, qg, kg, qk, decay_last))) for i in range(b)]
    o = jnp.stack([x[0] for x in outs], axis=1)
    s = jnp.stack([x[1] for x in outs], axis=0)
    o = jnp.moveaxis(jnp.swapaxes(o, 2, 3), 0, 1).reshape(b, t, h, -1)
    return o, s


TRI_LANES = 128


def _tri_inv_kernel(n_ref, t_ref):
    c, _, lanes = n_ref.shape
    rowid = lax.broadcasted_iota(jnp.int32, (c, lanes), 0)
    t_ref[...] = jnp.zeros(t_ref.shape, t_ref.dtype)

    def row(i, carry):
        def term(j, acc):
            return acc - n_ref[i, pl.ds(j, 1), :] * t_ref[j]

        t_ref[i] = lax.fori_loop(0, c, term, jnp.where(rowid == i, 1.0, 0.0), unroll=8)
        return carry

    lax.fori_loop(0, c, row, 0)


def _unit_lower_inverse(nm):
    s, c, _ = nm.shape
    sp = -(-s // TRI_LANES) * TRI_LANES
    nt = jnp.pad(nm, ((0, sp - s), (0, 0), (0, 0))).transpose(1, 2, 0)
    spec = pl.BlockSpec((c, c, TRI_LANES), lambda g: (0, 0, g))
    t = pl.pallas_call(
        _tri_inv_kernel,
        grid=(sp // TRI_LANES,),
        in_specs=[spec],
        out_specs=spec,
        out_shape=jax.ShapeDtypeStruct((c, c, sp), jnp.float32),
        compiler_params=pltpu.CompilerParams(dimension_semantics=("parallel",), vmem_limit_bytes=VMEM_LIMIT),
        name="tri_inv",
    )(nt)
    return t.transpose(2, 0, 1)[:s]


def _gdn_scan_kernel(s0_ref, u_ref, w_ref, qg_ref, kg_ref, qk_ref, dl_ref, o_ref, s_ref):
    bf = jnp.bfloat16

    @pl.when(pl.program_id(0) == 0)
    def _():
        s_ref[...] = s0_ref[...]

    for hh in range(s_ref.shape[0]):
        s = s_ref[hh]
        sb = s.astype(bf)
        v_new = u_ref[0, hh] - jnp.dot(w_ref[0, hh].astype(bf), sb, preferred_element_type=jnp.float32)
        vb = v_new.astype(bf)
        o_ref[0, hh] = (jnp.dot(qg_ref[0, hh].astype(bf), sb, preferred_element_type=jnp.float32)
                        + jnp.dot(qk_ref[0, hh].astype(bf), vb, preferred_element_type=jnp.float32))
        s_ref[hh] = s * dl_ref[0, hh] + lax.dot_general(
            kg_ref[0, hh].astype(bf), vb, (((0,), (0,)), ((), ())), preferred_element_type=jnp.float32)


def _gdn_scan(s0, u, w, qg, kg, qk, decay_last):
    n, h, c, d = u.shape
    blk = lambda *shape: pl.BlockSpec((1,) + shape, lambda i: (i,) + (0,) * len(shape))
    state = pl.BlockSpec((h, d, d), lambda i: (0, 0, 0))
    o, s = pl.pallas_call(
        _gdn_scan_kernel,
        grid=(n,),
        in_specs=[state, blk(h, c, d), blk(h, c, d), blk(h, c, d), blk(h, c, d), blk(h, c, c), blk(h, 1, d)],
        out_specs=[blk(h, c, d), state],
        out_shape=[jax.ShapeDtypeStruct((n, h, c, d), jnp.float32), jax.ShapeDtypeStruct((h, d, d), jnp.float32)],
        compiler_params=pltpu.CompilerParams(dimension_semantics=("arbitrary",), vmem_limit_bytes=VMEM_LIMIT),
        name="gdn_scan",
    )(s0, u, w, qg, kg, qk, decay_last)
    return o, s


def _gdn_recurrent(q, k, v, g, beta, s0):
    def step(s, inp):
        q_t, k_t, v_t, g_t, b_t = inp
        s = s * jnp.exp(g_t)[..., None, None]
        ks = jnp.einsum('bhk,bhkv->bhv', k_t, s)
        s = s + jnp.einsum('bhk,bhv->bhkv', k_t * b_t[..., None], v_t - ks)
        return s, jnp.einsum('bhk,bhkv->bhv', q_t, s)

    tm = lambda a: jnp.moveaxis(a, 1, 0)
    s, o = lax.scan(step, s0, (tm(q), tm(k), tm(v), tm(g), tm(beta)))
    return jnp.moveaxis(o, 0, 1), s


def _gdn_out(o, z, norm_w):
    b, t = z.shape[:2]
    o = o * lax.rsqrt(jnp.mean(o * o, -1, keepdims=True) + RMS_EPS) * norm_w
    return o.reshape(b, t, D_GDN) * jax.nn.silu(z)


def _att_heads(q_att, k_att, v_att, q_idx, w_idx, k_idx, pos):
    b, t, _ = q_att.shape
    qa = _rope(q_att.reshape(b, t, H_ATT, D_HEAD), pos)
    ka = _rope(k_att.reshape(b, t, H_ATT, D_HEAD), pos)
    va = v_att.reshape(b, t, H_ATT, D_HEAD)
    qi = _rope(q_idx.reshape(b, t, H_IDX, D_IDX), pos)
    ki = _rope(k_idx[:, :, None, :], pos)[:, :, 0]
    wi = w_idx * (H_IDX ** -0.5 * D_IDX ** -0.5)
    return qa, ka, va, qi, wi, ki


def _indexer_topk(qi, wi, ki, q_pos, n_sel):
    rel = jax.nn.relu(jnp.einsum('bthd,bsd->bths', qi, ki))
    score = jnp.einsum('bths,bth->bts', rel, wi).astype(jnp.float32)
    key_pos = jnp.arange(ki.shape[1])
    score = jnp.where(key_pos[None, None, :] <= q_pos[None, :, None], score, -jnp.inf)
    _, sel = lax.top_k(score, n_sel)
    return sel


def _sparse_attend(q, k_sel, v_sel, sel, q_pos):
    logits = jnp.einsum('bthd,btkhd->bthk', q, k_sel).astype(jnp.float32) * D_HEAD ** -0.5
    valid = (sel <= q_pos[None, :, None])[:, :, None, :]
    p = jax.nn.softmax(jnp.where(valid, logits, -jnp.inf), axis=-1)
    return jnp.einsum('bthk,btkhd->bthd', p.astype(v_sel.dtype), v_sel)


MXU_COLS = 256
DSA_TQ = 128
DSA_TK = 256
DSA_HG = max(1, MXU_COLS // DSA_TQ)
DSA_SUB = 2
DSA_TS = DSA_SUB * DSA_TK
INT32_MIN = np.int32(-2 ** 31)


def _float_to_key(x):
    bits = lax.bitcast_convert_type(x + 0.0, jnp.int32)
    return bits ^ ((bits >> 31) & jnp.int32(0x7FFFFFFF))


def _kth_largest_key(xs_ref, nch, ch, n_sel):
    lanes = xs_ref.shape[1]

    def count_ge(cand):
        def body(c, acc):
            r0 = pl.multiple_of(c * ch, ch)
            ind = jnp.where(xs_ref[pl.ds(r0, ch), :] >= cand, 1, 0).astype(jnp.int32)
            return acc + jnp.sum(ind.reshape(ch // 8, 8, lanes), axis=0)

        acc = lax.fori_loop(0, nch, body, jnp.zeros((8, lanes), jnp.int32))
        return jnp.sum(acc, axis=0, keepdims=True)

    def bit_cond(st):
        b, _, open_ = st
        return jnp.logical_and(b < 32, jnp.max(open_) > 0)

    def bit_body(st):
        b, thr_u, open_ = st
        cand_u = thr_u | lax.shift_left(jnp.int32(1), 31 - b)
        cnt = count_ge(cand_u ^ INT32_MIN)
        take = jnp.logical_and(cnt >= n_sel, open_ > 0)
        return b + 1, jnp.where(take, cand_u, thr_u), jnp.where(cnt == n_sel, 0, open_)

    _, thr_u, _ = lax.while_loop(
        bit_cond, bit_body,
        (jnp.int32(0), jnp.zeros((1, lanes), jnp.int32), jnp.ones((1, lanes), jnp.int32)))
    return thr_u ^ INT32_MIN


def _dsa_prompt_kernel(ii_ref, jj_ref, qi_ref, w_ref, ki_ref, q_ref, k_ref, vt_ref, o_ref,
                       xs_ref, thr_ref, bias_ref, m_ref, l_ref, acc_ref, *, n_sel):
    s_id = pl.program_id(0)
    i = ii_ref[s_id]
    j = jj_ref[s_id]
    last = (i * DSA_TQ + DSA_TQ - 1) // DSA_TS
    nch = (last + 1) * DSA_SUB
    kiota = lax.broadcasted_iota(jnp.int32, (DSA_TK, DSA_TQ), 0)
    qpos = i * DSA_TQ + lax.broadcasted_iota(jnp.int32, (DSA_TK, DSA_TQ), 1)

    @pl.when(j == 0)
    def _index():
        def score_body(c, carry):
            r0 = pl.multiple_of(c * DSA_TK, DSA_TK)
            kchunk = ki_ref[pl.ds(r0, DSA_TK), :]
            sc = jnp.zeros((DSA_TK, DSA_TQ), jnp.float32)
            for hg in range(H_IDX // DSA_HG):
                z = jnp.dot(kchunk, qi_ref[0, hg], preferred_element_type=jnp.float32)
                for sub in range(DSA_HG):
                    hd = hg * DSA_HG + sub
                    sc = sc + w_ref[0, hd:hd + 1, :] * jnp.maximum(z[:, sub * DSA_TQ:(sub + 1) * DSA_TQ], 0.0)
            xs_ref[pl.ds(r0, DSA_TK), :] = jnp.where(r0 + kiota <= qpos, _float_to_key(sc), INT32_MIN)
            return carry

        lax.fori_loop(0, nch, score_body, 0)

        thr_ref[...] = _kth_largest_key(xs_ref, nch, DSA_TK, n_sel)
        m_ref[...] = jnp.full(m_ref.shape, -1e30, jnp.float32)
        l_ref[...] = jnp.zeros(l_ref.shape, jnp.float32)
        acc_ref[...] = jnp.zeros(acc_ref.shape, jnp.float32)

    for sub in range(DSA_SUB):
        r0 = pl.multiple_of(j * DSA_TS + sub * DSA_TK, DSA_TK)
        sel = jnp.where(xs_ref[pl.ds(r0, DSA_TK), :] >= thr_ref[...], r0 + kiota, jnp.int32(2 ** 30)) <= qpos
        bias_ref[sub * DSA_TK:(sub + 1) * DSA_TK, :] = jnp.where(sel, 0.0, -jnp.inf)
    for h in range(H_ATT):
        for sub in range(DSA_SUB):
            rows = slice(sub * DSA_TK, (sub + 1) * DSA_TK)
            s = jnp.dot(k_ref[rows, h * D_HEAD:(h + 1) * D_HEAD], q_ref[h], preferred_element_type=jnp.float32)
            s = s + bias_ref[rows, :]
            m_old = m_ref[h:h + 1, :]
            m_new = jnp.maximum(m_old, jnp.max(s, axis=0, keepdims=True))
            alpha = jnp.exp(m_old - m_new)
            p = jnp.exp(s - m_new)
            l_ref[h:h + 1, :] = alpha * l_ref[h:h + 1, :] + jnp.sum(p, axis=0, keepdims=True)
            m_ref[h:h + 1, :] = m_new
            acc_ref[h] = alpha * acc_ref[h] + jnp.dot(vt_ref[h, :, rows], p.astype(jnp.bfloat16),
                                                      preferred_element_type=jnp.float32)

    @pl.when(j == last)
    def _finish():
        for h in range(H_ATT):
            o_ref[:, h * D_HEAD:(h + 1) * D_HEAD] = (acc_ref[h] / l_ref[h:h + 1, :]).T


def _dsa_prompt_one(q, k, v, qi, wi, ki):
    t = q.shape[0]
    assert t % DSA_TS == 0 and DSA_TK % DSA_TQ == 0
    nq = t // DSA_TQ
    n_sel = min(TOPK_ATT_MAX, t // 4)
    bf = jnp.bfloat16
    qit = qi.reshape(nq, DSA_TQ, H_IDX // DSA_HG, DSA_HG, D_IDX).transpose(0, 2, 4, 3, 1).reshape(
        nq, H_IDX // DSA_HG, D_IDX, DSA_HG * DSA_TQ).astype(bf)
    wr = wi.reshape(nq, DSA_TQ, H_IDX).transpose(0, 2, 1)
    qt = (q * D_HEAD ** -0.5).transpose(1, 2, 0).astype(bf)
    kf = k.reshape(t, D_ATT).astype(bf)
    vt = v.transpose(1, 2, 0).astype(bf)
    steps = [(i, j) for i in range(nq) for j in range((i * DSA_TQ + DSA_TQ - 1) // DSA_TS + 1)]
    ii = jnp.asarray([s[0] for s in steps], jnp.int32)
    jj = jnp.asarray([s[1] for s in steps], jnp.int32)
    grid_spec = pltpu.PrefetchScalarGridSpec(
        num_scalar_prefetch=2,
        grid=(len(steps),),
        in_specs=[
            pl.BlockSpec((1, H_IDX // DSA_HG, D_IDX, DSA_HG * DSA_TQ), lambda s, ii, jj: (ii[s], 0, 0, 0)),
            pl.BlockSpec((1, H_IDX, DSA_TQ), lambda s, ii, jj: (ii[s], 0, 0)),
            pl.BlockSpec((t, D_IDX), lambda s, ii, jj: (0, 0)),
            pl.BlockSpec((H_ATT, D_HEAD, DSA_TQ), lambda s, ii, jj: (0, 0, ii[s])),
            pl.BlockSpec((DSA_TS, D_ATT), lambda s, ii, jj: (jj[s], 0)),
            pl.BlockSpec((H_ATT, D_HEAD, DSA_TS), lambda s, ii, jj: (0, 0, jj[s])),
        ],
        out_specs=pl.BlockSpec((DSA_TQ, D_ATT), lambda s, ii, jj: (ii[s], 0)),
        scratch_shapes=[
            pltpu.VMEM((t, DSA_TQ), jnp.int32),
            pltpu.VMEM((1, DSA_TQ), jnp.int32),
            pltpu.VMEM((DSA_TS, DSA_TQ), jnp.float32),
            pltpu.VMEM((H_ATT, DSA_TQ), jnp.float32),
            pltpu.VMEM((H_ATT, DSA_TQ), jnp.float32),
            pltpu.VMEM((H_ATT, D_HEAD, DSA_TQ), jnp.float32),
        ])
    return pl.pallas_call(
        functools.partial(_dsa_prompt_kernel, n_sel=n_sel),
        grid_spec=grid_spec,
        out_shape=jax.ShapeDtypeStruct((t, D_ATT), jnp.float32),
        compiler_params=pltpu.CompilerParams(dimension_semantics=("arbitrary",), vmem_limit_bytes=VMEM_LIMIT),
        name="dsa_prompt",
    )(ii, jj, qit, wr, ki.astype(bf), qt, kf, vt)


def _dsa_prompt(q, k, v, qi, wi, ki):
    return jnp.stack([_dsa_prompt_one(q[b], k[b], v[b], qi[b], wi[b], ki[b]) for b in range(q.shape[0])])


SMP_IDX_PAGES = 8
SMP_ATT_PAGES = 8
SMP_CH = 128


def _smp_score_kernel(pt_ref, *refs):
    pages = refs[:SMP_IDX_PAGES]
    qi_ref, wb_ref, o_ref = refs[SMP_IDX_PAGES:]
    nq = o_ref.shape[1]
    for g, ki_ref in enumerate(pages):
        z = lax.dot_general(qi_ref[0], ki_ref[0, 0].astype(jnp.bfloat16), (((1,), (1,)), ((), ())),
                            preferred_element_type=jnp.float32)
        r = jnp.maximum(z, 0.0) * wb_ref[0]
        o_ref[0, :, g * PAGE_SIZE:(g + 1) * PAGE_SIZE] = jnp.sum(r.reshape(nq, H_IDX, PAGE_SIZE), axis=1)


def _smp_thr_kernel(s_ref, thr_ref, xs_ref, *, n_sel):
    nch = s_ref.shape[0] // SMP_CH

    def conv(c, carry):
        r0 = pl.multiple_of(c * SMP_CH, SMP_CH)
        xs_ref[pl.ds(r0, SMP_CH), :] = _float_to_key(s_ref[pl.ds(r0, SMP_CH), :])
        return carry

    lax.fori_loop(0, nch, conv, 0)
    key = _kth_largest_key(xs_ref, nch, SMP_CH, n_sel)
    thr_ref[...] = lax.bitcast_convert_type(key ^ ((key >> 31) & jnp.int32(0x7FFFFFFF)), jnp.float32)


def _smp_attn_kernel(pt_ref, *refs, nq):
    kp = refs[:SMP_ATT_PAGES]
    vp = refs[SMP_ATT_PAGES:2 * SMP_ATT_PAGES]
    (sc_ref, scn_ref, thr_ref, q_ref, kn_ref, vn_ref, ex_ref, hm_ref,
     o_ref, m_ref, l_ref, acc_ref) = refs[2 * SMP_ATT_PAGES:]
    pp = pl.program_id(1)
    rows = PAGE_SIZE * H_ATT

    def attend(k, v, sc):
        sel = jnp.where(sc >= thr_ref[0], 1.0, 0.0)
        sel = jnp.concatenate(
            [jnp.broadcast_to(sel[tt:tt + 1, :], (H_ATT, PAGE_SIZE)) for tt in range(nq)], axis=0)
        valid = jnp.dot(sel.astype(jnp.bfloat16), ex_ref[...], preferred_element_type=jnp.float32) * hm_ref[...]
        s = lax.dot_general(q_ref[0], k.reshape(rows, D_HEAD).astype(jnp.bfloat16), (((1,), (1,)), ((), ())),
                            preferred_element_type=jnp.float32)
        s = jnp.where(valid > 0.5, s, -jnp.inf)
        m_old = m_ref[...]
        m_new = jnp.maximum(m_old, jnp.max(s, axis=1, keepdims=True))
        alpha = jnp.exp(m_old - m_new)
        p = jnp.exp(s - m_new)
        l_ref[...] = alpha * l_ref[...] + jnp.sum(p, axis=1, keepdims=True)
        m_ref[...] = m_new
        acc_ref[...] = alpha * acc_ref[...] + jnp.dot(
            p.astype(jnp.bfloat16), v.reshape(rows, D_HEAD).astype(jnp.bfloat16), preferred_element_type=jnp.float32)

    @pl.when(pp == 0)
    def _():
        m_ref[...] = jnp.full(m_ref.shape, -1e30, jnp.float32)
        l_ref[...] = jnp.zeros(l_ref.shape, jnp.float32)
        acc_ref[...] = jnp.zeros(acc_ref.shape, jnp.float32)
        attend(kn_ref[0], vn_ref[0], scn_ref[0])

    for g in range(SMP_ATT_PAGES):
        attend(kp[g][0, 0], vp[g][0, 0], sc_ref[0, :, g * PAGE_SIZE:(g + 1) * PAGE_SIZE])

    @pl.when(pp == pl.num_programs(1) - 1)
    def _():
        o_ref[0] = acc_ref[...] / l_ref[...]


def _dsa_sample(q, k_new, v_new, qi, wi, ki_new, cache_k, cache_v, cache_idx_k, page_table, layer):
    db, t = q.shape[:2]
    npg = page_table.shape[1]
    past_len = npg * PAGE_SIZE
    n_sel = min(TOPK_ATT_MAX, (past_len + t) // 4)
    assert npg % SMP_IDX_PAGES == 0 and npg % SMP_ATT_PAGES == 0 and (t * H_IDX) % 8 == 0
    bf = jnp.bfloat16
    pt = page_table.reshape(-1).astype(jnp.int32)
    n_pool = cache_k.shape[1]

    def page_spec(width, per_step, g):
        return pl.BlockSpec((1, 1, PAGE_SIZE, width),
                            lambda b, p, pt: (layer, pt[b * npg + p * per_step + g], 0, 0))

    row_spec = lambda rows, width: pl.BlockSpec((1, rows, width), lambda b, p, pt: (b, 0, 0))
    score_past = pl.pallas_call(
        _smp_score_kernel,
        grid_spec=pltpu.PrefetchScalarGridSpec(
            num_scalar_prefetch=1, grid=(db, npg // SMP_IDX_PAGES),
            in_specs=[page_spec(D_IDX, SMP_IDX_PAGES, g) for g in range(SMP_IDX_PAGES)]
            + [row_spec(t * H_IDX, D_IDX), row_spec(t * H_IDX, PAGE_SIZE)],
            out_specs=pl.BlockSpec((1, t, SMP_IDX_PAGES * PAGE_SIZE), lambda b, p, pt: (b, 0, p))),
        out_shape=jax.ShapeDtypeStruct((db, t, past_len), jnp.float32),
        compiler_params=pltpu.CompilerParams(dimension_semantics=("parallel", "arbitrary"),
                                             vmem_limit_bytes=VMEM_LIMIT),
        name="smp_score",
    )(pt, *([cache_idx_k] * SMP_IDX_PAGES), qi.reshape(db, t * H_IDX, D_IDX).astype(bf),
      jnp.broadcast_to(wi.reshape(db, t * H_IDX, 1), (db, t * H_IDX, PAGE_SIZE)))

    rel = jax.nn.relu(jnp.einsum('bthd,bsd->bths', qi, ki_new))
    score_new = jnp.einsum('bths,bth->bts', rel, wi)
    score_new = jnp.where(jnp.arange(t)[None, None, :] <= jnp.arange(t)[None, :, None], score_new, -jnp.inf)
    score_new = jnp.pad(score_new, ((0, 0), (0, 0), (0, PAGE_SIZE - t)), constant_values=-jnp.inf)

    score_t = jnp.concatenate([score_past, score_new], axis=2).reshape(db * t, past_len + PAGE_SIZE).T
    thr = pl.pallas_call(
        functools.partial(_smp_thr_kernel, n_sel=n_sel),
        out_shape=jax.ShapeDtypeStruct((1, db * t), jnp.float32),
        scratch_shapes=[pltpu.VMEM(score_t.shape, jnp.int32)],
        compiler_params=pltpu.CompilerParams(vmem_limit_bytes=VMEM_LIMIT),
        name="smp_thr",
    )(score_t)
    thr_b = jnp.broadcast_to(thr.reshape(db, t, 1), (db, t, PAGE_SIZE))

    nrow = t * H_ATT
    cols = PAGE_SIZE * H_ATT
    q_rows = (q * D_HEAD ** -0.5).reshape(db, nrow, D_HEAD).astype(bf)
    expand = jnp.repeat(jnp.eye(PAGE_SIZE, dtype=bf), H_ATT, axis=1)
    head_match = (jnp.arange(cols)[None, :] % H_ATT == jnp.arange(nrow)[:, None] % H_ATT).astype(jnp.float32)
    pad_page = lambda a: jnp.pad(a, ((0, 0), (0, PAGE_SIZE - t), (0, 0), (0, 0)))
    cache_spec = lambda g: pl.BlockSpec(
        (1, 1, PAGE_SIZE, H_ATT, D_HEAD),
        lambda b, p, pt: (layer, pt[b * npg + p * SMP_ATT_PAGES + g], 0, 0, 0))
    const_spec = lambda shape: pl.BlockSpec(shape, lambda b, p, pt: (0,) * len(shape))
    new_spec = pl.BlockSpec((1, PAGE_SIZE, H_ATT, D_HEAD), lambda b, p, pt: (b, 0, 0, 0))
    out = pl.pallas_call(
        functools.partial(_smp_attn_kernel, nq=t),
        grid_spec=pltpu.PrefetchScalarGridSpec(
            num_scalar_prefetch=1, grid=(db, npg // SMP_ATT_PAGES),
            in_specs=[cache_spec(g) for g in range(SMP_ATT_PAGES)] * 2
            + [pl.BlockSpec((1, t, SMP_ATT_PAGES * PAGE_SIZE), lambda b, p, pt: (b, 0, p)),
               row_spec(t, PAGE_SIZE), row_spec(t, PAGE_SIZE), row_spec(nrow, D_HEAD),
               new_spec, new_spec, const_spec((PAGE_SIZE, cols)), const_spec((nrow, cols))],
            out_specs=row_spec(nrow, D_HEAD),
            scratch_shapes=[pltpu.VMEM((nrow, 1), jnp.float32), pltpu.VMEM((nrow, 1), jnp.float32),
                            pltpu.VMEM((nrow, D_HEAD), jnp.float32)]),
        out_shape=jax.ShapeDtypeStruct((db, nrow, D_HEAD), jnp.float32),
        compiler_params=pltpu.CompilerParams(dimension_semantics=("parallel", "arbitrary"),
                                             vmem_limit_bytes=VMEM_LIMIT),
        name="smp_attn",
    )(pt, *([cache_k] * SMP_ATT_PAGES), *([cache_v] * SMP_ATT_PAGES),
      score_past, score_new, thr_b, q_rows, pad_page(k_new), pad_page(v_new), expand, head_match)
    return out.reshape(db, t, D_ATT)


def _peer(x, wq, k1, k2, u, v):
    shape = x.shape
    xf = x.reshape(-1, D_MODEL)
    n = xf.shape[0]
    pad = (-n) % PEER_BLOCK
    xf = jnp.pad(xf, ((0, pad), (0, 0)))
    xb16 = xf.astype(jnp.bfloat16)
    e1, e2, gate = _peer_route(_mm(xb16, wq), k1, k2)
    wmat = _peer_gate_matrix(e1, e2, gate)
    out = _peer_dense(xb16, u, v, wmat)
    return out[:n].reshape(shape)


PEER_RT = 128
PEER_RH = 2
PEER_TOPK_LOG2 = PEER_TOPK.bit_length() - 1
assert 1 << PEER_TOPK_LOG2 == PEER_TOPK
PEER_PAIRS = tuple((i, j) for i in range(PEER_TOPK) for j in range(PEER_TOPK) if (i + 1) * (j + 1) <= PEER_TOPK)


def _extract_top(s, ids, n):
    big = jnp.int32(2 ** 30)
    vals, sel = [], []
    for _ in range(n):
        m = jnp.max(s, axis=0, keepdims=True)
        pick = jnp.min(jnp.where(s == m, ids, big), axis=0, keepdims=True)
        vals.append(m)
        sel.append(pick)
        s = jnp.where(ids == pick, -jnp.inf, s)
    return vals, sel


def _peer_route_kernel(q_ref, k1_ref, k2_ref, e1_ref, e2_ref, g_ref):
    kio = lax.broadcasted_iota(jnp.int32, (PEER_NKEYS, PEER_RT), 0)
    flat = jnp.concatenate(
        [jnp.full((1, PEER_RT), i * PEER_TOPK + j, jnp.int32) for i, j in PEER_PAIRS], axis=0)
    for hh in range(PEER_RH):
        tops = []
        for side, kref in ((0, k1_ref), (1, k2_ref)):
            lo = (2 * hh + side) * (PEER_DQ // 2)
            qs = q_ref[:, lo:lo + PEER_DQ // 2].astype(jnp.bfloat16)
            s = lax.dot_general(kref[hh], qs, (((1,), (1,)), ((), ())),
                                preferred_element_type=jnp.float32)
            tops.append(_extract_top(s, kio, PEER_TOPK))
        (v1, i1), (v2, i2) = tops
        cand = jnp.concatenate([v1[i] + v2[j] for i, j in PEER_PAIRS], axis=0)
        top_s, top_f = _extract_top(cand, flat, PEER_TOPK)
        ex = [jnp.exp(t - top_s[0]) for t in top_s]
        inv = 1.0 / sum(ex)
        for r in range(PEER_TOPK):
            fi = top_f[r] >> PEER_TOPK_LOG2
            fj = top_f[r] & (PEER_TOPK - 1)
            e1 = sum(jnp.where(fi == i, i1[i], 0) for i in range(PEER_TOPK))
            e2 = sum(jnp.where(fj == j, i2[j], 0) for j in range(PEER_TOPK))
            e1_ref[hh, r:r + 1, :] = e1
            e2_ref[hh, r:r + 1, :] = e2
            g_ref[hh, r:r + 1, :] = ex[r] * inv


def _peer_route(q, k1, k2):
    n = q.shape[0]
    assert n % PEER_RT == 0 and PEER_HEADS % PEER_RH == 0
    kspec = pl.BlockSpec((PEER_RH, PEER_NKEYS, PEER_DQ // 2), lambda i, h: (h, 0, 0))
    ospec = pl.BlockSpec((PEER_RH, PEER_TOPK, PEER_RT), lambda i, h: (h, 0, i))
    oshape = lambda dt: jax.ShapeDtypeStruct((PEER_HEADS, PEER_TOPK, n), dt)
    e1, e2, g = pl.pallas_call(
        _peer_route_kernel,
        grid=(n // PEER_RT, PEER_HEADS // PEER_RH),
        in_specs=[pl.BlockSpec((PEER_RT, PEER_RH * PEER_DQ), lambda i, h: (i, h)), kspec, kspec],
        out_specs=[ospec, ospec, ospec],
        out_shape=[oshape(jnp.int32), oshape(jnp.int32), oshape(jnp.float32)],
        compiler_params=pltpu.CompilerParams(dimension_semantics=("parallel", "parallel"),
                                             vmem_limit_bytes=VMEM_LIMIT),
        name="peer_route",
    )(q, k1.astype(jnp.bfloat16), k2.astype(jnp.bfloat16))
    tok_major = lambda a: a.reshape(PEER_HEADS * PEER_TOPK, n).T
    return tok_major(e1), tok_major(e2), tok_major(g)


PEER_GT = 16


def _peer_gate_kernel(i1_ref, i2_ref, g_ref, w_ref, tmp_ref):
    iota = lax.broadcasted_iota(jnp.int32, (PEER_NKEYS, PEER_HEADS * PEER_TOPK), 0)

    def body(gi, carry):
        t0 = pl.multiple_of(gi * PEER_GT, PEER_GT)
        for kk in range(PEER_GT):
            a = jnp.where(iota == i1_ref[pl.ds(t0 + kk, 1), :], g_ref[pl.ds(t0 + kk, 1), :], 0.0).astype(jnp.bfloat16)
            b = jnp.where(iota == i2_ref[pl.ds(t0 + kk, 1), :], 1.0, 0.0).astype(jnp.bfloat16)
            tmp_ref[kk * PEER_NKEYS:(kk + 1) * PEER_NKEYS, :] = lax.dot_general(
                a, b, (((1,), (1,)), ((), ())), preferred_element_type=jnp.float32)
        for aa in range(PEER_NKEYS):
            rows = tmp_ref[pl.ds(aa, PEER_GT, stride=PEER_NKEYS), :]
            w_ref[pl.ds(t0, PEER_GT), aa * PEER_NKEYS:(aa + 1) * PEER_NKEYS] = rows.astype(w_ref.dtype)
        return carry

    lax.fori_loop(0, i1_ref.shape[0] // PEER_GT, body, 0)


def _peer_gate_matrix(e1, e2, gate):
    n, p = e1.shape
    tw = _pick(n, (128,))
    assert tw % PEER_GT == 0
    spec = pl.BlockSpec((tw, p), lambda i: (i, 0))
    return pl.pallas_call(
        _peer_gate_kernel,
        grid=(n // tw,),
        in_specs=[spec, spec, spec],
        out_specs=pl.BlockSpec((tw, PEER_NKEYS * PEER_NKEYS), lambda i: (i, 0)),
        out_shape=jax.ShapeDtypeStruct((n, PEER_NKEYS * PEER_NKEYS), jnp.bfloat16),
        scratch_shapes=[pltpu.VMEM((PEER_GT * PEER_NKEYS, PEER_NKEYS), jnp.float32)],
        compiler_params=pltpu.CompilerParams(dimension_semantics=("parallel",), vmem_limit_bytes=VMEM_LIMIT),
        name="peer_gate",
    )(e1, e2, gate)


def _peer_dense_kernel(h_ref, ut_ref, w_ref, v_ref, o_ref):
    @pl.when(pl.program_id(1) == 0)
    def _():
        o_ref[...] = jnp.zeros_like(o_ref)

    half = ut_ref.shape[1] // 2
    upd = None
    for s in range(2):
        z = jnp.dot(h_ref[...], ut_ref[:, s * half:(s + 1) * half], preferred_element_type=jnp.float32)
        act = 0.5 * z * (1.0 + lax.erf(z * (2.0 ** -0.5)))
        c = (w_ref[:, s * half:(s + 1) * half].astype(jnp.float32) * act).astype(jnp.bfloat16)
        d = jnp.dot(c, v_ref[s * half:(s + 1) * half, :], preferred_element_type=jnp.float32)
        upd = d if upd is None else upd + d
    o_ref[...] += upd


def _peer_dense(xb16, ut, vb, wmat):
    n = xb16.shape[0]
    e = vb.shape[0]
    tm = _pick(n, (512, 256, 128))
    te = 512
    return pl.pallas_call(
        _peer_dense_kernel,
        grid=(n // tm, e // te),
        in_specs=[pl.BlockSpec((tm, D_MODEL), lambda i, j: (i, 0)),
                  pl.BlockSpec((D_MODEL, te), lambda i, j: (0, j)),
                  pl.BlockSpec((tm, te), lambda i, j: (i, j)),
                  pl.BlockSpec((te, D_MODEL), lambda i, j: (j, 0))],
        out_specs=pl.BlockSpec((tm, D_MODEL), lambda i, j: (i, 0)),
        out_shape=jax.ShapeDtypeStruct((n, D_MODEL), jnp.float32),
        compiler_params=pltpu.CompilerParams(dimension_semantics=("parallel", "arbitrary"),
                                             vmem_limit_bytes=56 * 1024 * 1024),
        name="peer_dense",
    )(xb16, ut, wmat, vb)


def _group(x, pos, conv_buf, ssm0, chunked, attend, w_in, conv_w, a_log, dt_bias, gdn_norm_w,
           w_br_gdn, w_br_att, w_out, ln1_g, ln1_b, peer_wq, peer_k1, peer_k2, peer_u, peer_v, ln2_g, ln2_b):
    xb = x.reshape(-1, D_MODEL).astype(jnp.bfloat16)
    proj = lambda name: _mm(xb, w_in[name]).reshape(*x.shape[:-1], -1)
    qkv, z, q_att, k_att, v_att, q_idx, k_idx, gate_gdn, gate_att = (
        proj(n) for n in ("qkv", "z", "q_att", "k_att", "v_att", "q_idx", "k_idx", "gate_gdn", "gate_att"))
    small = proj("small")
    b_raw, a_raw, w_idx = small[..., :H_GDN], small[..., H_GDN:2 * H_GDN], small[..., 2 * H_GDN:2 * H_GDN + H_IDX]
    conv_out, conv_new = _causal_conv(qkv, conv_buf, conv_w)
    q, k, v = _gdn_qkv(conv_out)
    g, beta = _gdn_gates(b_raw, a_raw, a_log, dt_bias)
    gdn = _gdn_chunked if chunked else _gdn_recurrent
    o, ssm_new = gdn(q, k, v, g, beta, ssm0.astype(jnp.float32))
    o_gdn = _gdn_out(o, z, gdn_norm_w)
    qa, ka, va, qi, wi, ki = _att_heads(q_att, k_att, v_att, q_idx, w_idx, k_idx, pos)
    o_att = attend(qa, ka, va, qi, wi, ki)
    mix = _mmf(jax.nn.sigmoid(gate_gdn) * _mmf(o_gdn, w_br_gdn, D_MODEL) + jax.nn.sigmoid(gate_att) * _mmf(o_att, w_br_att, D_MODEL),
               w_out, D_MODEL)
    h = _layernorm(DEEPNORM_ALPHA * x + mix, ln1_g, ln1_b)
    y = _layernorm(DEEPNORM_ALPHA * h + _peer(h, peer_wq, peer_k1, peer_k2, peer_u, peer_v), ln2_g, ln2_b)
    return y, ka, va, ki, conv_new, ssm_new.astype(x.dtype)


def kernel(x_prompt, x_sample, cache_k, cache_v, cache_idx_k, state_conv, state_ssm, page_table, w_in, conv_w, a_log, dt_bias, gdn_norm_w, w_br_gdn, w_br_att, w_out, ln1_g, ln1_b, peer_wq, peer_k1, peer_k2, peer_u, peer_v, ln2_g, ln2_b):
    bp, sp = x_prompt.shape[:2]
    ds = x_sample.shape[1]
    past_len = page_table.shape[1] * PAGE_SIZE
    pos_p = jnp.arange(sp)
    pos_s = past_len + jnp.arange(ds)
    l = 0
    wts = (_in_proj_weights(w_in[l]), conv_w[l], a_log[l], dt_bias[l], gdn_norm_w[l], _wcast(w_br_gdn[l]), _wcast(w_br_att[l]),
           _wcast(w_out[l]), ln1_g[l], ln1_b[l], _wcast(peer_wq[l]), peer_k1[l], peer_k2[l],
           peer_u[l].astype(jnp.bfloat16).T, peer_v[l].astype(jnp.bfloat16), ln2_g[l], ln2_b[l])
    y_p, kp, vp, ip, cp, ssp = _group(
        x_prompt, pos_p, jnp.zeros((bp, CONV_W - 1, CONV_DIM), x_prompt.dtype),
        jnp.zeros((bp, H_GDN, D_HEAD, D_HEAD), jnp.float32), True, _dsa_prompt, *wts)
    attend_s = functools.partial(_dsa_sample, cache_k=cache_k, cache_v=cache_v, cache_idx_k=cache_idx_k,
                                 page_table=page_table, layer=l)
    y_s, kss, vss, iss, css, sss = _group(
        x_sample, pos_s, state_conv[l], state_ssm[l], False, attend_s, *wts)
    st = lambda a: a[None]
    return (y_p, y_s, st(kp), st(vp), st(ip), st(cp), st(ssp), st(kss), st(vss), st(iss), st(css), st(sss))
```

```python
import functools
import math

import jax
import jax.numpy as jnp
import numpy as np
from jax import lax
from jax.experimental import pallas as pl
from jax.experimental.pallas import tpu as pltpu

D_MODEL = 4096
PAGE_SIZE = 128
D_HEAD = 128
H_GDN = 16
H_ATT = 16
D_GDN = H_GDN * D_HEAD
D_ATT = H_ATT * D_HEAD
CONV_W = 4
CONV_DIM = 3 * D_GDN
GDN_CHUNK = 64
H_IDX = 32
D_IDX = 128
TOPK_ATT_MAX = 256
Q_BLOCK = 128
ROPE_THETA = 10000.0
LN_EPS = 1e-5
RMS_EPS = 1e-6
PEER_HEADS = 8
PEER_NKEYS = 128
PEER_DQ = 256
PEER_TOPK = 16
PEER_BLOCK = 128
DEPTH = 1
DEEPNORM_ALPHA = (2.0 * DEPTH) ** 0.25
IN_SPLITS = (CONV_DIM, D_GDN, H_GDN, H_GDN, D_ATT, D_ATT, D_ATT, H_IDX * D_IDX, H_IDX, D_IDX, D_MODEL, D_MODEL)
IN_OFFSETS = tuple(int(o) for o in np.cumsum(IN_SPLITS)[:-1])

VMEM_LIMIT = 48 * 1024 * 1024


def _mm_kernel(x_ref, w_ref, o_ref):
    o_ref[...] = jnp.dot(x_ref[...], w_ref[...], preferred_element_type=jnp.float32)


def _pick(n, cands):
    for c in cands:
        if n % c == 0:
            return c
    return n


def _mm(x, w):
    m, k = x.shape
    n = w.shape[1]
    tm = _pick(m, (1024, 512, 256, 128))
    tn = _pick(n, (1024, 512, 256, 128))
    return pl.pallas_call(
        _mm_kernel,
        grid=(m // tm, n // tn),
        in_specs=[pl.BlockSpec((tm, k), lambda i, j: (i, 0)),
                  pl.BlockSpec((k, tn), lambda i, j: (0, j))],
        out_specs=pl.BlockSpec((tm, tn), lambda i, j: (i, j)),
        out_shape=jax.ShapeDtypeStruct((m, n), jnp.float32),
        compiler_params=pltpu.CompilerParams(
            dimension_semantics=("parallel", "parallel"),
            vmem_limit_bytes=VMEM_LIMIT),
        name="mm",
    )(x, w)


def _wcast(w):
    return jnp.pad(w.astype(jnp.bfloat16), ((0, 0), (0, (-w.shape[1]) % 128)))


def _in_proj_weights(w):
    names = ("qkv", "z", "b_raw", "a_raw", "q_att", "k_att", "v_att", "q_idx", "w_idx", "k_idx", "gate_gdn", "gate_att")
    bounds = (0,) + IN_OFFSETS + (sum(IN_SPLITS),)
    col = {n: w[:, bounds[i]:bounds[i + 1]].astype(jnp.bfloat16) for i, n in enumerate(names)}
    out = {n: col[n] for n in names if col[n].shape[1] % 128 == 0}
    out["small"] = _wcast(jnp.concatenate([col["b_raw"], col["a_raw"], col["w_idx"]], axis=1))
    return out


def _mmf(x, wb, n):
    shp = x.shape
    y = _mm(x.reshape(-1, shp[-1]).astype(jnp.bfloat16), wb)[:, :n]
    return y.reshape(*shp[:-1], n)


def _layernorm(x, g, b):
    mu = jnp.mean(x, -1, keepdims=True)
    var = jnp.mean(jnp.square(x - mu), -1, keepdims=True)
    return (x - mu) * lax.rsqrt(var + LN_EPS) * g + b


def _rope(x, pos):
    half = x.shape[-1] // 2
    inv = ROPE_THETA ** (-jnp.arange(half, dtype=jnp.float32) / half)
    ang = pos.astype(jnp.float32)[:, None] * inv[None, :]
    cos = jnp.cos(ang)[None, :, None, :]
    sin = jnp.sin(ang)[None, :, None, :]
    x1 = x[..., :half]
    x2 = x[..., half:]
    return jnp.concatenate([x1 * cos - x2 * sin, x2 * cos + x1 * sin], -1)


def _l2n(a):
    return a * lax.rsqrt(jnp.sum(a * a, -1, keepdims=True) + RMS_EPS)


def _causal_conv(u, buf, conv_w):
    t = u.shape[1]
    up = jnp.concatenate([buf.astype(u.dtype), u], axis=1)
    out = sum(up[:, j:j + t] * conv_w[j] for j in range(CONV_W))
    return jax.nn.silu(out), up[:, t:]


def _gdn_qkv(conv_out):
    b, t, _ = conv_out.shape
    q, k, v = (a.reshape(b, t, H_GDN, D_HEAD) for a in jnp.split(conv_out, 3, axis=-1))
    return _l2n(q) * D_HEAD ** -0.5, _l2n(k), v


def _gdn_gates(b_raw, a_raw, a_log, dt_bias):
    beta = jax.nn.sigmoid(b_raw)
    g = -jnp.exp(a_log) * jax.nn.softplus(a_raw + dt_bias)
    return g, beta


def _gdn_chunked(q, k, v, g, beta, s0):
    b, t, h, _ = q.shape
    c = GDN_CHUNK
    n = t // c

    def ch(a):
        return jnp.swapaxes(jnp.moveaxis(a.reshape(b, n, c, h, *a.shape[3:]), 1, 0), 2, 3)

    qc, kc, vc, gc, bc = ch(q), ch(k), ch(v), ch(g), ch(beta)
    gam = jnp.cumsum(gc, axis=-1)
    incl = jnp.tril(jnp.ones((c, c), bool))
    strict = jnp.tril(jnp.ones((c, c), bool), -1)
    decay = jnp.exp(jnp.where(incl, gam[..., :, None] - gam[..., None, :], -jnp.inf))
    kb = kc * bc[..., None]
    n_mat = jnp.where(strict, jnp.einsum('nbhid,nbhjd->nbhij', kb, kc) * decay, 0.0)
    t_inv = _unit_lower_inverse(n_mat.reshape(-1, c, c)).reshape(n_mat.shape)
    hi = lax.Precision.HIGH
    u = jnp.einsum('nbhij,nbhjd->nbhid', t_inv, vc * bc[..., None], precision=hi)
    w = jnp.einsum('nbhij,nbhjd->nbhid', t_inv, kb * jnp.exp(gam)[..., None], precision=hi)
    qk = jnp.einsum('nbhid,nbhjd->nbhij', qc, kc) * decay

    g_last = gam[..., -1:]
    qg = qc * jnp.exp(gam)[..., None]
    kg = kc * jnp.exp(g_last - gam)[..., None]
    decay_last = jnp.broadcast_to(jnp.exp(g_last)[..., None], (n, b, h, 1, D_HEAD))
    outs = [_gdn_scan(s0[i], *(a[:, i] for a in (u, w, qg, kg, qk, decay_last))) for i in range(b)]
    o = jnp.stack([x[0] for x in outs], axis=1)
    s = jnp.stack([x[1] for x in outs], axis=0)
    o = jnp.moveaxis(jnp.swapaxes(o, 2, 3), 0, 1).reshape(b, t, h, -1)
    return o, s


TRI_LANES = 128


def _tri_inv_kernel(n_ref, t_ref):
    c, _, lanes = n_ref.shape
    rowid = lax.broadcasted_iota(jnp.int32, (c, lanes), 0)
    t_ref[...] = jnp.zeros(t_ref.shape, t_ref.dtype)

    def row(i, carry):
        def term(j, acc):
            return acc - n_ref[i, pl.ds(j, 1), :] * t_ref[j]

        t_ref[i] = lax.fori_loop(0, c, term, jnp.where(rowid == i, 1.0, 0.0), unroll=8)
        return carry

    lax.fori_loop(0, c, row, 0)


def _unit_lower_inverse(nm):
    s, c, _ = nm.shape
    sp = -(-s // TRI_LANES) * TRI_LANES
    nt = jnp.pad(nm, ((0, sp - s), (0, 0), (0, 0))).transpose(1, 2, 0)
    spec = pl.BlockSpec((c, c, TRI_LANES), lambda g: (0, 0, g))
    t = pl.pallas_call(
        _tri_inv_kernel,
        grid=(sp // TRI_LANES,),
        in_specs=[spec],
        out_specs=spec,
        out_shape=jax.ShapeDtypeStruct((c, c, sp), jnp.float32),
        compiler_params=pltpu.CompilerParams(dimension_semantics=("parallel",), vmem_limit_bytes=VMEM_LIMIT),
        name="tri_inv",
    )(nt)
    return t.transpose(2, 0, 1)[:s]


def _gdn_scan_kernel(s0_ref, u_ref, w_ref, qg_ref, kg_ref, qk_ref, dl_ref, o_ref, s_ref):
    bf = jnp.bfloat16

    @pl.when(pl.program_id(0) == 0)
    def _():
        s_ref[...] = s0_ref[...]

    for hh in range(s_ref.shape[0]):
        s = s_ref[hh]
        sb = s.astype(bf)
        v_new = u_ref[0, hh] - jnp.dot(w_ref[0, hh].astype(bf), sb, preferred_element_type=jnp.float32)
        vb = v_new.astype(bf)
        o_ref[0, hh] = (jnp.dot(qg_ref[0, hh].astype(bf), sb, preferred_element_type=jnp.float32)
                        + jnp.dot(qk_ref[0, hh].astype(bf), vb, preferred_element_type=jnp.float32))
        s_ref[hh] = s * dl_ref[0, hh] + lax.dot_general(
            kg_ref[0, hh].astype(bf), vb, (((0,), (0,)), ((), ())), preferred_element_type=jnp.float32)


def _gdn_scan(s0, u, w, qg, kg, qk, decay_last):
    n, h, c, d = u.shape
    blk = lambda *shape: pl.BlockSpec((1,) + shape, lambda i: (i,) + (0,) * len(shape))
    state = pl.BlockSpec((h, d, d), lambda i: (0, 0, 0))
    o, s = pl.pallas_call(
        _gdn_scan_kernel,
        grid=(n,),
        in_specs=[state, blk(h, c, d), blk(h, c, d), blk(h, c, d), blk(h, c, d), blk(h, c, c), blk(h, 1, d)],
        out_specs=[blk(h, c, d), state],
        out_shape=[jax.ShapeDtypeStruct((n, h, c, d), jnp.float32), jax.ShapeDtypeStruct((h, d, d), jnp.float32)],
        compiler_params=pltpu.CompilerParams(dimension_semantics=("arbitrary",), vmem_limit_bytes=VMEM_LIMIT),
        name="gdn_scan",
    )(s0, u, w, qg, kg, qk, decay_last)
    return o, s


def _gdn_step_kernel(s0_ref, q_ref, k_ref, kb_ref, v_ref, dec_ref, o_ref, s_ref):
    nt = v_ref.shape[2]
    for hh in range(s0_ref.shape[1]):
        s = s0_ref[0, hh]
        for tt in range(nt):
            s = s * dec_ref[0, hh, tt:tt + 1, :]
            ks = jnp.sum(s * k_ref[0, hh, :, tt:tt + 1], axis=0, keepdims=True)
            s = s + kb_ref[0, hh, :, tt:tt + 1] * (v_ref[0, hh, tt:tt + 1, :] - ks)
            o_ref[0, hh, tt:tt + 1, :] = jnp.sum(s * q_ref[0, hh, :, tt:tt + 1], axis=0, keepdims=True)
        s_ref[0, hh] = s


def _gdn_recurrent(q, k, v, g, beta, s0):
    b, t, h, d = q.shape
    cols = lambda a: a.transpose(0, 2, 3, 1)
    rows = lambda a: a.transpose(0, 2, 1, 3)
    dec = jnp.broadcast_to(jnp.exp(g).transpose(0, 2, 1)[..., None], (b, h, t, d))
    cspec = pl.BlockSpec((1, h, d, t), lambda i: (i, 0, 0, 0))
    rspec = pl.BlockSpec((1, h, t, d), lambda i: (i, 0, 0, 0))
    sspec = pl.BlockSpec((1, h, d, d), lambda i: (i, 0, 0, 0))
    o, s = pl.pallas_call(
        _gdn_step_kernel,
        grid=(b,),
        in_specs=[sspec, cspec, cspec, cspec, rspec, rspec],
        out_specs=[rspec, sspec],
        out_shape=[jax.ShapeDtypeStruct((b, h, t, d), jnp.float32), jax.ShapeDtypeStruct((b, h, d, d), jnp.float32)],
        compiler_params=pltpu.CompilerParams(dimension_semantics=("parallel",), vmem_limit_bytes=VMEM_LIMIT),
        name="gdn_step",
    )(s0, cols(q), cols(k), cols(k * beta[..., None]), rows(v), dec)
    return o.transpose(0, 2, 1, 3), s


def _gdn_out(o, z, norm_w):
    b, t = z.shape[:2]
    o = o * lax.rsqrt(jnp.mean(o * o, -1, keepdims=True) + RMS_EPS) * norm_w
    return o.reshape(b, t, D_GDN) * jax.nn.silu(z)


def _att_heads(q_att, k_att, v_att, q_idx, w_idx, k_idx, pos):
    b, t, _ = q_att.shape
    qa = _rope(q_att.reshape(b, t, H_ATT, D_HEAD), pos)
    ka = _rope(k_att.reshape(b, t, H_ATT, D_HEAD), pos)
    va = v_att.reshape(b, t, H_ATT, D_HEAD)
    qi = _rope(q_idx.reshape(b, t, H_IDX, D_IDX), pos)
    ki = _rope(k_idx[:, :, None, :], pos)[:, :, 0]
    wi = w_idx * (H_IDX ** -0.5 * D_IDX ** -0.5)
    return qa, ka, va, qi, wi, ki


def _indexer_topk(qi, wi, ki, q_pos, n_sel):
    rel = jax.nn.relu(jnp.einsum('bthd,bsd->bths', qi, ki))
    score = jnp.einsum('bths,bth->bts', rel, wi).astype(jnp.float32)
    key_pos = jnp.arange(ki.shape[1])
    score = jnp.where(key_pos[None, None, :] <= q_pos[None, :, None], score, -jnp.inf)
    _, sel = lax.top_k(score, n_sel)
    return sel


def _sparse_attend(q, k_sel, v_sel, sel, q_pos):
    logits = jnp.einsum('bthd,btkhd->bthk', q, k_sel).astype(jnp.float32) * D_HEAD ** -0.5
    valid = (sel <= q_pos[None, :, None])[:, :, None, :]
    p = jax.nn.softmax(jnp.where(valid, logits, -jnp.inf), axis=-1)
    return jnp.einsum('bthk,btkhd->bthd', p.astype(v_sel.dtype), v_sel)


MXU_COLS = 256
DSA_TQ = 128
DSA_TK = 256
DSA_HG = max(1, MXU_COLS // DSA_TQ)
DSA_SUB = 2
DSA_TS = DSA_SUB * DSA_TK
INT32_MIN = np.int32(-2 ** 31)


def _float_to_key(x):
    bits = lax.bitcast_convert_type(x + 0.0, jnp.int32)
    return bits ^ ((bits >> 31) & jnp.int32(0x7FFFFFFF))


def _kth_largest_key(xs_ref, nch, ch, n_sel):
    lanes = xs_ref.shape[1]

    def count_ge(cand):
        def body(c, acc):
            r0 = pl.multiple_of(c * ch, ch)
            ind = jnp.where(xs_ref[pl.ds(r0, ch), :] >= cand, 1, 0).astype(jnp.int32)
            return acc + jnp.sum(ind.reshape(ch // 8, 8, lanes), axis=0)

        acc = lax.fori_loop(0, nch, body, jnp.zeros((8, lanes), jnp.int32))
        return jnp.sum(acc, axis=0, keepdims=True)

    def bit_cond(st):
        b, _, open_ = st
        return jnp.logical_and(b < 32, jnp.max(open_) > 0)

    def bit_body(st):
        b, thr_u, open_ = st
        cand_u = thr_u | lax.shift_left(jnp.int32(1), 31 - b)
        cnt = count_ge(cand_u ^ INT32_MIN)
        take = jnp.logical_and(cnt >= n_sel, open_ > 0)
        return b + 1, jnp.where(take, cand_u, thr_u), jnp.where(cnt == n_sel, 0, open_)

    _, thr_u, _ = lax.while_loop(
        bit_cond, bit_body,
        (jnp.int32(0), jnp.zeros((1, lanes), jnp.int32), jnp.ones((1, lanes), jnp.int32)))
    return thr_u ^ INT32_MIN


def _dsa_prompt_kernel(ii_ref, jj_ref, qi_ref, w_ref, ki_ref, q_ref, k_ref, vt_ref, o_ref,
                       xs_ref, thr_ref, bias_ref, m_ref, l_ref, acc_ref, *, n_sel):
    s_id = pl.program_id(0)
    i = ii_ref[s_id]
    j = jj_ref[s_id]
    last = (i * DSA_TQ + DSA_TQ - 1) // DSA_TS
    nch = (last + 1) * DSA_SUB
    kiota = lax.broadcasted_iota(jnp.int32, (DSA_TK, DSA_TQ), 0)
    qpos = i * DSA_TQ + lax.broadcasted_iota(jnp.int32, (DSA_TK, DSA_TQ), 1)

    @pl.when(j == 0)
    def _index():
        def score_body(c, carry):
            r0 = pl.multiple_of(c * DSA_TK, DSA_TK)
            kchunk = ki_ref[pl.ds(r0, DSA_TK), :]
            sc = jnp.zeros((DSA_TK, DSA_TQ), jnp.float32)
            for hg in range(H_IDX // DSA_HG):
                z = jnp.dot(kchunk, qi_ref[0, hg], preferred_element_type=jnp.float32)
                for sub in range(DSA_HG):
                    hd = hg * DSA_HG + sub
                    sc = sc + w_ref[0, hd:hd + 1, :] * jnp.maximum(z[:, sub * DSA_TQ:(sub + 1) * DSA_TQ], 0.0)
            xs_ref[pl.ds(r0, DSA_TK), :] = jnp.where(r0 + kiota <= qpos, _float_to_key(sc), INT32_MIN)
            return carry

        lax.fori_loop(0, nch, score_body, 0)

        thr_ref[...] = _kth_largest_key(xs_ref, nch, DSA_TK, n_sel)
        m_ref[...] = jnp.full(m_ref.shape, -1e30, jnp.float32)
        l_ref[...] = jnp.zeros(l_ref.shape, jnp.float32)
        acc_ref[...] = jnp.zeros(acc_ref.shape, jnp.float32)

    for sub in range(DSA_SUB):
        r0 = pl.multiple_of(j * DSA_TS + sub * DSA_TK, DSA_TK)
        sel = jnp.where(xs_ref[pl.ds(r0, DSA_TK), :] >= thr_ref[...], r0 + kiota, jnp.int32(2 ** 30)) <= qpos
        bias_ref[sub * DSA_TK:(sub + 1) * DSA_TK, :] = jnp.where(sel, 0.0, -jnp.inf)
    for h in range(H_ATT):
        for sub in range(DSA_SUB):
            rows = slice(sub * DSA_TK, (sub + 1) * DSA_TK)
            s = jnp.dot(k_ref[rows, h * D_HEAD:(h + 1) * D_HEAD], q_ref[h], preferred_element_type=jnp.float32)
            s = s + bias_ref[rows, :]
            m_old = m_ref[h:h + 1, :]
            m_new = jnp.maximum(m_old, jnp.max(s, axis=0, keepdims=True))
            alpha = jnp.exp(m_old - m_new)
            p = jnp.exp(s - m_new)
            l_ref[h:h + 1, :] = alpha * l_ref[h:h + 1, :] + jnp.sum(p, axis=0, keepdims=True)
            m_ref[h:h + 1, :] = m_new
            acc_ref[h] = alpha * acc_ref[h] + jnp.dot(vt_ref[h, :, rows], p.astype(jnp.bfloat16),
                                                      preferred_element_type=jnp.float32)

    @pl.when(j == last)
    def _finish():
        for h in range(H_ATT):
            o_ref[:, h * D_HEAD:(h + 1) * D_HEAD] = (acc_ref[h] / l_ref[h:h + 1, :]).T


def _dsa_prompt_one(q, k, v, qi, wi, ki):
    t = q.shape[0]
    assert t % DSA_TS == 0 and DSA_TK % DSA_TQ == 0
    nq = t // DSA_TQ
    n_sel = min(TOPK_ATT_MAX, t // 4)
    bf = jnp.bfloat16
    qit = qi.reshape(nq, DSA_TQ, H_IDX // DSA_HG, DSA_HG, D_IDX).transpose(0, 2, 4, 3, 1).reshape(
        nq, H_IDX // DSA_HG, D_IDX, DSA_HG * DSA_TQ).astype(bf)
    wr = wi.reshape(nq, DSA_TQ, H_IDX).transpose(0, 2, 1)
    qt = (q * D_HEAD ** -0.5).transpose(1, 2, 0).astype(bf)
    kf = k.reshape(t, D_ATT).astype(bf)
    vt = v.transpose(1, 2, 0).astype(bf)
    steps = [(i, j) for i in range(nq) for j in range((i * DSA_TQ + DSA_TQ - 1) // DSA_TS + 1)]
    ii = jnp.asarray([s[0] for s in steps], jnp.int32)
    jj = jnp.asarray([s[1] for s in steps], jnp.int32)
    grid_spec = pltpu.PrefetchScalarGridSpec(
        num_scalar_prefetch=2,
        grid=(len(steps),),
        in_specs=[
            pl.BlockSpec((1, H_IDX // DSA_HG, D_IDX, DSA_HG * DSA_TQ), lambda s, ii, jj: (ii[s], 0, 0, 0)),
            pl.BlockSpec((1, H_IDX, DSA_TQ), lambda s, ii, jj: (ii[s], 0, 0)),
            pl.BlockSpec((t, D_IDX), lambda s, ii, jj: (0, 0)),
            pl.BlockSpec((H_ATT, D_HEAD, DSA_TQ), lambda s, ii, jj: (0, 0, ii[s])),
            pl.BlockSpec((DSA_TS, D_ATT), lambda s, ii, jj: (jj[s], 0)),
            pl.BlockSpec((H_ATT, D_HEAD, DSA_TS), lambda s, ii, jj: (0, 0, jj[s])),
        ],
        out_specs=pl.BlockSpec((DSA_TQ, D_ATT), lambda s, ii, jj: (ii[s], 0)),
        scratch_shapes=[
            pltpu.VMEM((t, DSA_TQ), jnp.int32),
            pltpu.VMEM((1, DSA_TQ), jnp.int32),
            pltpu.VMEM((DSA_TS, DSA_TQ), jnp.float32),
            pltpu.VMEM((H_ATT, DSA_TQ), jnp.float32),
            pltpu.VMEM((H_ATT, DSA_TQ), jnp.float32),
            pltpu.VMEM((H_ATT, D_HEAD, DSA_TQ), jnp.float32),
        ])
    return pl.pallas_call(
        functools.partial(_dsa_prompt_kernel, n_sel=n_sel),
        grid_spec=grid_spec,
        out_shape=jax.ShapeDtypeStruct((t, D_ATT), jnp.float32),
        compiler_params=pltpu.CompilerParams(dimension_semantics=("arbitrary",), vmem_limit_bytes=VMEM_LIMIT),
        name="dsa_prompt",
    )(ii, jj, qit, wr, ki.astype(bf), qt, kf, vt)


def _dsa_prompt(q, k, v, qi, wi, ki):
    return jnp.stack([_dsa_prompt_one(q[b], k[b], v[b], qi[b], wi[b], ki[b]) for b in range(q.shape[0])])


SMP_IDX_PAGES = 8
SMP_ATT_PAGES = 8
SMP_CH = 128


def _smp_score_kernel(pt_ref, *refs):
    pages = refs[:SMP_IDX_PAGES]
    qi_ref, wb_ref, o_ref = refs[SMP_IDX_PAGES:]
    nq = o_ref.shape[1]
    for g, ki_ref in enumerate(pages):
        z = lax.dot_general(qi_ref[0], ki_ref[0, 0].astype(jnp.bfloat16), (((1,), (1,)), ((), ())),
                            preferred_element_type=jnp.float32)
        r = jnp.maximum(z, 0.0) * wb_ref[0]
        o_ref[0, :, g * PAGE_SIZE:(g + 1) * PAGE_SIZE] = jnp.sum(r.reshape(nq, H_IDX, PAGE_SIZE), axis=1)


def _smp_thr_kernel(s_ref, thr_ref, xs_ref, *, n_sel):
    nch = s_ref.shape[0] // SMP_CH

    def conv(c, carry):
        r0 = pl.multiple_of(c * SMP_CH, SMP_CH)
        xs_ref[pl.ds(r0, SMP_CH), :] = _float_to_key(s_ref[pl.ds(r0, SMP_CH), :])
        return carry

    lax.fori_loop(0, nch, conv, 0)
    key = _kth_largest_key(xs_ref, nch, SMP_CH, n_sel)
    thr_ref[...] = lax.bitcast_convert_type(key ^ ((key >> 31) & jnp.int32(0x7FFFFFFF)), jnp.float32)


def _smp_attn_kernel(pt_ref, *refs, nq):
    kp = refs[:SMP_ATT_PAGES]
    vp = refs[SMP_ATT_PAGES:2 * SMP_ATT_PAGES]
    (sc_ref, scn_ref, thr_ref, q_ref, kn_ref, vn_ref, ex_ref, hm_ref,
     o_ref, m_ref, l_ref, acc_ref) = refs[2 * SMP_ATT_PAGES:]
    pp = pl.program_id(1)
    rows = PAGE_SIZE * H_ATT

    def attend(k, v, sc):
        sel = jnp.where(sc >= thr_ref[0], 1.0, 0.0)
        sel = jnp.concatenate(
            [jnp.broadcast_to(sel[tt:tt + 1, :], (H_ATT, PAGE_SIZE)) for tt in range(nq)], axis=0)
        valid = jnp.dot(sel.astype(jnp.bfloat16), ex_ref[...], preferred_element_type=jnp.float32) * hm_ref[...]
        s = lax.dot_general(q_ref[0], k.reshape(rows, D_HEAD).astype(jnp.bfloat16), (((1,), (1,)), ((), ())),
                            preferred_element_type=jnp.float32)
        s = jnp.where(valid > 0.5, s, -jnp.inf)
        m_old = m_ref[...]
        m_new = jnp.maximum(m_old, jnp.max(s, axis=1, keepdims=True))
        alpha = jnp.exp(m_old - m_new)
        p = jnp.exp(s - m_new)
        l_ref[...] = alpha * l_ref[...] + jnp.sum(p, axis=1, keepdims=True)
        m_ref[...] = m_new
        acc_ref[...] = alpha * acc_ref[...] + jnp.dot(
            p.astype(jnp.bfloat16), v.reshape(rows, D_HEAD).astype(jnp.bfloat16), preferred_element_type=jnp.float32)

    @pl.when(pp == 0)
    def _():
        m_ref[...] = jnp.full(m_ref.shape, -1e30, jnp.float32)
        l_ref[...] = jnp.zeros(l_ref.shape, jnp.float32)
        acc_ref[...] = jnp.zeros(acc_ref.shape, jnp.float32)
        attend(kn_ref[0], vn_ref[0], scn_ref[0])

    for g in range(SMP_ATT_PAGES):
        attend(kp[g][0, 0], vp[g][0, 0], sc_ref[0, :, g * PAGE_SIZE:(g + 1) * PAGE_SIZE])

    @pl.when(pp == pl.num_programs(1) - 1)
    def _():
        o_ref[0] = acc_ref[...] / l_ref[...]


def _dsa_sample(q, k_new, v_new, qi, wi, ki_new, cache_k, cache_v, cache_idx_k, page_table, layer):
    db, t = q.shape[:2]
    npg = page_table.shape[1]
    past_len = npg * PAGE_SIZE
    n_sel = min(TOPK_ATT_MAX, (past_len + t) // 4)
    assert npg % SMP_IDX_PAGES == 0 and npg % SMP_ATT_PAGES == 0 and (t * H_IDX) % 8 == 0
    bf = jnp.bfloat16
    pt = page_table.reshape(-1).astype(jnp.int32)
    n_pool = cache_k.shape[1]

    def page_spec(width, per_step, g):
        return pl.BlockSpec((1, 1, PAGE_SIZE, width),
                            lambda b, p, pt: (layer, pt[b * npg + p * per_step + g], 0, 0))

    row_spec = lambda rows, width: pl.BlockSpec((1, rows, width), lambda b, p, pt: (b, 0, 0))
    score_past = pl.pallas_call(
        _smp_score_kernel,
        grid_spec=pltpu.PrefetchScalarGridSpec(
            num_scalar_prefetch=1, grid=(db, npg // SMP_IDX_PAGES),
            in_specs=[page_spec(D_IDX, SMP_IDX_PAGES, g) for g in range(SMP_IDX_PAGES)]
            + [row_spec(t * H_IDX, D_IDX), row_spec(t * H_IDX, PAGE_SIZE)],
            out_specs=pl.BlockSpec((1, t, SMP_IDX_PAGES * PAGE_SIZE), lambda b, p, pt: (b, 0, p))),
        out_shape=jax.ShapeDtypeStruct((db, t, past_len), jnp.float32),
        compiler_params=pltpu.CompilerParams(dimension_semantics=("parallel", "arbitrary"),
                                             vmem_limit_bytes=VMEM_LIMIT),
        name="smp_score",
    )(pt, *([cache_idx_k] * SMP_IDX_PAGES), qi.reshape(db, t * H_IDX, D_IDX).astype(bf),
      jnp.broadcast_to(wi.reshape(db, t * H_IDX, 1), (db, t * H_IDX, PAGE_SIZE)))

    rel = jax.nn.relu(jnp.einsum('bthd,bsd->bths', qi, ki_new))
    score_new = jnp.einsum('bths,bth->bts', rel, wi)
    score_new = jnp.where(jnp.arange(t)[None, None, :] <= jnp.arange(t)[None, :, None], score_new, -jnp.inf)
    score_new = jnp.pad(score_new, ((0, 0), (0, 0), (0, PAGE_SIZE - t)), constant_values=-jnp.inf)

    score_t = jnp.concatenate([score_past, score_new], axis=2).reshape(db * t, past_len + PAGE_SIZE).T
    thr = pl.pallas_call(
        functools.partial(_smp_thr_kernel, n_sel=n_sel),
        out_shape=jax.ShapeDtypeStruct((1, db * t), jnp.float32),
        scratch_shapes=[pltpu.VMEM(score_t.shape, jnp.int32)],
        compiler_params=pltpu.CompilerParams(vmem_limit_bytes=VMEM_LIMIT),
        name="smp_thr",
    )(score_t)
    thr_b = jnp.broadcast_to(thr.reshape(db, t, 1), (db, t, PAGE_SIZE))

    nrow = t * H_ATT
    cols = PAGE_SIZE * H_ATT
    q_rows = (q * D_HEAD ** -0.5).reshape(db, nrow, D_HEAD).astype(bf)
    expand = jnp.repeat(jnp.eye(PAGE_SIZE, dtype=bf), H_ATT, axis=1)
    head_match = (jnp.arange(cols)[None, :] % H_ATT == jnp.arange(nrow)[:, None] % H_ATT).astype(jnp.float32)
    pad_page = lambda a: jnp.pad(a, ((0, 0), (0, PAGE_SIZE - t), (0, 0), (0, 0)))
    cache_spec = lambda g: pl.BlockSpec(
        (1, 1, PAGE_SIZE, H_ATT, D_HEAD),
        lambda b, p, pt: (layer, pt[b * npg + p * SMP_ATT_PAGES + g], 0, 0, 0))
    const_spec = lambda shape: pl.BlockSpec(shape, lambda b, p, pt: (0,) * len(shape))
    new_spec = pl.BlockSpec((1, PAGE_SIZE, H_ATT, D_HEAD), lambda b, p, pt: (b, 0, 0, 0))
    out = pl.pallas_call(
        functools.partial(_smp_attn_kernel, nq=t),
        grid_spec=pltpu.PrefetchScalarGridSpec(
            num_scalar_prefetch=1, grid=(db, npg // SMP_ATT_PAGES),
            in_specs=[cache_spec(g) for g in range(SMP_ATT_PAGES)] * 2
            + [pl.BlockSpec((1, t, SMP_ATT_PAGES * PAGE_SIZE), lambda b, p, pt: (b, 0, p)),
               row_spec(t, PAGE_SIZE), row_spec(t, PAGE_SIZE), row_spec(nrow, D_HEAD),
               new_spec, new_spec, const_spec((PAGE_SIZE, cols)), const_spec((nrow, cols))],
            out_specs=row_spec(nrow, D_HEAD),
            scratch_shapes=[pltpu.VMEM((nrow, 1), jnp.float32), pltpu.VMEM((nrow, 1), jnp.float32),
                            pltpu.VMEM((nrow, D_HEAD), jnp.float32)]),
        out_shape=jax.ShapeDtypeStruct((db, nrow, D_HEAD), jnp.float32),
        compiler_params=pltpu.CompilerParams(dimension_semantics=("parallel", "arbitrary"),
                                             vmem_limit_bytes=VMEM_LIMIT),
        name="smp_attn",
    )(pt, *([cache_k] * SMP_ATT_PAGES), *([cache_v] * SMP_ATT_PAGES),
      score_past, score_new, thr_b, q_rows, pad_page(k_new), pad_page(v_new), expand, head_match)
    return out.reshape(db, t, D_ATT)


def _peer(x, wq, k1, k2, u, v):
    shape = x.shape
    xf = x.reshape(-1, D_MODEL)
    n = xf.shape[0]
    pad = (-n) % PEER_BLOCK
    xf = jnp.pad(xf, ((0, pad), (0, 0)))
    xb16 = xf.astype(jnp.bfloat16)
    e1, e2, gate = _peer_route(_mm(xb16, wq), k1, k2)
    wmat = _peer_gate_matrix(e1, e2, gate)
    out = _peer_dense(xb16, u, v, wmat)
    return out[:n].reshape(shape)


PEER_RT = 128
PEER_RH = 2
PEER_TOPK_LOG2 = PEER_TOPK.bit_length() - 1
assert 1 << PEER_TOPK_LOG2 == PEER_TOPK
PEER_PAIRS = tuple((i, j) for i in range(PEER_TOPK) for j in range(PEER_TOPK) if (i + 1) * (j + 1) <= PEER_TOPK)


def _extract_top(s, ids, n):
    big = jnp.int32(2 ** 30)
    vals, sel = [], []
    for _ in range(n):
        m = jnp.max(s, axis=0, keepdims=True)
        pick = jnp.min(jnp.where(s == m, ids, big), axis=0, keepdims=True)
        vals.append(m)
        sel.append(pick)
        s = jnp.where(ids == pick, -jnp.inf, s)
    return vals, sel


def _peer_route_kernel(q_ref, k1_ref, k2_ref, e1_ref, e2_ref, g_ref):
    kio = lax.broadcasted_iota(jnp.int32, (PEER_NKEYS, PEER_RT), 0)
    flat = jnp.concatenate(
        [jnp.full((1, PEER_RT), i * PEER_TOPK + j, jnp.int32) for i, j in PEER_PAIRS], axis=0)
    for hh in range(PEER_RH):
        tops = []
        for side, kref in ((0, k1_ref), (1, k2_ref)):
            lo = (2 * hh + side) * (PEER_DQ // 2)
            qs = q_ref[:, lo:lo + PEER_DQ // 2].astype(jnp.bfloat16)
            s = lax.dot_general(kref[hh], qs, (((1,), (1,)), ((), ())),
                                preferred_element_type=jnp.float32)
            tops.append(_extract_top(s, kio, PEER_TOPK))
        (v1, i1), (v2, i2) = tops
        cand = jnp.concatenate([v1[i] + v2[j] for i, j in PEER_PAIRS], axis=0)
        top_s, top_f = _extract_top(cand, flat, PEER_TOPK)
        ts = jnp.concatenate(top_s, axis=0)
        tf = jnp.concatenate(top_f, axis=0)
        ex = jnp.exp(ts - top_s[0])
        g_ref[hh] = ex * (1.0 / jnp.sum(ex, axis=0, keepdims=True))
        fi = tf >> PEER_TOPK_LOG2
        fj = tf & (PEER_TOPK - 1)
        e1_ref[hh] = sum(jnp.where(fi == i, i1[i], 0) for i in range(PEER_TOPK))
        e2_ref[hh] = sum(jnp.where(fj == j, i2[j], 0) for j in range(PEER_TOPK))


def _peer_route(q, k1, k2):
    n = q.shape[0]
    assert n % PEER_RT == 0 and PEER_HEADS % PEER_RH == 0
    kspec = pl.BlockSpec((PEER_RH, PEER_NKEYS, PEER_DQ // 2), lambda i, h: (h, 0, 0))
    ospec = pl.BlockSpec((PEER_RH, PEER_TOPK, PEER_RT), lambda i, h: (h, 0, i))
    oshape = lambda dt: jax.ShapeDtypeStruct((PEER_HEADS, PEER_TOPK, n), dt)
    e1, e2, g = pl.pallas_call(
        _peer_route_kernel,
        grid=(n // PEER_RT, PEER_HEADS // PEER_RH),
        in_specs=[pl.BlockSpec((PEER_RT, PEER_RH * PEER_DQ), lambda i, h: (i, h)), kspec, kspec],
        out_specs=[ospec, ospec, ospec],
        out_shape=[oshape(jnp.int32), oshape(jnp.int32), oshape(jnp.float32)],
        compiler_params=pltpu.CompilerParams(dimension_semantics=("parallel", "parallel"),
                                             vmem_limit_bytes=VMEM_LIMIT),
        name="peer_route",
    )(q, k1.astype(jnp.bfloat16), k2.astype(jnp.bfloat16))
    tok_major = lambda a: a.reshape(PEER_HEADS * PEER_TOPK, n).T
    return tok_major(e1), tok_major(e2), tok_major(g)


PEER_GT = 16


def _peer_gate_kernel(i1_ref, i2_ref, g_ref, w_ref, tmp_ref):
    iota = lax.broadcasted_iota(jnp.int32, (PEER_NKEYS, PEER_HEADS * PEER_TOPK), 0)

    def body(gi, carry):
        t0 = pl.multiple_of(gi * PEER_GT, PEER_GT)
        for kk in range(PEER_GT):
            a = jnp.where(iota == i1_ref[pl.ds(t0 + kk, 1), :], g_ref[pl.ds(t0 + kk, 1), :], 0.0).astype(jnp.bfloat16)
            b = jnp.where(iota == i2_ref[pl.ds(t0 + kk, 1), :], 1.0, 0.0).astype(jnp.bfloat16)
            tmp_ref[kk * PEER_NKEYS:(kk + 1) * PEER_NKEYS, :] = lax.dot_general(
                a, b, (((1,), (1,)), ((), ())), preferred_element_type=jnp.float32)
        for aa in range(PEER_NKEYS):
            rows = tmp_ref[pl.ds(aa, PEER_GT, stride=PEER_NKEYS), :]
            w_ref[pl.ds(t0, PEER_GT), aa * PEER_NKEYS:(aa + 1) * PEER_NKEYS] = rows.astype(w_ref.dtype)
        return carry

    lax.fori_loop(0, i1_ref.shape[0] // PEER_GT, body, 0)


def _peer_gate_matrix(e1, e2, gate):
    n, p = e1.shape
    tw = _pick(n, (128,))
    assert tw % PEER_GT == 0
    spec = pl.BlockSpec((tw, p), lambda i: (i, 0))
    return pl.pallas_call(
        _peer_gate_kernel,
        grid=(n // tw,),
        in_specs=[spec, spec, spec],
        out_specs=pl.BlockSpec((tw, PEER_NKEYS * PEER_NKEYS), lambda i: (i, 0)),
        out_shape=jax.ShapeDtypeStruct((n, PEER_NKEYS * PEER_NKEYS), jnp.bfloat16),
        scratch_shapes=[pltpu.VMEM((PEER_GT * PEER_NKEYS, PEER_NKEYS), jnp.float32)],
        compiler_params=pltpu.CompilerParams(dimension_semantics=("parallel",), vmem_limit_bytes=VMEM_LIMIT),
        name="peer_gate",
    )(e1, e2, gate)


def _peer_dense_kernel(h_ref, ut_ref, w_ref, v_ref, o_ref):
    @pl.when(pl.program_id(1) == 0)
    def _():
        o_ref[...] = jnp.zeros_like(o_ref)

    half = ut_ref.shape[1] // 2
    upd = None
    for s in range(2):
        z = jnp.dot(h_ref[...], ut_ref[:, s * half:(s + 1) * half], preferred_element_type=jnp.float32)
        act = 0.5 * z * (1.0 + lax.erf(z * (2.0 ** -0.5)))
        c = (w_ref[:, s * half:(s + 1) * half].astype(jnp.float32) * act).astype(jnp.bfloat16)
        d = jnp.dot(c, v_ref[s * half:(s + 1) * half, :], preferred_element_type=jnp.float32)
        upd = d if upd is None else upd + d
    o_ref[...] += upd


def _peer_dense(xb16, ut, vb, wmat):
    n = xb16.shape[0]
    e = vb.shape[0]
    tm = _pick(n, (512, 256, 128))
    te = 512
    return pl.pallas_call(
        _peer_dense_kernel,
        grid=(n // tm, e // te),
        in_specs=[pl.BlockSpec((tm, D_MODEL), lambda i, j: (i, 0)),
                  pl.BlockSpec((D_MODEL, te), lambda i, j: (0, j)),
                  pl.BlockSpec((tm, te), lambda i, j: (i, j)),
                  pl.BlockSpec((te, D_MODEL), lambda i, j: (j, 0))],
        out_specs=pl.BlockSpec((tm, D_MODEL), lambda i, j: (i, 0)),
        out_shape=jax.ShapeDtypeStruct((n, D_MODEL), jnp.float32),
        compiler_params=pltpu.CompilerParams(dimension_semantics=("parallel", "arbitrary"),
                                             vmem_limit_bytes=56 * 1024 * 1024),
        name="peer_dense",
    )(xb16, ut, wmat, vb)


def _group(x, pos, conv_buf, ssm0, chunked, attend, w_in, conv_w, a_log, dt_bias, gdn_norm_w,
           w_br_gdn, w_br_att, w_out, ln1_g, ln1_b, peer_wq, peer_k1, peer_k2, peer_u, peer_v, ln2_g, ln2_b):
    xb = x.reshape(-1, D_MODEL).astype(jnp.bfloat16)
    proj = lambda name: _mm(xb, w_in[name]).reshape(*x.shape[:-1], -1)
    qkv, z, q_att, k_att, v_att, q_idx, k_idx, gate_gdn, gate_att = (
        proj(n) for n in ("qkv", "z", "q_att", "k_att", "v_att", "q_idx", "k_idx", "gate_gdn", "gate_att"))
    small = proj("small")
    b_raw, a_raw, w_idx = small[..., :H_GDN], small[..., H_GDN:2 * H_GDN], small[..., 2 * H_GDN:2 * H_GDN + H_IDX]
    conv_out, conv_new = _causal_conv(qkv, conv_buf, conv_w)
    q, k, v = _gdn_qkv(conv_out)
    g, beta = _gdn_gates(b_raw, a_raw, a_log, dt_bias)
    gdn = _gdn_chunked if chunked else _gdn_recurrent
    o, ssm_new = gdn(q, k, v, g, beta, ssm0.astype(jnp.float32))
    o_gdn = _gdn_out(o, z, gdn_norm_w)
    qa, ka, va, qi, wi, ki = _att_heads(q_att, k_att, v_att, q_idx, w_idx, k_idx, pos)
    o_att = attend(qa, ka, va, qi, wi, ki)
    mix = _mmf(jax.nn.sigmoid(gate_gdn) * _mmf(o_gdn, w_br_gdn, D_MODEL) + jax.nn.sigmoid(gate_att) * _mmf(o_att, w_br_att, D_MODEL),
               w_out, D_MODEL)
    h = _layernorm(DEEPNORM_ALPHA * x + mix, ln1_g, ln1_b)
    y = _layernorm(DEEPNORM_ALPHA * h + _peer(h, peer_wq, peer_k1, peer_k2, peer_u, peer_v), ln2_g, ln2_b)
    return y, ka, va, ki, conv_new, ssm_new.astype(x.dtype)


def kernel(x_prompt, x_sample, cache_k, cache_v, cache_idx_k, state_conv, state_ssm, page_table, w_in, conv_w, a_log, dt_bias, gdn_norm_w, w_br_gdn, w_br_att, w_out, ln1_g, ln1_b, peer_wq, peer_k1, peer_k2, peer_u, peer_v, ln2_g, ln2_b):
    bp, sp = x_prompt.shape[:2]
    ds = x_sample.shape[1]
    past_len = page_table.shape[1] * PAGE_SIZE
    pos_p = jnp.arange(sp)
    pos_s = past_len + jnp.arange(ds)
    l = 0
    wts = (_in_proj_weights(w_in[l]), conv_w[l], a_log[l], dt_bias[l], gdn_norm_w[l], _wcast(w_br_gdn[l]), _wcast(w_br_att[l]),
           _wcast(w_out[l]), ln1_g[l], ln1_b[l], _wcast(peer_wq[l]), peer_k1[l], peer_k2[l],
           peer_u[l].astype(jnp.bfloat16).T, peer_v[l].astype(jnp.bfloat16), ln2_g[l], ln2_b[l])
    y_p, kp, vp, ip, cp, ssp = _group(
        x_prompt, pos_p, jnp.zeros((bp, CONV_W - 1, CONV_DIM), x_prompt.dtype),
        jnp.zeros((bp, H_GDN, D_HEAD, D_HEAD), jnp.float32), True, _dsa_prompt, *wts)
    attend_s = functools.partial(_dsa_sample, cache_k=cache_k, cache_v=cache_v, cache_idx_k=cache_idx_k,
                                 page_table=page_table, layer=l)
    y_s, kss, vss, iss, css, sss = _group(
        x_sample, pos_s, state_conv[l], state_ssm[l], False, attend_s, *wts)
    st = lambda a: a[None]
    return (y_p, y_s, st(kp), st(vp), st(ip), st(cp), st(ssp), st(kss), st(vss), st(iss), st(css), st(sss))
```

```python
import functools
import math

import jax
import jax.numpy as jnp
import numpy as np
from jax import lax
from jax.experimental import pallas as pl
from jax.experimental.pallas import tpu as pltpu

D_MODEL = 4096
PAGE_SIZE = 128
D_HEAD = 128
H_GDN = 16
H_ATT = 16
D_GDN = H_GDN * D_HEAD
D_ATT = H_ATT * D_HEAD
CONV_W = 4
CONV_DIM = 3 * D_GDN
GDN_CHUNK = 64
H_IDX = 32
D_IDX = 128
TOPK_ATT_MAX = 256
Q_BLOCK = 128
ROPE_THETA = 10000.0
LN_EPS = 1e-5
RMS_EPS = 1e-6
PEER_HEADS = 8
PEER_NKEYS = 128
PEER_DQ = 256
PEER_TOPK = 16
PEER_BLOCK = 128
DEPTH = 1
DEEPNORM_ALPHA = (2.0 * DEPTH) ** 0.25
IN_SPLITS = (CONV_DIM, D_GDN, H_GDN, H_GDN, D_ATT, D_ATT, D_ATT, H_IDX * D_IDX, H_IDX, D_IDX, D_MODEL, D_MODEL)
IN_OFFSETS = tuple(int(o) for o in np.cumsum(IN_SPLITS)[:-1])

VMEM_LIMIT = 48 * 1024 * 1024


def _mm_kernel(x_ref, w_ref, o_ref):
    o_ref[...] = jnp.dot(x_ref[...], w_ref[...], preferred_element_type=jnp.float32)


def _pick(n, cands):
    for c in cands:
        if n % c == 0:
            return c
    return n


def _mm(x, w):
    m, k = x.shape
    n = w.shape[1]
    tm = _pick(m, (1024, 512, 256, 128))
    tn = _pick(n, (1024, 512, 256, 128))
    return pl.pallas_call(
        _mm_kernel,
        grid=(m // tm, n // tn),
        in_specs=[pl.BlockSpec((tm, k), lambda i, j: (i, 0)),
                  pl.BlockSpec((k, tn), lambda i, j: (0, j))],
        out_specs=pl.BlockSpec((tm, tn), lambda i, j: (i, j)),
        out_shape=jax.ShapeDtypeStruct((m, n), jnp.float32),
        compiler_params=pltpu.CompilerParams(
            dimension_semantics=("parallel", "parallel"),
            vmem_limit_bytes=VMEM_LIMIT),
        name="mm",
    )(x, w)


def _wcast(w):
    return jnp.pad(w.astype(jnp.bfloat16), ((0, 0), (0, (-w.shape[1]) % 128)))


def _in_proj_weights(w):
    names = ("qkv", "z", "b_raw", "a_raw", "q_att", "k_att", "v_att", "q_idx", "w_idx", "k_idx", "gate_gdn", "gate_att")
    bounds = (0,) + IN_OFFSETS + (sum(IN_SPLITS),)
    col = {n: w[:, bounds[i]:bounds[i + 1]].astype(jnp.bfloat16) for i, n in enumerate(names)}
    out = {n: col[n] for n in names if col[n].shape[1] % 128 == 0}
    out["small"] = _wcast(jnp.concatenate([col["b_raw"], col["a_raw"], col["w_idx"]], axis=1))
    return out


def _mmf(x, wb, n):
    shp = x.shape
    y = _mm(x.reshape(-1, shp[-1]).astype(jnp.bfloat16), wb)[:, :n]
    return y.reshape(*shp[:-1], n)


def _layernorm(x, g, b):
    mu = jnp.mean(x, -1, keepdims=True)
    var = jnp.mean(jnp.square(x - mu), -1, keepdims=True)
    return (x - mu) * lax.rsqrt(var + LN_EPS) * g + b


def _rope(x, pos):
    half = x.shape[-1] // 2
    inv = ROPE_THETA ** (-jnp.arange(half, dtype=jnp.float32) / half)
    ang = pos.astype(jnp.float32)[:, None] * inv[None, :]
    cos = jnp.cos(ang)[None, :, None, :]
    sin = jnp.sin(ang)[None, :, None, :]
    x1 = x[..., :half]
    x2 = x[..., half:]
    return jnp.concatenate([x1 * cos - x2 * sin, x2 * cos + x1 * sin], -1)


def _l2n(a):
    return a * lax.rsqrt(jnp.sum(a * a, -1, keepdims=True) + RMS_EPS)


def _causal_conv(u, buf, conv_w):
    t = u.shape[1]
    up = jnp.concatenate([buf.astype(u.dtype), u], axis=1)
    out = sum(up[:, j:j + t] * conv_w[j] for j in range(CONV_W))
    return jax.nn.silu(out), up[:, t:]


def _gdn_qkv(conv_out):
    b, t, _ = conv_out.shape
    q, k, v = (a.reshape(b, t, H_GDN, D_HEAD) for a in jnp.split(conv_out, 3, axis=-1))
    return _l2n(q) * D_HEAD ** -0.5, _l2n(k), v


def _gdn_gates(b_raw, a_raw, a_log, dt_bias):
    beta = jax.nn.sigmoid(b_raw)
    g = -jnp.exp(a_log) * jax.nn.softplus(a_raw + dt_bias)
    return g, beta


def _gdn_chunked(q, k, v, g, beta, s0):
    b, t, h, _ = q.shape
    c = GDN_CHUNK
    n = t // c

    def ch(a):
        return jnp.swapaxes(jnp.moveaxis(a.reshape(b, n, c, h, *a.shape[3:]), 1, 0), 2, 3)

    qc, kc, vc, gc, bc = ch(q), ch(k), ch(v), ch(g), ch(beta)
    gam = jnp.cumsum(gc, axis=-1)
    incl = jnp.tril(jnp.ones((c, c), bool))
    strict = jnp.tril(jnp.ones((c, c), bool), -1)
    decay = jnp.exp(jnp.where(incl, gam[..., :, None] - gam[..., None, :], -jnp.inf))
    kb = kc * bc[..., None]
    n_mat = jnp.where(strict, jnp.einsum('nbhid,nbhjd->nbhij', kb, kc) * decay, 0.0)
    t_inv = _unit_lower_inverse(n_mat.reshape(-1, c, c)).reshape(n_mat.shape)
    u = jnp.einsum('nbhij,nbhjd->nbhid', t_inv, vc * bc[..., None])
    w = jnp.einsum('nbhij,nbhjd->nbhid', t_inv, kb * jnp.exp(gam)[..., None])
    qk = jnp.einsum('nbhid,nbhjd->nbhij', qc, kc) * decay

    g_last = gam[..., -1:]
    qg = qc * jnp.exp(gam)[..., None]
    kg = kc * jnp.exp(g_last - gam)[..., None]
    decay_last = jnp.broadcast_to(jnp.exp(g_last)[..., None], (n, b, h, 1, D_HEAD))
    outs = [_gdn_scan(s0[i], *(a[:, i] for a in (u, w, qg, kg, qk, decay_last))) for i in range(b)]
    o = jnp.stack([x[0] for x in outs], axis=1)
    s = jnp.stack([x[1] for x in outs], axis=0)
    o = jnp.moveaxis(jnp.swapaxes(o, 2, 3), 0, 1).reshape(b, t, h, -1)
    return o, s


TRI_LANES = 128


def _tri_inv_kernel(n_ref, t_ref):
    c, _, lanes = n_ref.shape
    rowid = lax.broadcasted_iota(jnp.int32, (c, lanes), 0)
    t_ref[...] = jnp.zeros(t_ref.shape, t_ref.dtype)

    def row(i, carry):
        def term(j, acc):
            return acc - n_ref[i, pl.ds(j, 1), :] * t_ref[j]

        t_ref[i] = lax.fori_loop(0, c, term, jnp.where(rowid == i, 1.0, 0.0), unroll=8)
        return carry

    lax.fori_loop(0, c, row, 0)


def _unit_lower_inverse(nm):
    s, c, _ = nm.shape
    sp = -(-s // TRI_LANES) * TRI_LANES
    nt = jnp.pad(nm, ((0, sp - s), (0, 0), (0, 0))).transpose(1, 2, 0)
    spec = pl.BlockSpec((c, c, TRI_LANES), lambda g: (0, 0, g))
    t = pl.pallas_call(
        _tri_inv_kernel,
        grid=(sp // TRI_LANES,),
        in_specs=[spec],
        out_specs=spec,
        out_shape=jax.ShapeDtypeStruct((c, c, sp), jnp.float32),
        compiler_params=pltpu.CompilerParams(dimension_semantics=("parallel",), vmem_limit_bytes=VMEM_LIMIT),
        name="tri_inv",
    )(nt)
    return t.transpose(2, 0, 1)[:s]


def _gdn_scan_kernel(s0_ref, u_ref, w_ref, qg_ref, kg_ref, qk_ref, dl_ref, o_ref, s_ref):
    bf = jnp.bfloat16

    @pl.when(pl.program_id(0) == 0)
    def _():
        s_ref[...] = s0_ref[...]

    for hh in range(s_ref.shape[0]):
        s = s_ref[hh]
        sb = s.astype(bf)
        v_new = u_ref[0, hh] - jnp.dot(w_ref[0, hh].astype(bf), sb, preferred_element_type=jnp.float32)
        vb = v_new.astype(bf)
        o_ref[0, hh] = (jnp.dot(qg_ref[0, hh].astype(bf), sb, preferred_element_type=jnp.float32)
                        + jnp.dot(qk_ref[0, hh].astype(bf), vb, preferred_element_type=jnp.float32))
        s_ref[hh] = s * dl_ref[0, hh] + lax.dot_general(
            kg_ref[0, hh].astype(bf), vb, (((0,), (0,)), ((), ())), preferred_element_type=jnp.float32)


def _gdn_scan(s0, u, w, qg, kg, qk, decay_last):
    n, h, c, d = u.shape
    blk = lambda *shape: pl.BlockSpec((1,) + shape, lambda i: (i,) + (0,) * len(shape))
    state = pl.BlockSpec((h, d, d), lambda i: (0, 0, 0))
    o, s = pl.pallas_call(
        _gdn_scan_kernel,
        grid=(n,),
        in_specs=[state, blk(h, c, d), blk(h, c, d), blk(h, c, d), blk(h, c, d), blk(h, c, c), blk(h, 1, d)],
        out_specs=[blk(h, c, d), state],
        out_shape=[jax.ShapeDtypeStruct((n, h, c, d), jnp.float32), jax.ShapeDtypeStruct((h, d, d), jnp.float32)],
        compiler_params=pltpu.CompilerParams(dimension_semantics=("arbitrary",), vmem_limit_bytes=VMEM_LIMIT),
        name="gdn_scan",
    )(s0, u, w, qg, kg, qk, decay_last)
    return o, s


def _gdn_step_kernel(s0_ref, q_ref, k_ref, kb_ref, v_ref, dec_ref, o_ref, s_ref):
    nt = v_ref.shape[2]
    for hh in range(s0_ref.shape[1]):
        s = s0_ref[0, hh]
        for tt in range(nt):
            s = s * dec_ref[0, hh, tt:tt + 1, :]
            ks = jnp.sum(s * k_ref[0, hh, :, tt:tt + 1], axis=0, keepdims=True)
            s = s + kb_ref[0, hh, :, tt:tt + 1] * (v_ref[0, hh, tt:tt + 1, :] - ks)
            o_ref[0, hh, tt:tt + 1, :] = jnp.sum(s * q_ref[0, hh, :, tt:tt + 1], axis=0, keepdims=True)
        s_ref[0, hh] = s


def _gdn_recurrent(q, k, v, g, beta, s0):
    b, t, h, d = q.shape
    cols = lambda a: a.transpose(0, 2, 3, 1)
    rows = lambda a: a.transpose(0, 2, 1, 3)
    dec = jnp.broadcast_to(jnp.exp(g).transpose(0, 2, 1)[..., None], (b, h, t, d))
    cspec = pl.BlockSpec((1, h, d, t), lambda i: (i, 0, 0, 0))
    rspec = pl.BlockSpec((1, h, t, d), lambda i: (i, 0, 0, 0))
    sspec = pl.BlockSpec((1, h, d, d), lambda i: (i, 0, 0, 0))
    o, s = pl.pallas_call(
        _gdn_step_kernel,
        grid=(b,),
        in_specs=[sspec, cspec, cspec, cspec, rspec, rspec],
        out_specs=[rspec, sspec],
        out_shape=[jax.ShapeDtypeStruct((b, h, t, d), jnp.float32), jax.ShapeDtypeStruct((b, h, d, d), jnp.float32)],
        compiler_params=pltpu.CompilerParams(dimension_semantics=("parallel",), vmem_limit_bytes=VMEM_LIMIT),
        name="gdn_step",
    )(s0, cols(q), cols(k), cols(k * beta[..., None]), rows(v), dec)
    return o.transpose(0, 2, 1, 3), s


def _gdn_out(o, z, norm_w):
    b, t = z.shape[:2]
    o = o * lax.rsqrt(jnp.mean(o * o, -1, keepdims=True) + RMS_EPS) * norm_w
    return o.reshape(b, t, D_GDN) * jax.nn.silu(z)


def _att_heads(q_att, k_att, v_att, q_idx, w_idx, k_idx, pos):
    b, t, _ = q_att.shape
    qa = _rope(q_att.reshape(b, t, H_ATT, D_HEAD), pos)
    ka = _rope(k_att.reshape(b, t, H_ATT, D_HEAD), pos)
    va = v_att.reshape(b, t, H_ATT, D_HEAD)
    qi = _rope(q_idx.reshape(b, t, H_IDX, D_IDX), pos)
    ki = _rope(k_idx[:, :, None, :], pos)[:, :, 0]
    wi = w_idx * (H_IDX ** -0.5 * D_IDX ** -0.5)
    return qa, ka, va, qi, wi, ki


def _indexer_topk(qi, wi, ki, q_pos, n_sel):
    rel = jax.nn.relu(jnp.einsum('bthd,bsd->bths', qi, ki))
    score = jnp.einsum('bths,bth->bts', rel, wi).astype(jnp.float32)
    key_pos = jnp.arange(ki.shape[1])
    score = jnp.where(key_pos[None, None, :] <= q_pos[None, :, None], score, -jnp.inf)
    _, sel = lax.top_k(score, n_sel)
    return sel


def _sparse_attend(q, k_sel, v_sel, sel, q_pos):
    logits = jnp.einsum('bthd,btkhd->bthk', q, k_sel).astype(jnp.float32) * D_HEAD ** -0.5
    valid = (sel <= q_pos[None, :, None])[:, :, None, :]
    p = jax.nn.softmax(jnp.where(valid, logits, -jnp.inf), axis=-1)
    return jnp.einsum('bthk,btkhd->bthd', p.astype(v_sel.dtype), v_sel)


MXU_COLS = 256
DSA_TQ = 128
DSA_TK = 256
DSA_HG = max(1, MXU_COLS // DSA_TQ)
DSA_SUB = 2
DSA_TS = DSA_SUB * DSA_TK
INT32_MIN = np.int32(-2 ** 31)


def _float_to_key(x):
    bits = lax.bitcast_convert_type(x + 0.0, jnp.int32)
    return bits ^ ((bits >> 31) & jnp.int32(0x7FFFFFFF))


def _kth_largest_key(xs_ref, nch, ch, n_sel):
    lanes = xs_ref.shape[1]

    def count_ge(cand):
        def body(c, acc):
            r0 = pl.multiple_of(c * ch, ch)
            ind = jnp.where(xs_ref[pl.ds(r0, ch), :] >= cand, 1, 0).astype(jnp.int32)
            return acc + jnp.sum(ind.reshape(ch // 8, 8, lanes), axis=0)

        acc = lax.fori_loop(0, nch, body, jnp.zeros((8, lanes), jnp.int32))
        return jnp.sum(acc, axis=0, keepdims=True)

    def bit_cond(st):
        b, _, open_ = st
        return jnp.logical_and(b < 32, jnp.max(open_) > 0)

    def one_bit(b, thr_u, open_):
        cand_u = thr_u | lax.shift_left(jnp.int32(1), 31 - b)
        cnt = count_ge(cand_u ^ INT32_MIN)
        take = jnp.logical_and(cnt >= n_sel, open_ > 0)
        return jnp.where(take, cand_u, thr_u), jnp.where(cnt == n_sel, 0, open_)

    def bit_body(st):
        b, thr_u, open_ = st
        thr_u, open_ = one_bit(b, thr_u, open_)
        thr_u, open_ = one_bit(b + 1, thr_u, open_)
        return b + 2, thr_u, open_

    _, thr_u, _ = lax.while_loop(
        bit_cond, bit_body,
        (jnp.int32(0), jnp.zeros((1, lanes), jnp.int32), jnp.ones((1, lanes), jnp.int32)))
    return thr_u ^ INT32_MIN


def _dsa_prompt_kernel(ii_ref, jj_ref, qi_ref, w_ref, ki_ref, q_ref, k_ref, vt_ref, o_ref,
                       xs_ref, thr_ref, bias_ref, m_ref, l_ref, acc_ref, *, n_sel):
    s_id = pl.program_id(0)
    i = ii_ref[s_id]
    j = jj_ref[s_id]
    last = (i * DSA_TQ + DSA_TQ - 1) // DSA_TS
    nch = (last + 1) * DSA_SUB
    kiota = lax.broadcasted_iota(jnp.int32, (DSA_TK, DSA_TQ), 0)
    qpos = i * DSA_TQ + lax.broadcasted_iota(jnp.int32, (DSA_TK, DSA_TQ), 1)

    @pl.when(j == 0)
    def _index():
        def score_body(c, carry):
            r0 = pl.multiple_of(c * DSA_TK, DSA_TK)
            kchunk = ki_ref[pl.ds(r0, DSA_TK), :]
            sc = jnp.zeros((DSA_TK, DSA_TQ), jnp.float32)
            for hg in range(H_IDX // DSA_HG):
                z = jnp.dot(kchunk, qi_ref[0, hg], preferred_element_type=jnp.float32)
                for sub in range(DSA_HG):
                    hd = hg * DSA_HG + sub
                    sc = sc + w_ref[0, hd:hd + 1, :] * jnp.maximum(z[:, sub * DSA_TQ:(sub + 1) * DSA_TQ], 0.0)
            xs_ref[pl.ds(r0, DSA_TK), :] = jnp.where(r0 + kiota <= qpos, _float_to_key(sc), INT32_MIN)
            return carry

        lax.fori_loop(0, nch, score_body, 0)

        thr_ref[...] = _kth_largest_key(xs_ref, nch, DSA_TK, n_sel)
        m_ref[...] = jnp.full(m_ref.shape, -1e30, jnp.float32)
        l_ref[...] = jnp.zeros(l_ref.shape, jnp.float32)
        acc_ref[...] = jnp.zeros(acc_ref.shape, jnp.float32)

    for sub in range(DSA_SUB):
        r0 = pl.multiple_of(j * DSA_TS + sub * DSA_TK, DSA_TK)
        sel = jnp.where(xs_ref[pl.ds(r0, DSA_TK), :] >= thr_ref[...], r0 + kiota, jnp.int32(2 ** 30)) <= qpos
        bias_ref[sub * DSA_TK:(sub + 1) * DSA_TK, :] = jnp.where(sel, 0.0, -jnp.inf)
    for h in range(H_ATT):
        for sub in range(DSA_SUB):
            rows = slice(sub * DSA_TK, (sub + 1) * DSA_TK)
            s = jnp.dot(k_ref[rows, h * D_HEAD:(h + 1) * D_HEAD], q_ref[h], preferred_element_type=jnp.float32)
            s = s + bias_ref[rows, :]
            m_old = m_ref[h:h + 1, :]
            m_new = jnp.maximum(m_old, jnp.max(s, axis=0, keepdims=True))
            alpha = jnp.exp(m_old - m_new)
            p = jnp.exp(s - m_new)
            l_ref[h:h + 1, :] = alpha * l_ref[h:h + 1, :] + jnp.sum(p, axis=0, keepdims=True)
            m_ref[h:h + 1, :] = m_new
            acc_ref[h] = alpha * acc_ref[h] + jnp.dot(vt_ref[h, :, rows], p.astype(jnp.bfloat16),
                                                      preferred_element_type=jnp.float32)

    @pl.when(j == last)
    def _finish():
        for h in range(H_ATT):
            o_ref[:, h * D_HEAD:(h + 1) * D_HEAD] = (acc_ref[h] / l_ref[h:h + 1, :]).T


def _dsa_prompt_one(q, k, v, qi, wi, ki):
    t = q.shape[0]
    assert t % DSA_TS == 0 and DSA_TK % DSA_TQ == 0
    nq = t // DSA_TQ
    n_sel = min(TOPK_ATT_MAX, t // 4)
    bf = jnp.bfloat16
    qit = qi.reshape(nq, DSA_TQ, H_IDX // DSA_HG, DSA_HG, D_IDX).transpose(0, 2, 4, 3, 1).reshape(
        nq, H_IDX // DSA_HG, D_IDX, DSA_HG * DSA_TQ).astype(bf)
    wr = wi.reshape(nq, DSA_TQ, H_IDX).transpose(0, 2, 1)
    qt = (q * D_HEAD ** -0.5).transpose(1, 2, 0).astype(bf)
    kf = k.reshape(t, D_ATT).astype(bf)
    vt = v.transpose(1, 2, 0).astype(bf)
    steps = [(i, j) for i in range(nq) for j in range((i * DSA_TQ + DSA_TQ - 1) // DSA_TS + 1)]
    ii = jnp.asarray([s[0] for s in steps], jnp.int32)
    jj = jnp.asarray([s[1] for s in steps], jnp.int32)
    grid_spec = pltpu.PrefetchScalarGridSpec(
        num_scalar_prefetch=2,
        grid=(len(steps),),
        in_specs=[
            pl.BlockSpec((1, H_IDX // DSA_HG, D_IDX, DSA_HG * DSA_TQ), lambda s, ii, jj: (ii[s], 0, 0, 0)),
            pl.BlockSpec((1, H_IDX, DSA_TQ), lambda s, ii, jj: (ii[s], 0, 0)),
            pl.BlockSpec((t, D_IDX), lambda s, ii, jj: (0, 0)),
            pl.BlockSpec((H_ATT, D_HEAD, DSA_TQ), lambda s, ii, jj: (0, 0, ii[s])),
            pl.BlockSpec((DSA_TS, D_ATT), lambda s, ii, jj: (jj[s], 0)),
            pl.BlockSpec((H_ATT, D_HEAD, DSA_TS), lambda s, ii, jj: (0, 0, jj[s])),
        ],
        out_specs=pl.BlockSpec((DSA_TQ, D_ATT), lambda s, ii, jj: (ii[s], 0)),
        scratch_shapes=[
            pltpu.VMEM((t, DSA_TQ), jnp.int32),
            pltpu.VMEM((1, DSA_TQ), jnp.int32),
            pltpu.VMEM((DSA_TS, DSA_TQ), jnp.float32),
            pltpu.VMEM((H_ATT, DSA_TQ), jnp.float32),
            pltpu.VMEM((H_ATT, DSA_TQ), jnp.float32),
            pltpu.VMEM((H_ATT, D_HEAD, DSA_TQ), jnp.float32),
        ])
    return pl.pallas_call(
        functools.partial(_dsa_prompt_kernel, n_sel=n_sel),
        grid_spec=grid_spec,
        out_shape=jax.ShapeDtypeStruct((t, D_ATT), jnp.float32),
        compiler_params=pltpu.CompilerParams(dimension_semantics=("arbitrary",), vmem_limit_bytes=VMEM_LIMIT),
        name="dsa_prompt",
    )(ii, jj, qit, wr, ki.astype(bf), qt, kf, vt)


def _dsa_prompt(q, k, v, qi, wi, ki):
    return jnp.stack([_dsa_prompt_one(q[b], k[b], v[b], qi[b], wi[b], ki[b]) for b in range(q.shape[0])])


SMP_IDX_PAGES = 8
SMP_ATT_PAGES = 8
SMP_CH = 128


def _smp_score_kernel(pt_ref, *refs):
    pages = refs[:SMP_IDX_PAGES]
    qi_ref, wb_ref, o_ref = refs[SMP_IDX_PAGES:]
    nq = o_ref.shape[1]
    for g, ki_ref in enumerate(pages):
        z = lax.dot_general(qi_ref[0], ki_ref[0, 0].astype(jnp.bfloat16), (((1,), (1,)), ((), ())),
                            preferred_element_type=jnp.float32)
        r = jnp.maximum(z, 0.0) * wb_ref[0]
        o_ref[0, :, g * PAGE_SIZE:(g + 1) * PAGE_SIZE] = jnp.sum(r.reshape(nq, H_IDX, PAGE_SIZE), axis=1)


def _smp_thr_kernel(s_ref, thr_ref, xs_ref, *, n_sel):
    nch = s_ref.shape[0] // SMP_CH

    def conv(c, carry):
        r0 = pl.multiple_of(c * SMP_CH, SMP_CH)
        xs_ref[pl.ds(r0, SMP_CH), :] = _float_to_key(s_ref[pl.ds(r0, SMP_CH), :])
        return carry

    lax.fori_loop(0, nch, conv, 0)
    key = _kth_largest_key(xs_ref, nch, SMP_CH, n_sel)
    thr_ref[...] = lax.bitcast_convert_type(key ^ ((key >> 31) & jnp.int32(0x7FFFFFFF)), jnp.float32)


def _smp_attn_kernel(pt_ref, *refs, nq):
    kp = refs[:SMP_ATT_PAGES]
    vp = refs[SMP_ATT_PAGES:2 * SMP_ATT_PAGES]
    (sc_ref, scn_ref, thr_ref, q_ref, kn_ref, vn_ref, ex_ref, hm_ref,
     o_ref, m_ref, l_ref, acc_ref) = refs[2 * SMP_ATT_PAGES:]
    pp = pl.program_id(1)
    rows = PAGE_SIZE * H_ATT

    def attend(k, v, sc):
        sel = jnp.where(sc >= thr_ref[0], 1.0, 0.0)
        sel = jnp.concatenate(
            [jnp.broadcast_to(sel[tt:tt + 1, :], (H_ATT, PAGE_SIZE)) for tt in range(nq)], axis=0)
        valid = jnp.dot(sel.astype(jnp.bfloat16), ex_ref[...], preferred_element_type=jnp.float32) * hm_ref[...]
        s = lax.dot_general(q_ref[0], k.reshape(rows, D_HEAD).astype(jnp.bfloat16), (((1,), (1,)), ((), ())),
                            preferred_element_type=jnp.float32)
        s = jnp.where(valid > 0.5, s, -jnp.inf)
        m = jnp.maximum(jnp.max(s, axis=1, keepdims=True), -1e30)
        p = jnp.exp(s - m)
        acc = jnp.dot(p.astype(jnp.bfloat16), v.reshape(rows, D_HEAD).astype(jnp.bfloat16),
                      preferred_element_type=jnp.float32)
        return m, jnp.sum(p, axis=1, keepdims=True), acc

    def merge(m, l, acc):
        m_old = m_ref[...]
        m_new = jnp.maximum(m_old, m)
        a_old = jnp.exp(m_old - m_new)
        a_new = jnp.exp(m - m_new)
        l_ref[...] = a_old * l_ref[...] + a_new * l
        acc_ref[...] = a_old * acc_ref[...] + a_new * acc
        m_ref[...] = m_new

    @pl.when(pp == 0)
    def _():
        m_ref[...] = jnp.full(m_ref.shape, -1e30, jnp.float32)
        l_ref[...] = jnp.zeros(l_ref.shape, jnp.float32)
        acc_ref[...] = jnp.zeros(acc_ref.shape, jnp.float32)
        merge(*attend(kn_ref[0], vn_ref[0], scn_ref[0]))

    parts = [attend(kp[g][0, 0], vp[g][0, 0], sc_ref[0, :, g * PAGE_SIZE:(g + 1) * PAGE_SIZE])
             for g in range(SMP_ATT_PAGES)]
    for part in parts:
        merge(*part)

    @pl.when(pp == pl.num_programs(1) - 1)
    def _():
        o_ref[0] = acc_ref[...] / l_ref[...]


def _dsa_sample(q, k_new, v_new, qi, wi, ki_new, cache_k, cache_v, cache_idx_k, page_table, layer):
    db, t = q.shape[:2]
    npg = page_table.shape[1]
    past_len = npg * PAGE_SIZE
    n_sel = min(TOPK_ATT_MAX, (past_len + t) // 4)
    assert npg % SMP_IDX_PAGES == 0 and npg % SMP_ATT_PAGES == 0 and (t * H_IDX) % 8 == 0
    bf = jnp.bfloat16
    pt = page_table.reshape(-1).astype(jnp.int32)
    n_pool = cache_k.shape[1]

    def page_spec(width, per_step, g):
        return pl.BlockSpec((1, 1, PAGE_SIZE, width),
                            lambda b, p, pt: (layer, pt[b * npg + p * per_step + g], 0, 0))

    row_spec = lambda rows, width: pl.BlockSpec((1, rows, width), lambda b, p, pt: (b, 0, 0))
    score_past = pl.pallas_call(
        _smp_score_kernel,
        grid_spec=pltpu.PrefetchScalarGridSpec(
            num_scalar_prefetch=1, grid=(db, npg // SMP_IDX_PAGES),
            in_specs=[page_spec(D_IDX, SMP_IDX_PAGES, g) for g in range(SMP_IDX_PAGES)]
            + [row_spec(t * H_IDX, D_IDX), row_spec(t * H_IDX, PAGE_SIZE)],
            out_specs=pl.BlockSpec((1, t, SMP_IDX_PAGES * PAGE_SIZE), lambda b, p, pt: (b, 0, p))),
        out_shape=jax.ShapeDtypeStruct((db, t, past_len), jnp.float32),
        compiler_params=pltpu.CompilerParams(dimension_semantics=("parallel", "arbitrary"),
                                             vmem_limit_bytes=VMEM_LIMIT),
        name="smp_score",
    )(pt, *([cache_idx_k] * SMP_IDX_PAGES), qi.reshape(db, t * H_IDX, D_IDX).astype(bf),
      jnp.broadcast_to(wi.reshape(db, t * H_IDX, 1), (db, t * H_IDX, PAGE_SIZE)))

    rel = jax.nn.relu(jnp.einsum('bthd,bsd->bths', qi, ki_new))
    score_new = jnp.einsum('bths,bth->bts', rel, wi)
    score_new = jnp.where(jnp.arange(t)[None, None, :] <= jnp.arange(t)[None, :, None], score_new, -jnp.inf)
    score_new = jnp.pad(score_new, ((0, 0), (0, 0), (0, PAGE_SIZE - t)), constant_values=-jnp.inf)

    score_t = jnp.concatenate([score_past, score_new], axis=2).reshape(db * t, past_len + PAGE_SIZE).T
    thr = pl.pallas_call(
        functools.partial(_smp_thr_kernel, n_sel=n_sel),
        out_shape=jax.ShapeDtypeStruct((1, db * t), jnp.float32),
        scratch_shapes=[pltpu.VMEM(score_t.shape, jnp.int32)],
        compiler_params=pltpu.CompilerParams(vmem_limit_bytes=VMEM_LIMIT),
        name="smp_thr",
    )(score_t)
    thr_b = jnp.broadcast_to(thr.reshape(db, t, 1), (db, t, PAGE_SIZE))

    nrow = t * H_ATT
    cols = PAGE_SIZE * H_ATT
    q_rows = (q * D_HEAD ** -0.5).reshape(db, nrow, D_HEAD).astype(bf)
    expand = jnp.repeat(jnp.eye(PAGE_SIZE, dtype=bf), H_ATT, axis=1)
    head_match = (jnp.arange(cols)[None, :] % H_ATT == jnp.arange(nrow)[:, None] % H_ATT).astype(jnp.float32)
    pad_page = lambda a: jnp.pad(a, ((0, 0), (0, PAGE_SIZE - t), (0, 0), (0, 0)))
    cache_spec = lambda g: pl.BlockSpec(
        (1, 1, PAGE_SIZE, H_ATT, D_HEAD),
        lambda b, p, pt: (layer, pt[b * npg + p * SMP_ATT_PAGES + g], 0, 0, 0))
    const_spec = lambda shape: pl.BlockSpec(shape, lambda b, p, pt: (0,) * len(shape))
    new_spec = pl.BlockSpec((1, PAGE_SIZE, H_ATT, D_HEAD), lambda b, p, pt: (b, 0, 0, 0))
    out = pl.pallas_call(
        functools.partial(_smp_attn_kernel, nq=t),
        grid_spec=pltpu.PrefetchScalarGridSpec(
            num_scalar_prefetch=1, grid=(db, npg // SMP_ATT_PAGES),
            in_specs=[cache_spec(g) for g in range(SMP_ATT_PAGES)] * 2
            + [pl.BlockSpec((1, t, SMP_ATT_PAGES * PAGE_SIZE), lambda b, p, pt: (b, 0, p)),
               row_spec(t, PAGE_SIZE), row_spec(t, PAGE_SIZE), row_spec(nrow, D_HEAD),
               new_spec, new_spec, const_spec((PAGE_SIZE, cols)), const_spec((nrow, cols))],
            out_specs=row_spec(nrow, D_HEAD),
            scratch_shapes=[pltpu.VMEM((nrow, 1), jnp.float32), pltpu.VMEM((nrow, 1), jnp.float32),
                            pltpu.VMEM((nrow, D_HEAD), jnp.float32)]),
        out_shape=jax.ShapeDtypeStruct((db, nrow, D_HEAD), jnp.float32),
        compiler_params=pltpu.CompilerParams(dimension_semantics=("parallel", "arbitrary"),
                                             vmem_limit_bytes=VMEM_LIMIT),
        name="smp_attn",
    )(pt, *([cache_k] * SMP_ATT_PAGES), *([cache_v] * SMP_ATT_PAGES),
      score_past, score_new, thr_b, q_rows, pad_page(k_new), pad_page(v_new), expand, head_match)
    return out.reshape(db, t, D_ATT)


def _peer(x, wq, k1, k2, u, v):
    shape = x.shape
    xf = x.reshape(-1, D_MODEL)
    n = xf.shape[0]
    pad = (-n) % PEER_BLOCK
    xf = jnp.pad(xf, ((0, pad), (0, 0)))
    xb16 = xf.astype(jnp.bfloat16)
    e1, e2, gate = _peer_route(_mm(xb16, wq), k1, k2)
    wmat = _peer_gate_matrix(e1, e2, gate)
    out = _peer_dense(xb16, u, v, wmat)
    return out[:n].reshape(shape)


PEER_RT = 128
PEER_RH = 2
PEER_TOPK_LOG2 = PEER_TOPK.bit_length() - 1
assert 1 << PEER_TOPK_LOG2 == PEER_TOPK
PEER_PAIRS = tuple((i, j) for i in range(PEER_TOPK) for j in range(PEER_TOPK) if (i + 1) * (j + 1) <= PEER_TOPK)


def _extract_top(s, ids, n):
    big = jnp.int32(2 ** 30)
    vals, sel = [], []
    for _ in range(n):
        m = jnp.max(s, axis=0, keepdims=True)
        pick = jnp.min(jnp.where(s == m, ids, big), axis=0, keepdims=True)
        vals.append(m)
        sel.append(pick)
        s = jnp.where(ids == pick, -jnp.inf, s)
    return vals, sel


def _peer_route_kernel(q_ref, k1_ref, k2_ref, e1_ref, e2_ref, g_ref):
    kio = lax.broadcasted_iota(jnp.int32, (PEER_NKEYS, PEER_RT), 0)
    flat = jnp.concatenate(
        [jnp.full((1, PEER_RT), i * PEER_TOPK + j, jnp.int32) for i, j in PEER_PAIRS], axis=0)
    for hh in range(PEER_RH):
        tops = []
        for side, kref in ((0, k1_ref), (1, k2_ref)):
            lo = (2 * hh + side) * (PEER_DQ // 2)
            qs = q_ref[:, lo:lo + PEER_DQ // 2].astype(jnp.bfloat16)
            s = lax.dot_general(kref[hh], qs, (((1,), (1,)), ((), ())),
                                preferred_element_type=jnp.float32)
            tops.append(_extract_top(s, kio, PEER_TOPK))
        (v1, i1), (v2, i2) = tops
        cand = jnp.concatenate([v1[i] + v2[j] for i, j in PEER_PAIRS], axis=0)
        top_s, top_f = _extract_top(cand, flat, PEER_TOPK)
        ts = jnp.concatenate(top_s, axis=0)
        tf = jnp.concatenate(top_f, axis=0)
        ex = jnp.exp(ts - top_s[0])
        g_ref[hh] = ex * (1.0 / jnp.sum(ex, axis=0, keepdims=True))
        fi = tf >> PEER_TOPK_LOG2
        fj = tf & (PEER_TOPK - 1)
        e1_ref[hh] = sum(jnp.where(fi == i, i1[i], 0) for i in range(PEER_TOPK))
        e2_ref[hh] = sum(jnp.where(fj == j, i2[j], 0) for j in range(PEER_TOPK))


def _peer_route(q, k1, k2):
    n = q.shape[0]
    assert n % PEER_RT == 0 and PEER_HEADS % PEER_RH == 0
    kspec = pl.BlockSpec((PEER_RH, PEER_NKEYS, PEER_DQ // 2), lambda i, h: (h, 0, 0))
    ospec = pl.BlockSpec((PEER_RH, PEER_TOPK, PEER_RT), lambda i, h: (h, 0, i))
    oshape = lambda dt: jax.ShapeDtypeStruct((PEER_HEADS, PEER_TOPK, n), dt)
    e1, e2, g = pl.pallas_call(
        _peer_route_kernel,
        grid=(n // PEER_RT, PEER_HEADS // PEER_RH),
        in_specs=[pl.BlockSpec((PEER_RT, PEER_RH * PEER_DQ), lambda i, h: (i, h)), kspec, kspec],
        out_specs=[ospec, ospec, ospec],
        out_shape=[oshape(jnp.int32), oshape(jnp.int32), oshape(jnp.float32)],
        compiler_params=pltpu.CompilerParams(dimension_semantics=("parallel", "parallel"),
                                             vmem_limit_bytes=VMEM_LIMIT),
        name="peer_route",
    )(q, k1.astype(jnp.bfloat16), k2.astype(jnp.bfloat16))
    tok_major = lambda a: a.reshape(PEER_HEADS * PEER_TOPK, n).T
    return tok_major(e1), tok_major(e2), tok_major(g)


PEER_GT = 16


def _peer_gate_kernel(i1_ref, i2_ref, g_ref, w_ref, tmp_ref):
    iota = lax.broadcasted_iota(jnp.int32, (PEER_NKEYS, PEER_HEADS * PEER_TOPK), 0)

    def body(gi, carry):
        t0 = pl.multiple_of(gi * PEER_GT, PEER_GT)
        for kk in range(PEER_GT):
            a = jnp.where(iota == i1_ref[pl.ds(t0 + kk, 1), :], g_ref[pl.ds(t0 + kk, 1), :], 0.0).astype(jnp.bfloat16)
            b = jnp.where(iota == i2_ref[pl.ds(t0 + kk, 1), :], 1.0, 0.0).astype(jnp.bfloat16)
            tmp_ref[kk * PEER_NKEYS:(kk + 1) * PEER_NKEYS, :] = lax.dot_general(
                a, b, (((1,), (1,)), ((), ())), preferred_element_type=jnp.float32)
        for aa in range(PEER_NKEYS):
            rows = tmp_ref[pl.ds(aa, PEER_GT, stride=PEER_NKEYS), :]
            w_ref[pl.ds(t0, PEER_GT), aa * PEER_NKEYS:(aa + 1) * PEER_NKEYS] = rows.astype(w_ref.dtype)
        return carry

    lax.fori_loop(0, i1_ref.shape[0] // PEER_GT, body, 0)


def _peer_gate_matrix(e1, e2, gate):
    n, p = e1.shape
    tw = _pick(n, (128,))
    assert tw % PEER_GT == 0
    spec = pl.BlockSpec((tw, p), lambda i: (i, 0))
    return pl.pallas_call(
        _peer_gate_kernel,
        grid=(n // tw,),
        in_specs=[spec, spec, spec],
        out_specs=pl.BlockSpec((tw, PEER_NKEYS * PEER_NKEYS), lambda i: (i, 0)),
        out_shape=jax.ShapeDtypeStruct((n, PEER_NKEYS * PEER_NKEYS), jnp.bfloat16),
        scratch_shapes=[pltpu.VMEM((PEER_GT * PEER_NKEYS, PEER_NKEYS), jnp.float32)],
        compiler_params=pltpu.CompilerParams(dimension_semantics=("parallel",), vmem_limit_bytes=VMEM_LIMIT),
        name="peer_gate",
    )(e1, e2, gate)


def _peer_dense_kernel(h_ref, ut_ref, w_ref, v_ref, o_ref):
    @pl.when(pl.program_id(1) == 0)
    def _():
        o_ref[...] = jnp.zeros_like(o_ref)

    half = ut_ref.shape[1] // 2
    upd = None
    for s in range(2):
        z = jnp.dot(h_ref[...], ut_ref[:, s * half:(s + 1) * half], preferred_element_type=jnp.float32)
        act = 0.5 * z * (1.0 + lax.erf(z * (2.0 ** -0.5)))
        c = (w_ref[:, s * half:(s + 1) * half].astype(jnp.float32) * act).astype(jnp.bfloat16)
        d = jnp.dot(c, v_ref[s * half:(s + 1) * half, :], preferred_element_type=jnp.float32)
        upd = d if upd is None else upd + d
    o_ref[...] += upd


def _peer_dense(xb16, ut, vb, wmat):
    n = xb16.shape[0]
    e = vb.shape[0]
    tm = _pick(n, (512, 256, 128))
    te = 512
    return pl.pallas_call(
        _peer_dense_kernel,
        grid=(n // tm, e // te),
        in_specs=[pl.BlockSpec((tm, D_MODEL), lambda i, j: (i, 0)),
                  pl.BlockSpec((D_MODEL, te), lambda i, j: (0, j)),
                  pl.BlockSpec((tm, te), lambda i, j: (i, j)),
                  pl.BlockSpec((te, D_MODEL), lambda i, j: (j, 0))],
        out_specs=pl.BlockSpec((tm, D_MODEL), lambda i, j: (i, 0)),
        out_shape=jax.ShapeDtypeStruct((n, D_MODEL), jnp.float32),
        compiler_params=pltpu.CompilerParams(dimension_semantics=("parallel", "arbitrary"),
                                             vmem_limit_bytes=56 * 1024 * 1024),
        name="peer_dense",
    )(xb16, ut, wmat, vb)


def _group(x, pos, conv_buf, ssm0, chunked, attend, w_in, conv_w, a_log, dt_bias, gdn_norm_w,
           w_br_gdn, w_br_att, w_out, ln1_g, ln1_b, peer_wq, peer_k1, peer_k2, peer_u, peer_v, ln2_g, ln2_b):
    xb = x.reshape(-1, D_MODEL).astype(jnp.bfloat16)
    proj = lambda name: _mm(xb, w_in[name]).reshape(*x.shape[:-1], -1)
    qkv, z, q_att, k_att, v_att, q_idx, k_idx, gate_gdn, gate_att = (
        proj(n) for n in ("qkv", "z", "q_att", "k_att", "v_att", "q_idx", "k_idx", "gate_gdn", "gate_att"))
    small = proj("small")
    b_raw, a_raw, w_idx = small[..., :H_GDN], small[..., H_GDN:2 * H_GDN], small[..., 2 * H_GDN:2 * H_GDN + H_IDX]
    conv_out, conv_new = _causal_conv(qkv, conv_buf, conv_w)
    q, k, v = _gdn_qkv(conv_out)
    g, beta = _gdn_gates(b_raw, a_raw, a_log, dt_bias)
    gdn = _gdn_chunked if chunked else _gdn_recurrent
    o, ssm_new = gdn(q, k, v, g, beta, ssm0.astype(jnp.float32))
    o_gdn = _gdn_out(o, z, gdn_norm_w)
    qa, ka, va, qi, wi, ki = _att_heads(q_att, k_att, v_att, q_idx, w_idx, k_idx, pos)
    o_att = attend(qa, ka, va, qi, wi, ki)
    mix = _mmf(jax.nn.sigmoid(gate_gdn) * _mmf(o_gdn, w_br_gdn, D_MODEL) + jax.nn.sigmoid(gate_att) * _mmf(o_att, w_br_att, D_MODEL),
               w_out, D_MODEL)
    h = _layernorm(DEEPNORM_ALPHA * x + mix, ln1_g, ln1_b)
    y = _layernorm(DEEPNORM_ALPHA * h + _peer(h, peer_wq, peer_k1, peer_k2, peer_u, peer_v), ln2_g, ln2_b)
    return y, ka, va, ki, conv_new, ssm_new.astype(x.dtype)


def kernel(x_prompt, x_sample, cache_k, cache_v, cache_idx_k, state_conv, state_ssm, page_table, w_in, conv_w, a_log, dt_bias, gdn_norm_w, w_br_gdn, w_br_att, w_out, ln1_g, ln1_b, peer_wq, peer_k1, peer_k2, peer_u, peer_v, ln2_g, ln2_b):
    bp, sp = x_prompt.shape[:2]
    ds = x_sample.shape[1]
    past_len = page_table.shape[1] * PAGE_SIZE
    pos_p = jnp.arange(sp)
    pos_s = past_len + jnp.arange(ds)
    l = 0
    wts = (_in_proj_weights(w_in[l]), conv_w[l], a_log[l], dt_bias[l], gdn_norm_w[l], _wcast(w_br_gdn[l]), _wcast(w_br_att[l]),
           _wcast(w_out[l]), ln1_g[l], ln1_b[l], _wcast(peer_wq[l]), peer_k1[l], peer_k2[l],
           peer_u[l].astype(jnp.bfloat16).T, peer_v[l].astype(jnp.bfloat16), ln2_g[l], ln2_b[l])
    y_p, kp, vp, ip, cp, ssp = _group(
        x_prompt, pos_p, jnp.zeros((bp, CONV_W - 1, CONV_DIM), x_prompt.dtype),
        jnp.zeros((bp, H_GDN, D_HEAD, D_HEAD), jnp.float32), True, _dsa_prompt, *wts)
    attend_s = functools.partial(_dsa_sample, cache_k=cache_k, cache_v=cache_v, cache_idx_k=cache_idx_k,
                                 page_table=page_table, layer=l)
    y_s, kss, vss, iss, css, sss = _group(
        x_sample, pos_s, state_conv[l], state_ssm[l], False, attend_s, *wts)
    st = lambda a: a[None]
    return (y_p, y_s, st(kp), st(vp), st(ip), st(cp), st(ssp), st(kss), st(vss), st(iss), st(css), st(sss))
```

```python
import functools
import math

import jax
import jax.numpy as jnp
import numpy as np
from jax import lax
from jax.experimental import pallas as pl
from jax.experimental.pallas import tpu as pltpu

D_MODEL = 4096
PAGE_SIZE = 128
D_HEAD = 128
H_GDN = 16
H_ATT = 16
D_GDN = H_GDN * D_HEAD
D_ATT = H_ATT * D_HEAD
CONV_W = 4
CONV_DIM = 3 * D_GDN
GDN_CHUNK = 64
H_IDX = 32
D_IDX = 128
TOPK_ATT_MAX = 256
Q_BLOCK = 128
ROPE_THETA = 10000.0
LN_EPS = 1e-5
RMS_EPS = 1e-6
PEER_HEADS = 8
PEER_NKEYS = 128
PEER_DQ = 256
PEER_TOPK = 16
PEER_BLOCK = 128
DEPTH = 1
DEEPNORM_ALPHA = (2.0 * DEPTH) ** 0.25
IN_SPLITS = (CONV_DIM, D_GDN, H_GDN, H_GDN, D_ATT, D_ATT, D_ATT, H_IDX * D_IDX, H_IDX, D_IDX, D_MODEL, D_MODEL)
IN_OFFSETS = tuple(int(o) for o in np.cumsum(IN_SPLITS)[:-1])

VMEM_LIMIT = 48 * 1024 * 1024


def _mm_kernel(x_ref, w_ref, o_ref):
    o_ref[...] = jnp.dot(x_ref[...], w_ref[...], preferred_element_type=jnp.float32)


def _pick(n, cands):
    for c in cands:
        if n % c == 0:
            return c
    return n


def _mm(x, w):
    m, k = x.shape
    n = w.shape[1]
    tm = _pick(m, (1024, 512, 256, 128))
    tn = _pick(n, (1024, 512, 256, 128))
    return pl.pallas_call(
        _mm_kernel,
        grid=(m // tm, n // tn),
        in_specs=[pl.BlockSpec((tm, k), lambda i, j: (i, 0)),
                  pl.BlockSpec((k, tn), lambda i, j: (0, j))],
        out_specs=pl.BlockSpec((tm, tn), lambda i, j: (i, j)),
        out_shape=jax.ShapeDtypeStruct((m, n), jnp.float32),
        compiler_params=pltpu.CompilerParams(
            dimension_semantics=("parallel", "parallel"),
            vmem_limit_bytes=VMEM_LIMIT),
        name="mm",
    )(x, w)


def _wcast(w):
    return jnp.pad(w.astype(jnp.bfloat16), ((0, 0), (0, (-w.shape[1]) % 128)))


def _in_proj_weights(w):
    names = ("qkv", "z", "b_raw", "a_raw", "q_att", "k_att", "v_att", "q_idx", "w_idx", "k_idx", "gate_gdn", "gate_att")
    bounds = (0,) + IN_OFFSETS + (sum(IN_SPLITS),)
    col = {n: w[:, bounds[i]:bounds[i + 1]].astype(jnp.bfloat16) for i, n in enumerate(names)}
    out = {n: col[n] for n in names if col[n].shape[1] % 128 == 0}
    out["small"] = _wcast(jnp.concatenate([col["b_raw"], col["a_raw"], col["w_idx"]], axis=1))
    return out


def _mmf(x, wb, n):
    shp = x.shape
    y = _mm(x.reshape(-1, shp[-1]).astype(jnp.bfloat16), wb)[:, :n]
    return y.reshape(*shp[:-1], n)


def _layernorm(x, g, b):
    mu = jnp.mean(x, -1, keepdims=True)
    var = jnp.mean(jnp.square(x - mu), -1, keepdims=True)
    return (x - mu) * lax.rsqrt(var + LN_EPS) * g + b


def _rope(x, pos):
    half = x.shape[-1] // 2
    inv = ROPE_THETA ** (-jnp.arange(half, dtype=jnp.float32) / half)
    ang = pos.astype(jnp.float32)[:, None] * inv[None, :]
    cos = jnp.cos(ang)[None, :, None, :]
    sin = jnp.sin(ang)[None, :, None, :]
    x1 = x[..., :half]
    x2 = x[..., half:]
    return jnp.concatenate([x1 * cos - x2 * sin, x2 * cos + x1 * sin], -1)


def _l2n(a):
    return a * lax.rsqrt(jnp.sum(a * a, -1, keepdims=True) + RMS_EPS)


def _causal_conv(u, buf, conv_w):
    t = u.shape[1]
    up = jnp.concatenate([buf.astype(u.dtype), u], axis=1)
    out = sum(up[:, j:j + t] * conv_w[j] for j in range(CONV_W))
    return jax.nn.silu(out), up[:, t:]


def _gdn_qkv(conv_out):
    b, t, _ = conv_out.shape
    q, k, v = (a.reshape(b, t, H_GDN, D_HEAD) for a in jnp.split(conv_out, 3, axis=-1))
    return _l2n(q) * D_HEAD ** -0.5, _l2n(k), v


def _gdn_gates(b_raw, a_raw, a_log, dt_bias):
    beta = jax.nn.sigmoid(b_raw)
    g = -jnp.exp(a_log) * jax.nn.softplus(a_raw + dt_bias)
    return g, beta


def _gdn_chunked(q, k, v, g, beta, s0):
    b, t, h, _ = q.shape
    c = GDN_CHUNK
    n = t // c

    def ch(a):
        return jnp.swapaxes(jnp.moveaxis(a.reshape(b, n, c, h, *a.shape[3:]), 1, 0), 2, 3)

    qc, kc, vc, gc, bc = ch(q), ch(k), ch(v), ch(g), ch(beta)
    gam = jnp.cumsum(gc, axis=-1)
    incl = jnp.tril(jnp.ones((c, c), bool))
    strict = jnp.tril(jnp.ones((c, c), bool), -1)
    decay = jnp.exp(jnp.where(incl, gam[..., :, None] - gam[..., None, :], -jnp.inf))
    kb = kc * bc[..., None]
    n_mat = jnp.where(strict, jnp.einsum('nbhid,nbhjd->nbhij', kb, kc) * decay, 0.0)
    t_inv = _unit_lower_inverse(n_mat.reshape(-1, c, c)).reshape(n_mat.shape)
    u = jnp.einsum('nbhij,nbhjd->nbhid', t_inv, vc * bc[..., None])
    w = jnp.einsum('nbhij,nbhjd->nbhid', t_inv, kb * jnp.exp(gam)[..., None])
    qk = jnp.einsum('nbhid,nbhjd->nbhij', qc, kc) * decay

    g_last = gam[..., -1:]
    qg = qc * jnp.exp(gam)[..., None]
    kg = kc * jnp.exp(g_last - gam)[..., None]
    decay_last = jnp.broadcast_to(jnp.exp(g_last)[..., None], (n, b, h, 1, D_HEAD))
    outs = [_gdn_scan(s0[i], *(a[:, i] for a in (u, w, qg, kg, qk, decay_last))) for i in range(b)]
    o = jnp.stack([x[0] for x in outs], axis=1)
    s = jnp.stack([x[1] for x in outs], axis=0)
    o = jnp.moveaxis(jnp.swapaxes(o, 2, 3), 0, 1).reshape(b, t, h, -1)
    return o, s


TRI_LANES = 128


def _tri_inv_kernel(n_ref, t_ref):
    c, _, lanes = n_ref.shape
    rowid = lax.broadcasted_iota(jnp.int32, (c, lanes), 0)
    t_ref[...] = jnp.zeros(t_ref.shape, t_ref.dtype)

    def row(i, carry):
        def term(j, acc):
            return acc - n_ref[i, pl.ds(j, 1), :] * t_ref[j]

        t_ref[i] = lax.fori_loop(0, c, term, jnp.where(rowid == i, 1.0, 0.0), unroll=8)
        return carry

    lax.fori_loop(0, c, row, 0)


def _unit_lower_inverse(nm):
    s, c, _ = nm.shape
    sp = -(-s // TRI_LANES) * TRI_LANES
    nt = jnp.pad(nm, ((0, sp - s), (0, 0), (0, 0))).transpose(1, 2, 0)
    spec = pl.BlockSpec((c, c, TRI_LANES), lambda g: (0, 0, g))
    t = pl.pallas_call(
        _tri_inv_kernel,
        grid=(sp // TRI_LANES,),
        in_specs=[spec],
        out_specs=spec,
        out_shape=jax.ShapeDtypeStruct((c, c, sp), jnp.float32),
        compiler_params=pltpu.CompilerParams(dimension_semantics=("parallel",), vmem_limit_bytes=VMEM_LIMIT),
        name="tri_inv",
    )(nt)
    return t.transpose(2, 0, 1)[:s]


def _gdn_scan_kernel(s0_ref, u_ref, w_ref, qg_ref, kg_ref, qk_ref, dl_ref, o_ref, s_ref):
    bf = jnp.bfloat16

    @pl.when(pl.program_id(0) == 0)
    def _():
        s_ref[...] = s0_ref[...]

    for hh in range(s_ref.shape[0]):
        s = s_ref[hh]
        sb = s.astype(bf)
        v_new = u_ref[0, hh] - jnp.dot(w_ref[0, hh].astype(bf), sb, preferred_element_type=jnp.float32)
        vb = v_new.astype(bf)
        o_ref[0, hh] = (jnp.dot(qg_ref[0, hh].astype(bf), sb, preferred_element_type=jnp.float32)
                        + jnp.dot(qk_ref[0, hh].astype(bf), vb, preferred_element_type=jnp.float32))
        s_ref[hh] = s * dl_ref[0, hh] + lax.dot_general(
            kg_ref[0, hh].astype(bf), vb, (((0,), (0,)), ((), ())), preferred_element_type=jnp.float32)


def _gdn_scan(s0, u, w, qg, kg, qk, decay_last):
    n, h, c, d = u.shape
    blk = lambda *shape: pl.BlockSpec((1,) + shape, lambda i: (i,) + (0,) * len(shape))
    state = pl.BlockSpec((h, d, d), lambda i: (0, 0, 0))
    o, s = pl.pallas_call(
        _gdn_scan_kernel,
        grid=(n,),
        in_specs=[state, blk(h, c, d), blk(h, c, d), blk(h, c, d), blk(h, c, d), blk(h, c, c), blk(h, 1, d)],
        out_specs=[blk(h, c, d), state],
        out_shape=[jax.ShapeDtypeStruct((n, h, c, d), jnp.float32), jax.ShapeDtypeStruct((h, d, d), jnp.float32)],
        compiler_params=pltpu.CompilerParams(dimension_semantics=("arbitrary",), vmem_limit_bytes=VMEM_LIMIT),
        name="gdn_scan",
    )(s0, u, w, qg, kg, qk, decay_last)
    return o, s


def _gdn_step_kernel(s0_ref, q_ref, k_ref, kb_ref, v_ref, dec_ref, o_ref, s_ref):
    nt = v_ref.shape[2]
    for hh in range(s0_ref.shape[1]):
        s = s0_ref[0, hh]
        for tt in range(nt):
            s = s * dec_ref[0, hh, tt:tt + 1, :]
            ks = jnp.sum(s * k_ref[0, hh, :, tt:tt + 1], axis=0, keepdims=True)
            s = s + kb_ref[0, hh, :, tt:tt + 1] * (v_ref[0, hh, tt:tt + 1, :] - ks)
            o_ref[0, hh, tt:tt + 1, :] = jnp.sum(s * q_ref[0, hh, :, tt:tt + 1], axis=0, keepdims=True)
        s_ref[0, hh] = s


def _gdn_recurrent(q, k, v, g, beta, s0):
    b, t, h, d = q.shape
    cols = lambda a: a.transpose(0, 2, 3, 1)
    rows = lambda a: a.transpose(0, 2, 1, 3)
    dec = jnp.broadcast_to(jnp.exp(g).transpose(0, 2, 1)[..., None], (b, h, t, d))
    cspec = pl.BlockSpec((1, h, d, t), lambda i: (i, 0, 0, 0))
    rspec = pl.BlockSpec((1, h, t, d), lambda i: (i, 0, 0, 0))
    sspec = pl.BlockSpec((1, h, d, d), lambda i: (i, 0, 0, 0))
    o, s = pl.pallas_call(
        _gdn_step_kernel,
        grid=(b,),
        in_specs=[sspec, cspec, cspec, cspec, rspec, rspec],
        out_specs=[rspec, sspec],
        out_shape=[jax.ShapeDtypeStruct((b, h, t, d), jnp.float32), jax.ShapeDtypeStruct((b, h, d, d), jnp.float32)],
        compiler_params=pltpu.CompilerParams(dimension_semantics=("parallel",), vmem_limit_bytes=VMEM_LIMIT),
        name="gdn_step",
    )(s0, cols(q), cols(k), cols(k * beta[..., None]), rows(v), dec)
    return o.transpose(0, 2, 1, 3), s


def _gdn_out(o, z, norm_w):
    b, t = z.shape[:2]
    o = o * lax.rsqrt(jnp.mean(o * o, -1, keepdims=True) + RMS_EPS) * norm_w
    return o.reshape(b, t, D_GDN) * jax.nn.silu(z)


def _att_heads(q_att, k_att, v_att, q_idx, w_idx, k_idx, pos):
    b, t, _ = q_att.shape
    qa = _rope(q_att.reshape(b, t, H_ATT, D_HEAD), pos)
    ka = _rope(k_att.reshape(b, t, H_ATT, D_HEAD), pos)
    va = v_att.reshape(b, t, H_ATT, D_HEAD)
    qi = _rope(q_idx.reshape(b, t, H_IDX, D_IDX), pos)
    ki = _rope(k_idx[:, :, None, :], pos)[:, :, 0]
    wi = w_idx * (H_IDX ** -0.5 * D_IDX ** -0.5)
    return qa, ka, va, qi, wi, ki


def _indexer_topk(qi, wi, ki, q_pos, n_sel):
    rel = jax.nn.relu(jnp.einsum('bthd,bsd->bths', qi, ki))
    score = jnp.einsum('bths,bth->bts', rel, wi).astype(jnp.float32)
    key_pos = jnp.arange(ki.shape[1])
    score = jnp.where(key_pos[None, None, :] <= q_pos[None, :, None], score, -jnp.inf)
    _, sel = lax.top_k(score, n_sel)
    return sel


def _sparse_attend(q, k_sel, v_sel, sel, q_pos):
    logits = jnp.einsum('bthd,btkhd->bthk', q, k_sel).astype(jnp.float32) * D_HEAD ** -0.5
    valid = (sel <= q_pos[None, :, None])[:, :, None, :]
    p = jax.nn.softmax(jnp.where(valid, logits, -jnp.inf), axis=-1)
    return jnp.einsum('bthk,btkhd->bthd', p.astype(v_sel.dtype), v_sel)


MXU_COLS = 256
DSA_TQ = 256
DSA_TK = 128
DSA_HG = max(1, MXU_COLS // DSA_TQ)
DSA_SUB = 4
DSA_TS = DSA_SUB * DSA_TK
INT32_MIN = np.int32(-2 ** 31)


def _float_to_key(x):
    bits = lax.bitcast_convert_type(x + 0.0, jnp.int32)
    return bits ^ ((bits >> 31) & jnp.int32(0x7FFFFFFF))


def _kth_largest_key(xs_ref, nch, ch, n_sel):
    lanes = xs_ref.shape[1]

    def count_ge(cand):
        def body(c, acc):
            r0 = pl.multiple_of(c * ch, ch)
            ind = jnp.where(xs_ref[pl.ds(r0, ch), :] >= cand, 1, 0).astype(jnp.int32)
            return acc + jnp.sum(ind.reshape(ch // 8, 8, lanes), axis=0)

        acc = lax.fori_loop(0, nch, body, jnp.zeros((8, lanes), jnp.int32))
        return jnp.sum(acc, axis=0, keepdims=True)

    def bit_cond(st):
        b, _, open_ = st
        return jnp.logical_and(b < 32, jnp.max(open_) > 0)

    def one_bit(b, thr_u, open_):
        cand_u = thr_u | lax.shift_left(jnp.int32(1), 31 - b)
        cnt = count_ge(cand_u ^ INT32_MIN)
        take = jnp.logical_and(cnt >= n_sel, open_ > 0)
        return jnp.where(take, cand_u, thr_u), jnp.where(cnt == n_sel, 0, open_)

    def bit_body(st):
        b, thr_u, open_ = st
        thr_u, open_ = one_bit(b, thr_u, open_)
        thr_u, open_ = one_bit(b + 1, thr_u, open_)
        return b + 2, thr_u, open_

    _, thr_u, _ = lax.while_loop(
        bit_cond, bit_body,
        (jnp.int32(0), jnp.zeros((1, lanes), jnp.int32), jnp.ones((1, lanes), jnp.int32)))
    return thr_u ^ INT32_MIN


def _dsa_prompt_kernel(ii_ref, jj_ref, qi_ref, w_ref, ki_ref, q_ref, k_ref, vt_ref, o_ref,
                       xs_ref, thr_ref, bias_ref, m_ref, l_ref, acc_ref, *, n_sel):
    s_id = pl.program_id(0)
    i = ii_ref[s_id]
    j = jj_ref[s_id]
    last = (i * DSA_TQ + DSA_TQ - 1) // DSA_TS
    nch = (last + 1) * DSA_SUB
    kiota = lax.broadcasted_iota(jnp.int32, (DSA_TK, DSA_TQ), 0)
    qpos = i * DSA_TQ + lax.broadcasted_iota(jnp.int32, (DSA_TK, DSA_TQ), 1)

    @pl.when(j == 0)
    def _index():
        def score_body(c, carry):
            r0 = pl.multiple_of(c * DSA_TK, DSA_TK)
            kchunk = ki_ref[pl.ds(r0, DSA_TK), :]
            sc = jnp.zeros((DSA_TK, DSA_TQ), jnp.float32)
            for hg in range(H_IDX // DSA_HG):
                z = jnp.dot(kchunk, qi_ref[0, hg], preferred_element_type=jnp.float32)
                for sub in range(DSA_HG):
                    hd = hg * DSA_HG + sub
                    sc = sc + w_ref[0, hd:hd + 1, :] * jnp.maximum(z[:, sub * DSA_TQ:(sub + 1) * DSA_TQ], 0.0)
            xs_ref[pl.ds(r0, DSA_TK), :] = jnp.where(r0 + kiota <= qpos, _float_to_key(sc), INT32_MIN)
            return carry

        lax.fori_loop(0, nch, score_body, 0)

        thr_ref[...] = _kth_largest_key(xs_ref, nch, DSA_TK, n_sel)
        m_ref[...] = jnp.full(m_ref.shape, -1e30, jnp.float32)
        l_ref[...] = jnp.zeros(l_ref.shape, jnp.float32)
        acc_ref[...] = jnp.zeros(acc_ref.shape, jnp.float32)

    for sub in range(DSA_SUB):
        r0 = pl.multiple_of(j * DSA_TS + sub * DSA_TK, DSA_TK)
        sel = jnp.where(xs_ref[pl.ds(r0, DSA_TK), :] >= thr_ref[...], r0 + kiota, jnp.int32(2 ** 30)) <= qpos
        bias_ref[sub * DSA_TK:(sub + 1) * DSA_TK, :] = jnp.where(sel, 0.0, -jnp.inf)
    for h in range(H_ATT):
        for sub in range(DSA_SUB):
            rows = slice(sub * DSA_TK, (sub + 1) * DSA_TK)
            s = jnp.dot(k_ref[rows, h * D_HEAD:(h + 1) * D_HEAD], q_ref[h], preferred_element_type=jnp.float32)
            s = s + bias_ref[rows, :]
            m_old = m_ref[h:h + 1, :]
            m_new = jnp.maximum(m_old, jnp.max(s, axis=0, keepdims=True))
            alpha = jnp.exp(m_old - m_new)
            p = jnp.exp(s - m_new)
            l_ref[h:h + 1, :] = alpha * l_ref[h:h + 1, :] + jnp.sum(p, axis=0, keepdims=True)
            m_ref[h:h + 1, :] = m_new
            acc_ref[h] = alpha * acc_ref[h] + jnp.dot(vt_ref[h, :, rows], p.astype(jnp.bfloat16),
                                                      preferred_element_type=jnp.float32)

    @pl.when(j == last)
    def _finish():
        for h in range(H_ATT):
            o_ref[:, h * D_HEAD:(h + 1) * D_HEAD] = (acc_ref[h] / l_ref[h:h + 1, :]).T


def _dsa_prompt_one(q, k, v, qi, wi, ki):
    t = q.shape[0]
    assert t % DSA_TS == 0 and t % DSA_TQ == 0
    nq = t // DSA_TQ
    n_sel = min(TOPK_ATT_MAX, t // 4)
    bf = jnp.bfloat16
    qit = qi.reshape(nq, DSA_TQ, H_IDX // DSA_HG, DSA_HG, D_IDX).transpose(0, 2, 4, 3, 1).reshape(
        nq, H_IDX // DSA_HG, D_IDX, DSA_HG * DSA_TQ).astype(bf)
    wr = wi.reshape(nq, DSA_TQ, H_IDX).transpose(0, 2, 1)
    qt = (q * D_HEAD ** -0.5).transpose(1, 2, 0).astype(bf)
    kf = k.reshape(t, D_ATT).astype(bf)
    vt = v.transpose(1, 2, 0).astype(bf)
    steps = [(i, j) for i in range(nq) for j in range((i * DSA_TQ + DSA_TQ - 1) // DSA_TS + 1)]
    ii = jnp.asarray([s[0] for s in steps], jnp.int32)
    jj = jnp.asarray([s[1] for s in steps], jnp.int32)
    grid_spec = pltpu.PrefetchScalarGridSpec(
        num_scalar_prefetch=2,
        grid=(len(steps),),
        in_specs=[
            pl.BlockSpec((1, H_IDX // DSA_HG, D_IDX, DSA_HG * DSA_TQ), lambda s, ii, jj: (ii[s], 0, 0, 0)),
            pl.BlockSpec((1, H_IDX, DSA_TQ), lambda s, ii, jj: (ii[s], 0, 0)),
            pl.BlockSpec((t, D_IDX), lambda s, ii, jj: (0, 0)),
            pl.BlockSpec((H_ATT, D_HEAD, DSA_TQ), lambda s, ii, jj: (0, 0, ii[s])),
            pl.BlockSpec((DSA_TS, D_ATT), lambda s, ii, jj: (jj[s], 0)),
            pl.BlockSpec((H_ATT, D_HEAD, DSA_TS), lambda s, ii, jj: (0, 0, jj[s])),
        ],
        out_specs=pl.BlockSpec((DSA_TQ, D_ATT), lambda s, ii, jj: (ii[s], 0)),
        scratch_shapes=[
            pltpu.VMEM((t, DSA_TQ), jnp.int32),
            pltpu.VMEM((1, DSA_TQ), jnp.int32),
            pltpu.VMEM((DSA_TS, DSA_TQ), jnp.float32),
            pltpu.VMEM((H_ATT, DSA_TQ), jnp.float32),
            pltpu.VMEM((H_ATT, DSA_TQ), jnp.float32),
            pltpu.VMEM((H_ATT, D_HEAD, DSA_TQ), jnp.float32),
        ])
    return pl.pallas_call(
        functools.partial(_dsa_prompt_kernel, n_sel=n_sel),
        grid_spec=grid_spec,
        out_shape=jax.ShapeDtypeStruct((t, D_ATT), jnp.float32),
        compiler_params=pltpu.CompilerParams(dimension_semantics=("arbitrary",), vmem_limit_bytes=VMEM_LIMIT),
        name="dsa_prompt",
    )(ii, jj, qit, wr, ki.astype(bf), qt, kf, vt)


def _dsa_prompt(q, k, v, qi, wi, ki):
    return jnp.stack([_dsa_prompt_one(q[b], k[b], v[b], qi[b], wi[b], ki[b]) for b in range(q.shape[0])])


SMP_IDX_PAGES = 8
SMP_ATT_PAGES = 8
SMP_CH = 128


def _smp_score_kernel(pt_ref, *refs):
    pages = refs[:SMP_IDX_PAGES]
    qi_ref, wb_ref, o_ref = refs[SMP_IDX_PAGES:]
    nq = o_ref.shape[1]
    for g, ki_ref in enumerate(pages):
        z = lax.dot_general(qi_ref[0], ki_ref[0, 0].astype(jnp.bfloat16), (((1,), (1,)), ((), ())),
                            preferred_element_type=jnp.float32)
        r = jnp.maximum(z, 0.0) * wb_ref[0]
        o_ref[0, :, g * PAGE_SIZE:(g + 1) * PAGE_SIZE] = jnp.sum(r.reshape(nq, H_IDX, PAGE_SIZE), axis=1)


def _smp_thr_kernel(s_ref, thr_ref, xs_ref, *, n_sel):
    nch = s_ref.shape[0] // SMP_CH

    def conv(c, carry):
        r0 = pl.multiple_of(c * SMP_CH, SMP_CH)
        xs_ref[pl.ds(r0, SMP_CH), :] = _float_to_key(s_ref[pl.ds(r0, SMP_CH), :])
        return carry

    lax.fori_loop(0, nch, conv, 0)
    key = _kth_largest_key(xs_ref, nch, SMP_CH, n_sel)
    thr_ref[...] = lax.bitcast_convert_type(key ^ ((key >> 31) & jnp.int32(0x7FFFFFFF)), jnp.float32)


def _smp_attn_kernel(pt_ref, *refs, nq):
    kp = refs[:SMP_ATT_PAGES]
    vp = refs[SMP_ATT_PAGES:2 * SMP_ATT_PAGES]
    (sc_ref, scn_ref, thr_ref, q_ref, kn_ref, vn_ref, ex_ref, hm_ref,
     o_ref, m_ref, l_ref, acc_ref) = refs[2 * SMP_ATT_PAGES:]
    pp = pl.program_id(1)
    rows = PAGE_SIZE * H_ATT

    def attend(k, v, sc):
        sel = jnp.where(sc >= thr_ref[0], 1.0, 0.0)
        sel = jnp.concatenate(
            [jnp.broadcast_to(sel[tt:tt + 1, :], (H_ATT, PAGE_SIZE)) for tt in range(nq)], axis=0)
        valid = jnp.dot(sel.astype(jnp.bfloat16), ex_ref[...], preferred_element_type=jnp.float32) * hm_ref[...]
        s = lax.dot_general(q_ref[0], k.reshape(rows, D_HEAD).astype(jnp.bfloat16), (((1,), (1,)), ((), ())),
                            preferred_element_type=jnp.float32)
        s = jnp.where(valid > 0.5, s, -jnp.inf)
        m = jnp.maximum(jnp.max(s, axis=1, keepdims=True), -1e30)
        p = jnp.exp(s - m)
        acc = jnp.dot(p.astype(jnp.bfloat16), v.reshape(rows, D_HEAD).astype(jnp.bfloat16),
                      preferred_element_type=jnp.float32)
        return m, jnp.sum(p, axis=1, keepdims=True), acc

    def merge(m, l, acc):
        m_old = m_ref[...]
        m_new = jnp.maximum(m_old, m)
        a_old = jnp.exp(m_old - m_new)
        a_new = jnp.exp(m - m_new)
        l_ref[...] = a_old * l_ref[...] + a_new * l
        acc_ref[...] = a_old * acc_ref[...] + a_new * acc
        m_ref[...] = m_new

    @pl.when(pp == 0)
    def _():
        m_ref[...] = jnp.full(m_ref.shape, -1e30, jnp.float32)
        l_ref[...] = jnp.zeros(l_ref.shape, jnp.float32)
        acc_ref[...] = jnp.zeros(acc_ref.shape, jnp.float32)
        merge(*attend(kn_ref[0], vn_ref[0], scn_ref[0]))

    parts = [attend(kp[g][0, 0], vp[g][0, 0], sc_ref[0, :, g * PAGE_SIZE:(g + 1) * PAGE_SIZE])
             for g in range(SMP_ATT_PAGES)]
    for part in parts:
        merge(*part)

    @pl.when(pp == pl.num_programs(1) - 1)
    def _():
        o_ref[0] = acc_ref[...] / l_ref[...]


def _dsa_sample(q, k_new, v_new, qi, wi, ki_new, cache_k, cache_v, cache_idx_k, page_table, layer):
    db, t = q.shape[:2]
    npg = page_table.shape[1]
    past_len = npg * PAGE_SIZE
    n_sel = min(TOPK_ATT_MAX, (past_len + t) // 4)
    assert npg % SMP_IDX_PAGES == 0 and npg % SMP_ATT_PAGES == 0 and (t * H_IDX) % 8 == 0
    bf = jnp.bfloat16
    pt = page_table.reshape(-1).astype(jnp.int32)
    n_pool = cache_k.shape[1]

    def page_spec(width, per_step, g):
        return pl.BlockSpec((1, 1, PAGE_SIZE, width),
                            lambda b, p, pt: (layer, pt[b * npg + p * per_step + g], 0, 0))

    row_spec = lambda rows, width: pl.BlockSpec((1, rows, width), lambda b, p, pt: (b, 0, 0))
    score_past = pl.pallas_call(
        _smp_score_kernel,
        grid_spec=pltpu.PrefetchScalarGridSpec(
            num_scalar_prefetch=1, grid=(db, npg // SMP_IDX_PAGES),
            in_specs=[page_spec(D_IDX, SMP_IDX_PAGES, g) for g in range(SMP_IDX_PAGES)]
            + [row_spec(t * H_IDX, D_IDX), row_spec(t * H_IDX, PAGE_SIZE)],
            out_specs=pl.BlockSpec((1, t, SMP_IDX_PAGES * PAGE_SIZE), lambda b, p, pt: (b, 0, p))),
        out_shape=jax.ShapeDtypeStruct((db, t, past_len), jnp.float32),
        compiler_params=pltpu.CompilerParams(dimension_semantics=("parallel", "arbitrary"),
                                             vmem_limit_bytes=VMEM_LIMIT),
        name="smp_score",
    )(pt, *([cache_idx_k] * SMP_IDX_PAGES), qi.reshape(db, t * H_IDX, D_IDX).astype(bf),
      jnp.broadcast_to(wi.reshape(db, t * H_IDX, 1), (db, t * H_IDX, PAGE_SIZE)))

    rel = jax.nn.relu(jnp.einsum('bthd,bsd->bths', qi, ki_new))
    score_new = jnp.einsum('bths,bth->bts', rel, wi)
    score_new = jnp.where(jnp.arange(t)[None, None, :] <= jnp.arange(t)[None, :, None], score_new, -jnp.inf)
    score_new = jnp.pad(score_new, ((0, 0), (0, 0), (0, PAGE_SIZE - t)), constant_values=-jnp.inf)

    score_t = jnp.concatenate([score_past, score_new], axis=2).reshape(db * t, past_len + PAGE_SIZE).T
    thr = pl.pallas_call(
        functools.partial(_smp_thr_kernel, n_sel=n_sel),
        out_shape=jax.ShapeDtypeStruct((1, db * t), jnp.float32),
        scratch_shapes=[pltpu.VMEM(score_t.shape, jnp.int32)],
        compiler_params=pltpu.CompilerParams(vmem_limit_bytes=VMEM_LIMIT),
        name="smp_thr",
    )(score_t)
    thr_b = jnp.broadcast_to(thr.reshape(db, t, 1), (db, t, PAGE_SIZE))

    nrow = t * H_ATT
    cols = PAGE_SIZE * H_ATT
    q_rows = (q * D_HEAD ** -0.5).reshape(db, nrow, D_HEAD).astype(bf)
    expand = jnp.repeat(jnp.eye(PAGE_SIZE, dtype=bf), H_ATT, axis=1)
    head_match = (jnp.arange(cols)[None, :] % H_ATT == jnp.arange(nrow)[:, None] % H_ATT).astype(jnp.float32)
    pad_page = lambda a: jnp.pad(a, ((0, 0), (0, PAGE_SIZE - t), (0, 0), (0, 0)))
    cache_spec = lambda g: pl.BlockSpec(
        (1, 1, PAGE_SIZE, H_ATT, D_HEAD),
        lambda b, p, pt: (layer, pt[b * npg + p * SMP_ATT_PAGES + g], 0, 0, 0))
    const_spec = lambda shape: pl.BlockSpec(shape, lambda b, p, pt: (0,) * len(shape))
    new_spec = pl.BlockSpec((1, PAGE_SIZE, H_ATT, D_HEAD), lambda b, p, pt: (b, 0, 0, 0))
    out = pl.pallas_call(
        functools.partial(_smp_attn_kernel, nq=t),
        grid_spec=pltpu.PrefetchScalarGridSpec(
            num_scalar_prefetch=1, grid=(db, npg // SMP_ATT_PAGES),
            in_specs=[cache_spec(g) for g in range(SMP_ATT_PAGES)] * 2
            + [pl.BlockSpec((1, t, SMP_ATT_PAGES * PAGE_SIZE), lambda b, p, pt: (b, 0, p)),
               row_spec(t, PAGE_SIZE), row_spec(t, PAGE_SIZE), row_spec(nrow, D_HEAD),
               new_spec, new_spec, const_spec((PAGE_SIZE, cols)), const_spec((nrow, cols))],
            out_specs=row_spec(nrow, D_HEAD),
            scratch_shapes=[pltpu.VMEM((nrow, 1), jnp.float32), pltpu.VMEM((nrow, 1), jnp.float32),
                            pltpu.VMEM((nrow, D_HEAD), jnp.float32)]),
        out_shape=jax.ShapeDtypeStruct((db, nrow, D_HEAD), jnp.float32),
        compiler_params=pltpu.CompilerParams(dimension_semantics=("parallel", "arbitrary"),
                                             vmem_limit_bytes=VMEM_LIMIT),
        name="smp_attn",
    )(pt, *([cache_k] * SMP_ATT_PAGES), *([cache_v] * SMP_ATT_PAGES),
      score_past, score_new, thr_b, q_rows, pad_page(k_new), pad_page(v_new), expand, head_match)
    return out.reshape(db, t, D_ATT)


def _peer(x, wq, k1, k2, u, v):
    shape = x.shape
    xf = x.reshape(-1, D_MODEL)
    n = xf.shape[0]
    pad = (-n) % PEER_BLOCK
    xf = jnp.pad(xf, ((0, pad), (0, 0)))
    xb16 = xf.astype(jnp.bfloat16)
    e1, e2, gate = _peer_route(_mm(xb16, wq), k1, k2)
    wmat = _peer_gate_matrix(e1, e2, gate)
    out = _peer_dense(xb16, u, v, wmat)
    return out[:n].reshape(shape)


PEER_RT = 128
PEER_RH = 2
PEER_TOPK_LOG2 = PEER_TOPK.bit_length() - 1
assert 1 << PEER_TOPK_LOG2 == PEER_TOPK
PEER_PAIRS = tuple((i, j) for i in range(PEER_TOPK) for j in range(PEER_TOPK) if (i + 1) * (j + 1) <= PEER_TOPK)


def _extract_top(s, ids, n):
    big = jnp.int32(2 ** 30)
    vals, sel = [], []
    for _ in range(n):
        m = jnp.max(s, axis=0, keepdims=True)
        pick = jnp.min(jnp.where(s == m, ids, big), axis=0, keepdims=True)
        vals.append(m)
        sel.append(pick)
        s = jnp.where(ids == pick, -jnp.inf, s)
    return vals, sel


def _peer_route_kernel(q_ref, k1_ref, k2_ref, e1_ref, e2_ref, g_ref):
    kio = lax.broadcasted_iota(jnp.int32, (PEER_NKEYS, PEER_RT), 0)
    flat = jnp.concatenate(
        [jnp.full((1, PEER_RT), i * PEER_TOPK + j, jnp.int32) for i, j in PEER_PAIRS], axis=0)
    for hh in range(PEER_RH):
        tops = []
        for side, kref in ((0, k1_ref), (1, k2_ref)):
            lo = (2 * hh + side) * (PEER_DQ // 2)
            qs = q_ref[:, lo:lo + PEER_DQ // 2].astype(jnp.bfloat16)
            s = lax.dot_general(kref[hh], qs, (((1,), (1,)), ((), ())),
                                preferred_element_type=jnp.float32)
            tops.append(_extract_top(s, kio, PEER_TOPK))
        (v1, i1), (v2, i2) = tops
        cand = jnp.concatenate([v1[i] + v2[j] for i, j in PEER_PAIRS], axis=0)
        top_s, top_f = _extract_top(cand, flat, PEER_TOPK)
        ts = jnp.concatenate(top_s, axis=0)
        tf = jnp.concatenate(top_f, axis=0)
        ex = jnp.exp(ts - top_s[0])
        g_ref[hh] = ex * (1.0 / jnp.sum(ex, axis=0, keepdims=True))
        fi = tf >> PEER_TOPK_LOG2
        fj = tf & (PEER_TOPK - 1)
        e1_ref[hh] = sum(jnp.where(fi == i, i1[i], 0) for i in range(PEER_TOPK))
        e2_ref[hh] = sum(jnp.where(fj == j, i2[j], 0) for j in range(PEER_TOPK))


def _peer_route(q, k1, k2):
    n = q.shape[0]
    assert n % PEER_RT == 0 and PEER_HEADS % PEER_RH == 0
    kspec = pl.BlockSpec((PEER_RH, PEER_NKEYS, PEER_DQ // 2), lambda i, h: (h, 0, 0))
    ospec = pl.BlockSpec((PEER_RH, PEER_TOPK, PEER_RT), lambda i, h: (h, 0, i))
    oshape = lambda dt: jax.ShapeDtypeStruct((PEER_HEADS, PEER_TOPK, n), dt)
    e1, e2, g = pl.pallas_call(
        _peer_route_kernel,
        grid=(n // PEER_RT, PEER_HEADS // PEER_RH),
        in_specs=[pl.BlockSpec((PEER_RT, PEER_RH * PEER_DQ), lambda i, h: (i, h)), kspec, kspec],
        out_specs=[ospec, ospec, ospec],
        out_shape=[oshape(jnp.int32), oshape(jnp.int32), oshape(jnp.float32)],
        compiler_params=pltpu.CompilerParams(dimension_semantics=("parallel", "parallel"),
                                             vmem_limit_bytes=VMEM_LIMIT),
        name="peer_route",
    )(q, k1.astype(jnp.bfloat16), k2.astype(jnp.bfloat16))
    tok_major = lambda a: a.reshape(PEER_HEADS * PEER_TOPK, n).T
    return tok_major(e1), tok_major(e2), tok_major(g)


PEER_GT = 16


def _peer_gate_kernel(i1_ref, i2_ref, g_ref, w_ref, tmp_ref):
    iota = lax.broadcasted_iota(jnp.int32, (PEER_NKEYS, PEER_HEADS * PEER_TOPK), 0)

    def body(gi, carry):
        t0 = pl.multiple_of(gi * PEER_GT, PEER_GT)
        for kk in range(PEER_GT):
            a = jnp.where(iota == i1_ref[pl.ds(t0 + kk, 1), :], g_ref[pl.ds(t0 + kk, 1), :], 0.0).astype(jnp.bfloat16)
            b = jnp.where(iota == i2_ref[pl.ds(t0 + kk, 1), :], 1.0, 0.0).astype(jnp.bfloat16)
            tmp_ref[kk * PEER_NKEYS:(kk + 1) * PEER_NKEYS, :] = lax.dot_general(
                a, b, (((1,), (1,)), ((), ())), preferred_element_type=jnp.float32)
        for aa in range(PEER_NKEYS):
            rows = tmp_ref[pl.ds(aa, PEER_GT, stride=PEER_NKEYS), :]
            w_ref[pl.ds(t0, PEER_GT), aa * PEER_NKEYS:(aa + 1) * PEER_NKEYS] = rows.astype(w_ref.dtype)
        return carry

    lax.fori_loop(0, i1_ref.shape[0] // PEER_GT, body, 0)


def _peer_gate_matrix(e1, e2, gate):
    n, p = e1.shape
    tw = _pick(n, (128,))
    assert tw % PEER_GT == 0
    spec = pl.BlockSpec((tw, p), lambda i: (i, 0))
    return pl.pallas_call(
        _peer_gate_kernel,
        grid=(n // tw,),
        in_specs=[spec, spec, spec],
        out_specs=pl.BlockSpec((tw, PEER_NKEYS * PEER_NKEYS), lambda i: (i, 0)),
        out_shape=jax.ShapeDtypeStruct((n, PEER_NKEYS * PEER_NKEYS), jnp.bfloat16),
        scratch_shapes=[pltpu.VMEM((PEER_GT * PEER_NKEYS, PEER_NKEYS), jnp.float32)],
        compiler_params=pltpu.CompilerParams(dimension_semantics=("parallel",), vmem_limit_bytes=VMEM_LIMIT),
        name="peer_gate",
    )(e1, e2, gate)


def _peer_dense_kernel(h_ref, ut_ref, w_ref, v_ref, o_ref):
    @pl.when(pl.program_id(1) == 0)
    def _():
        o_ref[...] = jnp.zeros_like(o_ref)

    half = ut_ref.shape[1] // 2
    upd = None
    for s in range(2):
        z = jnp.dot(h_ref[...], ut_ref[:, s * half:(s + 1) * half], preferred_element_type=jnp.float32)
        act = 0.5 * z * (1.0 + lax.erf(z * (2.0 ** -0.5)))
        c = (w_ref[:, s * half:(s + 1) * half].astype(jnp.float32) * act).astype(jnp.bfloat16)
        d = jnp.dot(c, v_ref[s * half:(s + 1) * half, :], preferred_element_type=jnp.float32)
        upd = d if upd is None else upd + d
    o_ref[...] += upd


def _peer_dense(xb16, ut, vb, wmat):
    n = xb16.shape[0]
    e = vb.shape[0]
    tm = _pick(n, (512, 256, 128))
    te = 512
    return pl.pallas_call(
        _peer_dense_kernel,
        grid=(n // tm, e // te),
        in_specs=[pl.BlockSpec((tm, D_MODEL), lambda i, j: (i, 0)),
                  pl.BlockSpec((D_MODEL, te), lambda i, j: (0, j)),
                  pl.BlockSpec((tm, te), lambda i, j: (i, j)),
                  pl.BlockSpec((te, D_MODEL), lambda i, j: (j, 0))],
        out_specs=pl.BlockSpec((tm, D_MODEL), lambda i, j: (i, 0)),
        out_shape=jax.ShapeDtypeStruct((n, D_MODEL), jnp.float32),
        compiler_params=pltpu.CompilerParams(dimension_semantics=("parallel", "arbitrary"),
                                             vmem_limit_bytes=56 * 1024 * 1024),
        name="peer_dense",
    )(xb16, ut, wmat, vb)


def _group(x, pos, conv_buf, ssm0, chunked, attend, w_in, conv_w, a_log, dt_bias, gdn_norm_w,
           w_br_gdn, w_br_att, w_out, ln1_g, ln1_b, peer_wq, peer_k1, peer_k2, peer_u, peer_v, ln2_g, ln2_b):
    xb = x.reshape(-1, D_MODEL).astype(jnp.bfloat16)
    proj = lambda name: _mm(xb, w_in[name]).reshape(*x.shape[:-1], -1)
    qkv, z, q_att, k_att, v_att, q_idx, k_idx, gate_gdn, gate_att = (
        proj(n) for n in ("qkv", "z", "q_att", "k_att", "v_att", "q_idx", "k_idx", "gate_gdn", "gate_att"))
    small = proj("small")
    b_raw, a_raw, w_idx = small[..., :H_GDN], small[..., H_GDN:2 * H_GDN], small[..., 2 * H_GDN:2 * H_GDN + H_IDX]
    conv_out, conv_new = _causal_conv(qkv, conv_buf, conv_w)
    q, k, v = _gdn_qkv(conv_out)
    g, beta = _gdn_gates(b_raw, a_raw, a_log, dt_bias)
    gdn = _gdn_chunked if chunked else _gdn_recurrent
    o, ssm_new = gdn(q, k, v, g, beta, ssm0.astype(jnp.float32))
    o_gdn = _gdn_out(o, z, gdn_norm_w)
    qa, ka, va, qi, wi, ki = _att_heads(q_att, k_att, v_att, q_idx, w_idx, k_idx, pos)
    o_att = attend(qa, ka, va, qi, wi, ki)
    mix = _mmf(jax.nn.sigmoid(gate_gdn) * _mmf(o_gdn, w_br_gdn, D_MODEL) + jax.nn.sigmoid(gate_att) * _mmf(o_att, w_br_att, D_MODEL),
               w_out, D_MODEL)
    h = _layernorm(DEEPNORM_ALPHA * x + mix, ln1_g, ln1_b)
    y = _layernorm(DEEPNORM_ALPHA * h + _peer(h, peer_wq, peer_k1, peer_k2, peer_u, peer_v), ln2_g, ln2_b)
    return y, ka, va, ki, conv_new, ssm_new.astype(x.dtype)


def kernel(x_prompt, x_sample, cache_k, cache_v, cache_idx_k, state_conv, state_ssm, page_table, w_in, conv_w, a_log, dt_bias, gdn_norm_w, w_br_gdn, w_br_att, w_out, ln1_g, ln1_b, peer_wq, peer_k1, peer_k2, peer_u, peer_v, ln2_g, ln2_b):
    bp, sp = x_prompt.shape[:2]
    ds = x_sample.shape[1]
    past_len = page_table.shape[1] * PAGE_SIZE
    pos_p = jnp.arange(sp)
    pos_s = past_len + jnp.arange(ds)
    l = 0
    wts = (_in_proj_weights(w_in[l]), conv_w[l], a_log[l], dt_bias[l], gdn_norm_w[l], _wcast(w_br_gdn[l]), _wcast(w_br_att[l]),
           _wcast(w_out[l]), ln1_g[l], ln1_b[l], _wcast(peer_wq[l]), peer_k1[l], peer_k2[l],
           peer_u[l].astype(jnp.bfloat16).T, peer_v[l].astype(jnp.bfloat16), ln2_g[l], ln2_b[l])
    y_p, kp, vp, ip, cp, ssp = _group(
        x_prompt, pos_p, jnp.zeros((bp, CONV_W - 1, CONV_DIM), x_prompt.dtype),
        jnp.zeros((bp, H_GDN, D_HEAD, D_HEAD), jnp.float32), True, _dsa_prompt, *wts)
    attend_s = functools.partial(_dsa_sample, cache_k=cache_k, cache_v=cache_v, cache_idx_k=cache_idx_k,
                                 page_table=page_table, layer=l)
    y_s, kss, vss, iss, css, sss = _group(
        x_sample, pos_s, state_conv[l], state_ssm[l], False, attend_s, *wts)
    st = lambda a: a[None]
    return (y_p, y_s, st(kp), st(vp), st(ip), st(cp), st(ssp), st(kss), st(vss), st(iss), st(css), st(sss))
```

```python
import functools

import jax
import jax.numpy as jnp
import numpy as np
from jax import lax
from jax.experimental import pallas as pl
from jax.experimental.pallas import tpu as pltpu

D_MODEL = 4096
PAGE_SIZE = 128
D_HEAD = 128
H_GDN = 16
H_ATT = 16
D_GDN = H_GDN * D_HEAD
D_ATT = H_ATT * D_HEAD
CONV_W = 4
CONV_DIM = 3 * D_GDN
GDN_CHUNK = 64
H_IDX = 32
D_IDX = 128
TOPK_ATT_MAX = 256
ROPE_THETA = 10000.0
LN_EPS = 1e-5
RMS_EPS = 1e-6
PEER_HEADS = 8
PEER_NKEYS = 128
PEER_DQ = 256
PEER_TOPK = 16
PEER_BLOCK = 128
DEPTH = 1
DEEPNORM_ALPHA = (2.0 * DEPTH) ** 0.25
IN_SPLITS = (CONV_DIM, D_GDN, H_GDN, H_GDN, D_ATT, D_ATT, D_ATT, H_IDX * D_IDX, H_IDX, D_IDX, D_MODEL, D_MODEL)
IN_OFFSETS = tuple(int(o) for o in np.cumsum(IN_SPLITS)[:-1])

VMEM_LIMIT = 48 * 1024 * 1024


def _mm_kernel(x_ref, w_ref, o_ref):
    o_ref[...] = jnp.dot(x_ref[...], w_ref[...], preferred_element_type=jnp.float32)


def _pick(n, cands):
    for c in cands:
        if n % c == 0:
            return c
    return n


def _mm(x, w):
    m, k = x.shape
    n = w.shape[1]
    tm = _pick(m, (1024, 512, 256, 128))
    tn = _pick(n, (1024, 512, 256, 128))
    return pl.pallas_call(
        _mm_kernel,
        grid=(m // tm, n // tn),
        in_specs=[pl.BlockSpec((tm, k), lambda i, j: (i, 0)),
                  pl.BlockSpec((k, tn), lambda i, j: (0, j))],
        out_specs=pl.BlockSpec((tm, tn), lambda i, j: (i, j)),
        out_shape=jax.ShapeDtypeStruct((m, n), jnp.float32),
        compiler_params=pltpu.CompilerParams(
            dimension_semantics=("parallel", "parallel"),
            vmem_limit_bytes=VMEM_LIMIT),
        name="mm",
    )(x, w)


def _wcast(w):
    return jnp.pad(w.astype(jnp.bfloat16), ((0, 0), (0, (-w.shape[1]) % 128)))


def _in_proj_weights(w):
    names = ("qkv", "z", "b_raw", "a_raw", "q_att", "k_att", "v_att", "q_idx", "w_idx", "k_idx", "gate_gdn", "gate_att")
    bounds = (0,) + IN_OFFSETS + (sum(IN_SPLITS),)
    col = {n: w[:, bounds[i]:bounds[i + 1]].astype(jnp.bfloat16) for i, n in enumerate(names)}
    out = {n: col[n] for n in names if col[n].shape[1] % 128 == 0}
    out["small"] = _wcast(jnp.concatenate([col["b_raw"], col["a_raw"], col["w_idx"]], axis=1))
    return out


def _mmf(x, wb, n):
    shp = x.shape
    y = _mm(x.reshape(-1, shp[-1]).astype(jnp.bfloat16), wb)[:, :n]
    return y.reshape(*shp[:-1], n)


def _layernorm(x, g, b):
    mu = jnp.mean(x, -1, keepdims=True)
    var = jnp.mean(jnp.square(x - mu), -1, keepdims=True)
    return (x - mu) * lax.rsqrt(var + LN_EPS) * g + b


def _rope(x, pos):
    half = x.shape[-1] // 2
    inv = ROPE_THETA ** (-jnp.arange(half, dtype=jnp.float32) / half)
    ang = pos.astype(jnp.float32)[:, None] * inv[None, :]
    cos = jnp.cos(ang)[None, :, None, :]
    sin = jnp.sin(ang)[None, :, None, :]
    x1 = x[..., :half]
    x2 = x[..., half:]
    return jnp.concatenate([x1 * cos - x2 * sin, x2 * cos + x1 * sin], -1)


def _l2n(a):
    return a * lax.rsqrt(jnp.sum(a * a, -1, keepdims=True) + RMS_EPS)


def _causal_conv(u, buf, conv_w):
    t = u.shape[1]
    up = jnp.concatenate([buf.astype(u.dtype), u], axis=1)
    out = sum(up[:, j:j + t] * conv_w[j] for j in range(CONV_W))
    return jax.nn.silu(out), up[:, t:]


def _gdn_qkv(conv_out):
    b, t, _ = conv_out.shape
    q, k, v = (a.reshape(b, t, H_GDN, D_HEAD) for a in jnp.split(conv_out, 3, axis=-1))
    return _l2n(q) * D_HEAD ** -0.5, _l2n(k), v


def _gdn_gates(b_raw, a_raw, a_log, dt_bias):
    beta = jax.nn.sigmoid(b_raw)
    g = -jnp.exp(a_log) * jax.nn.softplus(a_raw + dt_bias)
    return g, beta


def _gdn_chunked(q, k, v, g, beta, s0):
    b, t, h, _ = q.shape
    c = GDN_CHUNK
    n = t // c

    def ch(a):
        return jnp.swapaxes(jnp.moveaxis(a.reshape(b, n, c, h, *a.shape[3:]), 1, 0), 2, 3)

    qc, kc, vc, gc, bc = ch(q), ch(k), ch(v), ch(g), ch(beta)
    gam = jnp.cumsum(gc, axis=-1)
    incl = jnp.tril(jnp.ones((c, c), bool))
    strict = jnp.tril(jnp.ones((c, c), bool), -1)
    decay = jnp.exp(jnp.where(incl, gam[..., :, None] - gam[..., None, :], -jnp.inf))
    kb = kc * bc[..., None]
    n_mat = jnp.where(strict, jnp.einsum('nbhid,nbhjd->nbhij', kb, kc) * decay, 0.0)
    t_inv = _unit_lower_inverse(n_mat.reshape(-1, c, c)).reshape(n_mat.shape)
    u = jnp.einsum('nbhij,nbhjd->nbhid', t_inv, vc * bc[..., None])
    w = jnp.einsum('nbhij,nbhjd->nbhid', t_inv, kb * jnp.exp(gam)[..., None])
    qk = jnp.einsum('nbhid,nbhjd->nbhij', qc, kc) * decay

    g_last = gam[..., -1:]
    qg = qc * jnp.exp(gam)[..., None]
    kg = kc * jnp.exp(g_last - gam)[..., None]
    decay_last = jnp.broadcast_to(jnp.exp(g_last)[..., None], (n, b, h, 1, D_HEAD))
    outs = [_gdn_scan(s0[i], *(a[:, i] for a in (u, w, qg, kg, qk, decay_last))) for i in range(b)]
    o = jnp.stack([x[0] for x in outs], axis=1)
    s = jnp.stack([x[1] for x in outs], axis=0)
    o = jnp.moveaxis(jnp.swapaxes(o, 2, 3), 0, 1).reshape(b, t, h, -1)
    return o, s


TRI_LANES = 128


def _tri_inv_kernel(n_ref, t_ref):
    c, _, lanes = n_ref.shape
    rowid = lax.broadcasted_iota(jnp.int32, (c, lanes), 0)
    t_ref[...] = jnp.zeros(t_ref.shape, t_ref.dtype)

    def row(i, carry):
        def term(j, acc):
            return acc - n_ref[i, pl.ds(j, 1), :] * t_ref[j]

        t_ref[i] = lax.fori_loop(0, c, term, jnp.where(rowid == i, 1.0, 0.0), unroll=8)
        return carry

    lax.fori_loop(0, c, row, 0)


def _unit_lower_inverse(nm):
    s, c, _ = nm.shape
    sp = -(-s // TRI_LANES) * TRI_LANES
    nt = jnp.pad(nm, ((0, sp - s), (0, 0), (0, 0))).transpose(1, 2, 0)
    spec = pl.BlockSpec((c, c, TRI_LANES), lambda g: (0, 0, g))
    t = pl.pallas_call(
        _tri_inv_kernel,
        grid=(sp // TRI_LANES,),
        in_specs=[spec],
        out_specs=spec,
        out_shape=jax.ShapeDtypeStruct((c, c, sp), jnp.float32),
        compiler_params=pltpu.CompilerParams(dimension_semantics=("parallel",), vmem_limit_bytes=VMEM_LIMIT),
        name="tri_inv",
    )(nt)
    return t.transpose(2, 0, 1)[:s]


def _gdn_scan_kernel(s0_ref, u_ref, w_ref, qg_ref, kg_ref, qk_ref, dl_ref, o_ref, s_ref):
    bf = jnp.bfloat16

    @pl.when(pl.program_id(0) == 0)
    def _():
        s_ref[...] = s0_ref[...]

    for hh in range(s_ref.shape[0]):
        s = s_ref[hh]
        sb = s.astype(bf)
        v_new = u_ref[0, hh] - jnp.dot(w_ref[0, hh].astype(bf), sb, preferred_element_type=jnp.float32)
        vb = v_new.astype(bf)
        o_ref[0, hh] = (jnp.dot(qg_ref[0, hh].astype(bf), sb, preferred_element_type=jnp.float32)
                        + jnp.dot(qk_ref[0, hh].astype(bf), vb, preferred_element_type=jnp.float32))
        s_ref[hh] = s * dl_ref[0, hh] + lax.dot_general(
            kg_ref[0, hh].astype(bf), vb, (((0,), (0,)), ((), ())), preferred_element_type=jnp.float32)


def _gdn_scan(s0, u, w, qg, kg, qk, decay_last):
    n, h, c, d = u.shape
    blk = lambda *shape: pl.BlockSpec((1,) + shape, lambda i: (i,) + (0,) * len(shape))
    state = pl.BlockSpec((h, d, d), lambda i: (0, 0, 0))
    o, s = pl.pallas_call(
        _gdn_scan_kernel,
        grid=(n,),
        in_specs=[state, blk(h, c, d), blk(h, c, d), blk(h, c, d), blk(h, c, d), blk(h, c, c), blk(h, 1, d)],
        out_specs=[blk(h, c, d), state],
        out_shape=[jax.ShapeDtypeStruct((n, h, c, d), jnp.float32), jax.ShapeDtypeStruct((h, d, d), jnp.float32)],
        compiler_params=pltpu.CompilerParams(dimension_semantics=("arbitrary",), vmem_limit_bytes=VMEM_LIMIT),
        name="gdn_scan",
    )(s0, u, w, qg, kg, qk, decay_last)
    return o, s


def _gdn_step_kernel(s0_ref, q_ref, k_ref, kb_ref, v_ref, dec_ref, o_ref, s_ref):
    nt = v_ref.shape[2]
    for hh in range(s0_ref.shape[1]):
        s = s0_ref[0, hh]
        for tt in range(nt):
            s = s * dec_ref[0, hh, tt:tt + 1, :]
            ks = jnp.sum(s * k_ref[0, hh, :, tt:tt + 1], axis=0, keepdims=True)
            s = s + kb_ref[0, hh, :, tt:tt + 1] * (v_ref[0, hh, tt:tt + 1, :] - ks)
            o_ref[0, hh, tt:tt + 1, :] = jnp.sum(s * q_ref[0, hh, :, tt:tt + 1], axis=0, keepdims=True)
        s_ref[0, hh] = s


def _gdn_recurrent(q, k, v, g, beta, s0):
    b, t, h, d = q.shape
    cols = lambda a: a.transpose(0, 2, 3, 1)
    rows = lambda a: a.transpose(0, 2, 1, 3)
    dec = jnp.broadcast_to(jnp.exp(g).transpose(0, 2, 1)[..., None], (b, h, t, d))
    cspec = pl.BlockSpec((1, h, d, t), lambda i: (i, 0, 0, 0))
    rspec = pl.BlockSpec((1, h, t, d), lambda i: (i, 0, 0, 0))
    sspec = pl.BlockSpec((1, h, d, d), lambda i: (i, 0, 0, 0))
    o, s = pl.pallas_call(
        _gdn_step_kernel,
        grid=(b,),
        in_specs=[sspec, cspec, cspec, cspec, rspec, rspec],
        out_specs=[rspec, sspec],
        out_shape=[jax.ShapeDtypeStruct((b, h, t, d), jnp.float32), jax.ShapeDtypeStruct((b, h, d, d), jnp.float32)],
        compiler_params=pltpu.CompilerParams(dimension_semantics=("parallel",), vmem_limit_bytes=VMEM_LIMIT),
        name="gdn_step",
    )(s0, cols(q), cols(k), cols(k * beta[..., None]), rows(v), dec)
    return o.transpose(0, 2, 1, 3), s


def _gdn_out(o, z, norm_w):
    b, t = z.shape[:2]
    o = o * lax.rsqrt(jnp.mean(o * o, -1, keepdims=True) + RMS_EPS) * norm_w
    return o.reshape(b, t, D_GDN) * jax.nn.silu(z)


def _att_heads(q_att, k_att, v_att, q_idx, w_idx, k_idx, pos):
    b, t, _ = q_att.shape
    qa = _rope(q_att.reshape(b, t, H_ATT, D_HEAD), pos)
    ka = _rope(k_att.reshape(b, t, H_ATT, D_HEAD), pos)
    va = v_att.reshape(b, t, H_ATT, D_HEAD)
    qi = _rope(q_idx.reshape(b, t, H_IDX, D_IDX), pos)
    ki = _rope(k_idx[:, :, None, :], pos)[:, :, 0]
    wi = w_idx * (H_IDX ** -0.5 * D_IDX ** -0.5)
    return qa, ka, va, qi, wi, ki


MXU_COLS = 256
DSA_TQ = 256
DSA_TK = 128
DSA_HG = max(1, MXU_COLS // DSA_TQ)
DSA_SUB = 4
DSA_TS = DSA_SUB * DSA_TK
INT32_MIN = np.int32(-2 ** 31)


def _float_to_key(x):
    bits = lax.bitcast_convert_type(x + 0.0, jnp.int32)
    return bits ^ ((bits >> 31) & jnp.int32(0x7FFFFFFF))


def _kth_largest_key(xs_ref, nch, ch, n_sel):
    lanes = xs_ref.shape[1]

    def count_ge(cand):
        def body(c, acc):
            r0 = pl.multiple_of(c * ch, ch)
            ind = jnp.where(xs_ref[pl.ds(r0, ch), :] >= cand, 1, 0).astype(jnp.int32)
            return acc + jnp.sum(ind.reshape(ch // 8, 8, lanes), axis=0)

        acc = lax.fori_loop(0, nch, body, jnp.zeros((8, lanes), jnp.int32))
        return jnp.sum(acc, axis=0, keepdims=True)

    def bit_cond(st):
        b, _, open_ = st
        return jnp.logical_and(b < 32, jnp.max(open_) > 0)

    def one_bit(b, thr_u, open_):
        cand_u = thr_u | lax.shift_left(jnp.int32(1), 31 - b)
        cnt = count_ge(cand_u ^ INT32_MIN)
        take = jnp.logical_and(cnt >= n_sel, open_ > 0)
        return jnp.where(take, cand_u, thr_u), jnp.where(cnt == n_sel, 0, open_)

    def bit_body(st):
        b, thr_u, open_ = st
        thr_u, open_ = one_bit(b, thr_u, open_)
        thr_u, open_ = one_bit(b + 1, thr_u, open_)
        return b + 2, thr_u, open_

    _, thr_u, _ = lax.while_loop(
        bit_cond, bit_body,
        (jnp.int32(0), jnp.zeros((1, lanes), jnp.int32), jnp.ones((1, lanes), jnp.int32)))
    return thr_u ^ INT32_MIN


def _dsa_prompt_kernel(ii_ref, jj_ref, qi_ref, w_ref, ki_ref, q_ref, k_ref, vt_ref, o_ref,
                       xs_ref, thr_ref, bias_ref, m_ref, l_ref, acc_ref, *, n_sel):
    s_id = pl.program_id(0)
    i = ii_ref[s_id]
    j = jj_ref[s_id]
    last = (i * DSA_TQ + DSA_TQ - 1) // DSA_TS
    nch = (last + 1) * DSA_SUB
    kiota = lax.broadcasted_iota(jnp.int32, (DSA_TK, DSA_TQ), 0)
    qpos = i * DSA_TQ + lax.broadcasted_iota(jnp.int32, (DSA_TK, DSA_TQ), 1)

    @pl.when(j == 0)
    def _index():
        def score_body(c, carry):
            r0 = pl.multiple_of(c * DSA_TK, DSA_TK)
            kchunk = ki_ref[pl.ds(r0, DSA_TK), :]
            sc = jnp.zeros((DSA_TK, DSA_TQ), jnp.float32)
            for hg in range(H_IDX // DSA_HG):
                z = jnp.dot(kchunk, qi_ref[0, hg], preferred_element_type=jnp.float32)
                for sub in range(DSA_HG):
                    hd = hg * DSA_HG + sub
                    sc = sc + w_ref[0, hd:hd + 1, :] * jnp.maximum(z[:, sub * DSA_TQ:(sub + 1) * DSA_TQ], 0.0)
            xs_ref[pl.ds(r0, DSA_TK), :] = jnp.where(r0 + kiota <= qpos, _float_to_key(sc), INT32_MIN)
            return carry

        lax.fori_loop(0, nch, score_body, 0)

        thr_ref[...] = _kth_largest_key(xs_ref, nch, DSA_TK, n_sel)
        m_ref[...] = jnp.full(m_ref.shape, -1e30, jnp.float32)
        l_ref[...] = jnp.zeros(l_ref.shape, jnp.float32)
        acc_ref[...] = jnp.zeros(acc_ref.shape, jnp.float32)

    for sub in range(DSA_SUB):
        r0 = pl.multiple_of(j * DSA_TS + sub * DSA_TK, DSA_TK)
        sel = jnp.where(xs_ref[pl.ds(r0, DSA_TK), :] >= thr_ref[...], r0 + kiota, jnp.int32(2 ** 30)) <= qpos
        bias_ref[sub * DSA_TK:(sub + 1) * DSA_TK, :] = jnp.where(sel, 0.0, -jnp.inf)
    for h in range(H_ATT):
        for sub in range(DSA_SUB):
            rows = slice(sub * DSA_TK, (sub + 1) * DSA_TK)
            s = jnp.dot(k_ref[rows, h * D_HEAD:(h + 1) * D_HEAD], q_ref[h], preferred_element_type=jnp.float32)
            s = s + bias_ref[rows, :]
            m_old = m_ref[h:h + 1, :]
            m_new = jnp.maximum(m_old, jnp.max(s, axis=0, keepdims=True))
            alpha = jnp.exp(m_old - m_new)
            p = jnp.exp(s - m_new)
            l_ref[h:h + 1, :] = alpha * l_ref[h:h + 1, :] + jnp.sum(p, axis=0, keepdims=True)
            m_ref[h:h + 1, :] = m_new
            acc_ref[h] = alpha * acc_ref[h] + jnp.dot(vt_ref[h, :, rows], p.astype(jnp.bfloat16),
                                                      preferred_element_type=jnp.float32)

    @pl.when(j == last)
    def _finish():
        for h in range(H_ATT):
            o_ref[:, h * D_HEAD:(h + 1) * D_HEAD] = (acc_ref[h] / l_ref[h:h + 1, :]).T


def _dsa_prompt_one(q, k, v, qi, wi, ki):
    t = q.shape[0]
    assert t % DSA_TS == 0 and t % DSA_TQ == 0
    nq = t // DSA_TQ
    n_sel = min(TOPK_ATT_MAX, t // 4)
    bf = jnp.bfloat16
    qit = qi.reshape(nq, DSA_TQ, H_IDX // DSA_HG, DSA_HG, D_IDX).transpose(0, 2, 4, 3, 1).reshape(
        nq, H_IDX // DSA_HG, D_IDX, DSA_HG * DSA_TQ).astype(bf)
    wr = wi.reshape(nq, DSA_TQ, H_IDX).transpose(0, 2, 1)
    qt = (q * D_HEAD ** -0.5).transpose(1, 2, 0).astype(bf)
    kf = k.reshape(t, D_ATT).astype(bf)
    vt = v.transpose(1, 2, 0).astype(bf)
    steps = [(i, j) for i in range(nq) for j in range((i * DSA_TQ + DSA_TQ - 1) // DSA_TS + 1)]
    ii = jnp.asarray([s[0] for s in steps], jnp.int32)
    jj = jnp.asarray([s[1] for s in steps], jnp.int32)
    grid_spec = pltpu.PrefetchScalarGridSpec(
        num_scalar_prefetch=2,
        grid=(len(steps),),
        in_specs=[
            pl.BlockSpec((1, H_IDX // DSA_HG, D_IDX, DSA_HG * DSA_TQ), lambda s, ii, jj: (ii[s], 0, 0, 0)),
            pl.BlockSpec((1, H_IDX, DSA_TQ), lambda s, ii, jj: (ii[s], 0, 0)),
            pl.BlockSpec((t, D_IDX), lambda s, ii, jj: (0, 0)),
            pl.BlockSpec((H_ATT, D_HEAD, DSA_TQ), lambda s, ii, jj: (0, 0, ii[s])),
            pl.BlockSpec((DSA_TS, D_ATT), lambda s, ii, jj: (jj[s], 0)),
            pl.BlockSpec((H_ATT, D_HEAD, DSA_TS), lambda s, ii, jj: (0, 0, jj[s])),
        ],
        out_specs=pl.BlockSpec((DSA_TQ, D_ATT), lambda s, ii, jj: (ii[s], 0)),
        scratch_shapes=[
            pltpu.VMEM((t, DSA_TQ), jnp.int32),
            pltpu.VMEM((1, DSA_TQ), jnp.int32),
            pltpu.VMEM((DSA_TS, DSA_TQ), jnp.float32),
            pltpu.VMEM((H_ATT, DSA_TQ), jnp.float32),
            pltpu.VMEM((H_ATT, DSA_TQ), jnp.float32),
            pltpu.VMEM((H_ATT, D_HEAD, DSA_TQ), jnp.float32),
        ])
    return pl.pallas_call(
        functools.partial(_dsa_prompt_kernel, n_sel=n_sel),
        grid_spec=grid_spec,
        out_shape=jax.ShapeDtypeStruct((t, D_ATT), jnp.float32),
        compiler_params=pltpu.CompilerParams(dimension_semantics=("arbitrary",), vmem_limit_bytes=VMEM_LIMIT),
        name="dsa_prompt",
    )(ii, jj, qit, wr, ki.astype(bf), qt, kf, vt)


def _dsa_prompt(q, k, v, qi, wi, ki):
    return jnp.stack([_dsa_prompt_one(q[b], k[b], v[b], qi[b], wi[b], ki[b]) for b in range(q.shape[0])])


SMP_IDX_PAGES = 8
SMP_ATT_PAGES = 8
SMP_CH = 128


def _smp_score_kernel(pt_ref, *refs):
    pages = refs[:SMP_IDX_PAGES]
    qi_ref, wb_ref, o_ref = refs[SMP_IDX_PAGES:]
    nq = o_ref.shape[1]
    for g, ki_ref in enumerate(pages):
        z = lax.dot_general(qi_ref[0], ki_ref[0, 0].astype(jnp.bfloat16), (((1,), (1,)), ((), ())),
                            preferred_element_type=jnp.float32)
        r = jnp.maximum(z, 0.0) * wb_ref[0]
        o_ref[0, :, g * PAGE_SIZE:(g + 1) * PAGE_SIZE] = jnp.sum(r.reshape(nq, H_IDX, PAGE_SIZE), axis=1)


def _smp_thr_kernel(s_ref, thr_ref, xs_ref, *, n_sel):
    nch = s_ref.shape[0] // SMP_CH

    def conv(c, carry):
        r0 = pl.multiple_of(c * SMP_CH, SMP_CH)
        xs_ref[pl.ds(r0, SMP_CH), :] = _float_to_key(s_ref[pl.ds(r0, SMP_CH), :])
        return carry

    lax.fori_loop(0, nch, conv, 0)
    key = _kth_largest_key(xs_ref, nch, SMP_CH, n_sel)
    thr_ref[...] = lax.bitcast_convert_type(key ^ ((key >> 31) & jnp.int32(0x7FFFFFFF)), jnp.float32)


def _smp_attn_kernel(pt_ref, *refs, nq):
    kp = refs[:SMP_ATT_PAGES]
    vp = refs[SMP_ATT_PAGES:2 * SMP_ATT_PAGES]
    (sc_ref, scn_ref, thr_ref, q_ref, kn_ref, vn_ref, ex_ref, hm_ref,
     o_ref, m_ref, l_ref, acc_ref) = refs[2 * SMP_ATT_PAGES:]
    pp = pl.program_id(1)
    rows = PAGE_SIZE * H_ATT

    def attend(k, v, sc):
        sel = jnp.where(sc >= thr_ref[0], 1.0, 0.0)
        sel = jnp.concatenate(
            [jnp.broadcast_to(sel[tt:tt + 1, :], (H_ATT, PAGE_SIZE)) for tt in range(nq)], axis=0)
        valid = jnp.dot(sel.astype(jnp.bfloat16), ex_ref[...], preferred_element_type=jnp.float32) * hm_ref[...]
        s = lax.dot_general(q_ref[0], k.reshape(rows, D_HEAD).astype(jnp.bfloat16), (((1,), (1,)), ((), ())),
                            preferred_element_type=jnp.float32)
        s = jnp.where(valid > 0.5, s, -jnp.inf)
        m = jnp.maximum(jnp.max(s, axis=1, keepdims=True), -1e30)
        p = jnp.exp(s - m)
        acc = jnp.dot(p.astype(jnp.bfloat16), v.reshape(rows, D_HEAD).astype(jnp.bfloat16),
                      preferred_element_type=jnp.float32)
        return m, jnp.sum(p, axis=1, keepdims=True), acc

    def merge(m, l, acc):
        m_old = m_ref[...]
        m_new = jnp.maximum(m_old, m)
        a_old = jnp.exp(m_old - m_new)
        a_new = jnp.exp(m - m_new)
        l_ref[...] = a_old * l_ref[...] + a_new * l
        acc_ref[...] = a_old * acc_ref[...] + a_new * acc
        m_ref[...] = m_new

    @pl.when(pp == 0)
    def _():
        m_ref[...] = jnp.full(m_ref.shape, -1e30, jnp.float32)
        l_ref[...] = jnp.zeros(l_ref.shape, jnp.float32)
        acc_ref[...] = jnp.zeros(acc_ref.shape, jnp.float32)
        merge(*attend(kn_ref[0], vn_ref[0], scn_ref[0]))

    parts = [attend(kp[g][0, 0], vp[g][0, 0], sc_ref[0, :, g * PAGE_SIZE:(g + 1) * PAGE_SIZE])
             for g in range(SMP_ATT_PAGES)]
    for part in parts:
        merge(*part)

    @pl.when(pp == pl.num_programs(1) - 1)
    def _():
        o_ref[0] = acc_ref[...] / l_ref[...]


def _dsa_sample(q, k_new, v_new, qi, wi, ki_new, cache_k, cache_v, cache_idx_k, page_table, layer):
    db, t = q.shape[:2]
    npg = page_table.shape[1]
    past_len = npg * PAGE_SIZE
    n_sel = min(TOPK_ATT_MAX, (past_len + t) // 4)
    assert npg % SMP_IDX_PAGES == 0 and npg % SMP_ATT_PAGES == 0 and (t * H_IDX) % 8 == 0
    bf = jnp.bfloat16
    pt = page_table.reshape(-1).astype(jnp.int32)
    n_pool = cache_k.shape[1]

    def page_spec(width, per_step, g):
        return pl.BlockSpec((1, 1, PAGE_SIZE, width),
                            lambda b, p, pt: (layer, pt[b * npg + p * per_step + g], 0, 0))

    row_spec = lambda rows, width: pl.BlockSpec((1, rows, width), lambda b, p, pt: (b, 0, 0))
    score_past = pl.pallas_call(
        _smp_score_kernel,
        grid_spec=pltpu.PrefetchScalarGridSpec(
            num_scalar_prefetch=1, grid=(db, npg // SMP_IDX_PAGES),
            in_specs=[page_spec(D_IDX, SMP_IDX_PAGES, g) for g in range(SMP_IDX_PAGES)]
            + [row_spec(t * H_IDX, D_IDX), row_spec(t * H_IDX, PAGE_SIZE)],
            out_specs=pl.BlockSpec((1, t, SMP_IDX_PAGES * PAGE_SIZE), lambda b, p, pt: (b, 0, p))),
        out_shape=jax.ShapeDtypeStruct((db, t, past_len), jnp.float32),
        compiler_params=pltpu.CompilerParams(dimension_semantics=("parallel", "arbitrary"),
                                             vmem_limit_bytes=VMEM_LIMIT),
        name="smp_score",
    )(pt, *([cache_idx_k] * SMP_IDX_PAGES), qi.reshape(db, t * H_IDX, D_IDX).astype(bf),
      jnp.broadcast_to(wi.reshape(db, t * H_IDX, 1), (db, t * H_IDX, PAGE_SIZE)))

    rel = jax.nn.relu(jnp.einsum('bthd,bsd->bths', qi, ki_new))
    score_new = jnp.einsum('bths,bth->bts', rel, wi)
    score_new = jnp.where(jnp.arange(t)[None, None, :] <= jnp.arange(t)[None, :, None], score_new, -jnp.inf)
    score_new = jnp.pad(score_new, ((0, 0), (0, 0), (0, PAGE_SIZE - t)), constant_values=-jnp.inf)

    score_t = jnp.concatenate([score_past, score_new], axis=2).reshape(db * t, past_len + PAGE_SIZE).T
    thr = pl.pallas_call(
        functools.partial(_smp_thr_kernel, n_sel=n_sel),
        out_shape=jax.ShapeDtypeStruct((1, db * t), jnp.float32),
        scratch_shapes=[pltpu.VMEM(score_t.shape, jnp.int32)],
        compiler_params=pltpu.CompilerParams(vmem_limit_bytes=VMEM_LIMIT),
        name="smp_thr",
    )(score_t)
    thr_b = jnp.broadcast_to(thr.reshape(db, t, 1), (db, t, PAGE_SIZE))

    nrow = t * H_ATT
    cols = PAGE_SIZE * H_ATT
    q_rows = (q * D_HEAD ** -0.5).reshape(db, nrow, D_HEAD).astype(bf)
    expand = jnp.repeat(jnp.eye(PAGE_SIZE, dtype=bf), H_ATT, axis=1)
    head_match = (jnp.arange(cols)[None, :] % H_ATT == jnp.arange(nrow)[:, None] % H_ATT).astype(jnp.float32)
    pad_page = lambda a: jnp.pad(a, ((0, 0), (0, PAGE_SIZE - t), (0, 0), (0, 0)))
    cache_spec = lambda g: pl.BlockSpec(
        (1, 1, PAGE_SIZE, H_ATT, D_HEAD),
        lambda b, p, pt: (layer, pt[b * npg + p * SMP_ATT_PAGES + g], 0, 0, 0))
    const_spec = lambda shape: pl.BlockSpec(shape, lambda b, p, pt: (0,) * len(shape))
    new_spec = pl.BlockSpec((1, PAGE_SIZE, H_ATT, D_HEAD), lambda b, p, pt: (b, 0, 0, 0))
    out = pl.pallas_call(
        functools.partial(_smp_attn_kernel, nq=t),
        grid_spec=pltpu.PrefetchScalarGridSpec(
            num_scalar_prefetch=1, grid=(db, npg // SMP_ATT_PAGES),
            in_specs=[cache_spec(g) for g in range(SMP_ATT_PAGES)] * 2
            + [pl.BlockSpec((1, t, SMP_ATT_PAGES * PAGE_SIZE), lambda b, p, pt: (b, 0, p)),
               row_spec(t, PAGE_SIZE), row_spec(t, PAGE_SIZE), row_spec(nrow, D_HEAD),
               new_spec, new_spec, const_spec((PAGE_SIZE, cols)), const_spec((nrow, cols))],
            out_specs=row_spec(nrow, D_HEAD),
            scratch_shapes=[pltpu.VMEM((nrow, 1), jnp.float32), pltpu.VMEM((nrow, 1), jnp.float32),
                            pltpu.VMEM((nrow, D_HEAD), jnp.float32)]),
        out_shape=jax.ShapeDtypeStruct((db, nrow, D_HEAD), jnp.float32),
        compiler_params=pltpu.CompilerParams(dimension_semantics=("parallel", "arbitrary"),
                                             vmem_limit_bytes=VMEM_LIMIT),
        name="smp_attn",
    )(pt, *([cache_k] * SMP_ATT_PAGES), *([cache_v] * SMP_ATT_PAGES),
      score_past, score_new, thr_b, q_rows, pad_page(k_new), pad_page(v_new), expand, head_match)
    return out.reshape(db, t, D_ATT)


def _peer(x, wq, k1, k2, u, v):
    shape = x.shape
    xf = x.reshape(-1, D_MODEL)
    n = xf.shape[0]
    pad = (-n) % PEER_BLOCK
    xf = jnp.pad(xf, ((0, pad), (0, 0)))
    xb16 = xf.astype(jnp.bfloat16)
    e1, e2, gate = _peer_route(_mm(xb16, wq), k1, k2)
    wmat = _peer_gate_matrix(e1, e2, gate)
    out = _peer_dense(xb16, u, v, wmat)
    return out[:n].reshape(shape)


PEER_RT = 128
PEER_RH = 4
PEER_TOPK_LOG2 = PEER_TOPK.bit_length() - 1
assert 1 << PEER_TOPK_LOG2 == PEER_TOPK
PEER_PAIRS = tuple((i, j) for i in range(PEER_TOPK) for j in range(PEER_TOPK) if (i + 1) * (j + 1) <= PEER_TOPK)


def _extract_top(s, ids, n):
    big = jnp.int32(2 ** 30)
    vals, sel = [], []
    for _ in range(n):
        m = jnp.max(s, axis=0, keepdims=True)
        pick = jnp.min(jnp.where(s == m, ids, big), axis=0, keepdims=True)
        vals.append(m)
        sel.append(pick)
        s = jnp.where(ids == pick, -jnp.inf, s)
    return vals, sel


def _peer_route_kernel(q_ref, k1_ref, k2_ref, e1_ref, e2_ref, g_ref):
    kio = lax.broadcasted_iota(jnp.int32, (PEER_NKEYS, PEER_RT), 0)
    flat = jnp.concatenate(
        [jnp.full((1, PEER_RT), i * PEER_TOPK + j, jnp.int32) for i, j in PEER_PAIRS], axis=0)
    for hh in range(PEER_RH):
        tops = []
        for side, kref in ((0, k1_ref), (1, k2_ref)):
            lo = (2 * hh + side) * (PEER_DQ // 2)
            qs = q_ref[:, lo:lo + PEER_DQ // 2].astype(jnp.bfloat16)
            s = lax.dot_general(kref[hh], qs, (((1,), (1,)), ((), ())),
                                preferred_element_type=jnp.float32)
            tops.append(_extract_top(s, kio, PEER_TOPK))
        (v1, i1), (v2, i2) = tops
        cand = jnp.concatenate([v1[i] + v2[j] for i, j in PEER_PAIRS], axis=0)
        top_s, top_f = _extract_top(cand, flat, PEER_TOPK)
        ts = jnp.concatenate(top_s, axis=0)
        tf = jnp.concatenate(top_f, axis=0)
        ex = jnp.exp(ts - top_s[0])
        g_ref[hh] = ex * (1.0 / jnp.sum(ex, axis=0, keepdims=True))
        fi = tf >> PEER_TOPK_LOG2
        fj = tf & (PEER_TOPK - 1)
        e1_ref[hh] = sum(jnp.where(fi == i, i1[i], 0) for i in range(PEER_TOPK))
        e2_ref[hh] = sum(jnp.where(fj == j, i2[j], 0) for j in range(PEER_TOPK))


def _peer_route(q, k1, k2):
    n = q.shape[0]
    assert n % PEER_RT == 0 and PEER_HEADS % PEER_RH == 0
    kspec = pl.BlockSpec((PEER_RH, PEER_NKEYS, PEER_DQ // 2), lambda i, h: (h, 0, 0))
    ospec = pl.BlockSpec((PEER_RH, PEER_TOPK, PEER_RT), lambda i, h: (h, 0, i))
    oshape = lambda dt: jax.ShapeDtypeStruct((PEER_HEADS, PEER_TOPK, n), dt)
    e1, e2, g = pl.pallas_call(
        _peer_route_kernel,
        grid=(n // PEER_RT, PEER_HEADS // PEER_RH),
        in_specs=[pl.BlockSpec((PEER_RT, PEER_RH * PEER_DQ), lambda i, h: (i, h)), kspec, kspec],
        out_specs=[ospec, ospec, ospec],
        out_shape=[oshape(jnp.int32), oshape(jnp.int32), oshape(jnp.float32)],
        compiler_params=pltpu.CompilerParams(dimension_semantics=("parallel", "parallel"),
                                             vmem_limit_bytes=VMEM_LIMIT),
        name="peer_route",
    )(q, k1.astype(jnp.bfloat16), k2.astype(jnp.bfloat16))
    tok_major = lambda a: a.reshape(PEER_HEADS * PEER_TOPK, n).T
    return tok_major(e1), tok_major(e2), tok_major(g)


PEER_GT = 16


def _peer_gate_kernel(i1_ref, i2_ref, g_ref, w_ref, tmp_ref):
    iota = lax.broadcasted_iota(jnp.int32, (PEER_NKEYS, PEER_HEADS * PEER_TOPK), 0)

    def body(gi, carry):
        t0 = pl.multiple_of(gi * PEER_GT, PEER_GT)
        for kk in range(PEER_GT):
            a = jnp.where(iota == i1_ref[pl.ds(t0 + kk, 1), :], g_ref[pl.ds(t0 + kk, 1), :], 0.0).astype(jnp.bfloat16)
            b = jnp.where(iota == i2_ref[pl.ds(t0 + kk, 1), :], 1.0, 0.0).astype(jnp.bfloat16)
            tmp_ref[kk * PEER_NKEYS:(kk + 1) * PEER_NKEYS, :] = lax.dot_general(
                a, b, (((1,), (1,)), ((), ())), preferred_element_type=jnp.float32)
        for aa in range(PEER_NKEYS):
            rows = tmp_ref[pl.ds(aa, PEER_GT, stride=PEER_NKEYS), :]
            w_ref[pl.ds(t0, PEER_GT), aa * PEER_NKEYS:(aa + 1) * PEER_NKEYS] = rows.astype(w_ref.dtype)
        return carry

    lax.fori_loop(0, i1_ref.shape[0] // PEER_GT, body, 0)


def _peer_gate_matrix(e1, e2, gate):
    n, p = e1.shape
    tw = _pick(n, (128,))
    assert tw % PEER_GT == 0
    spec = pl.BlockSpec((tw, p), lambda i: (i, 0))
    return pl.pallas_call(
        _peer_gate_kernel,
        grid=(n // tw,),
        in_specs=[spec, spec, spec],
        out_specs=pl.BlockSpec((tw, PEER_NKEYS * PEER_NKEYS), lambda i: (i, 0)),
        out_shape=jax.ShapeDtypeStruct((n, PEER_NKEYS * PEER_NKEYS), jnp.bfloat16),
        scratch_shapes=[pltpu.VMEM((PEER_GT * PEER_NKEYS, PEER_NKEYS), jnp.float32)],
        compiler_params=pltpu.CompilerParams(dimension_semantics=("parallel",), vmem_limit_bytes=VMEM_LIMIT),
        name="peer_gate",
    )(e1, e2, gate)


def _peer_dense_kernel(h_ref, ut_ref, w_ref, v_ref, o_ref):
    @pl.when(pl.program_id(1) == 0)
    def _():
        o_ref[...] = jnp.zeros_like(o_ref)

    half = ut_ref.shape[1] // 2
    upd = None
    for s in range(2):
        z = jnp.dot(h_ref[...], ut_ref[:, s * half:(s + 1) * half], preferred_element_type=jnp.float32)
        act = 0.5 * z * (1.0 + lax.erf(z * (2.0 ** -0.5)))
        c = (w_ref[:, s * half:(s + 1) * half].astype(jnp.float32) * act).astype(jnp.bfloat16)
        d = jnp.dot(c, v_ref[s * half:(s + 1) * half, :], preferred_element_type=jnp.float32)
        upd = d if upd is None else upd + d
    o_ref[...] += upd


def _peer_dense(xb16, ut, vb, wmat):
    n = xb16.shape[0]
    e = vb.shape[0]
    tm = _pick(n, (512, 256, 128))
    te = 512
    return pl.pallas_call(
        _peer_dense_kernel,
        grid=(n // tm, e // te),
        in_specs=[pl.BlockSpec((tm, D_MODEL), lambda i, j: (i, 0)),
                  pl.BlockSpec((D_MODEL, te), lambda i, j: (0, j)),
                  pl.BlockSpec((tm, te), lambda i, j: (i, j)),
                  pl.BlockSpec((te, D_MODEL), lambda i, j: (j, 0))],
        out_specs=pl.BlockSpec((tm, D_MODEL), lambda i, j: (i, 0)),
        out_shape=jax.ShapeDtypeStruct((n, D_MODEL), jnp.float32),
        compiler_params=pltpu.CompilerParams(dimension_semantics=("parallel", "arbitrary"),
                                             vmem_limit_bytes=56 * 1024 * 1024),
        name="peer_dense",
    )(xb16, ut, wmat, vb)


def _group(x, pos, conv_buf, ssm0, chunked, attend, w_in, conv_w, a_log, dt_bias, gdn_norm_w,
           w_br_gdn, w_br_att, w_out, ln1_g, ln1_b, peer_wq, peer_k1, peer_k2, peer_u, peer_v, ln2_g, ln2_b):
    xb = x.reshape(-1, D_MODEL).astype(jnp.bfloat16)
    proj = lambda name: _mm(xb, w_in[name]).reshape(*x.shape[:-1], -1)
    qkv, z, q_att, k_att, v_att, q_idx, k_idx, gate_gdn, gate_att = (
        proj(n) for n in ("qkv", "z", "q_att", "k_att", "v_att", "q_idx", "k_idx", "gate_gdn", "gate_att"))
    small = proj("small")
    b_raw, a_raw, w_idx = small[..., :H_GDN], small[..., H_GDN:2 * H_GDN], small[..., 2 * H_GDN:2 * H_GDN + H_IDX]
    conv_out, conv_new = _causal_conv(qkv, conv_buf, conv_w)
    q, k, v = _gdn_qkv(conv_out)
    g, beta = _gdn_gates(b_raw, a_raw, a_log, dt_bias)
    gdn = _gdn_chunked if chunked else _gdn_recurrent
    o, ssm_new = gdn(q, k, v, g, beta, ssm0.astype(jnp.float32))
    o_gdn = _gdn_out(o, z, gdn_norm_w)
    qa, ka, va, qi, wi, ki = _att_heads(q_att, k_att, v_att, q_idx, w_idx, k_idx, pos)
    o_att = attend(qa, ka, va, qi, wi, ki)
    mix = _mmf(jax.nn.sigmoid(gate_gdn) * _mmf(o_gdn, w_br_gdn, D_MODEL) + jax.nn.sigmoid(gate_att) * _mmf(o_att, w_br_att, D_MODEL),
               w_out, D_MODEL)
    h = _layernorm(DEEPNORM_ALPHA * x + mix, ln1_g, ln1_b)
    y = _layernorm(DEEPNORM_ALPHA * h + _peer(h, peer_wq, peer_k1, peer_k2, peer_u, peer_v), ln2_g, ln2_b)
    return y, ka, va, ki, conv_new, ssm_new.astype(x.dtype)


def kernel(x_prompt, x_sample, cache_k, cache_v, cache_idx_k, state_conv, state_ssm, page_table, w_in, conv_w, a_log, dt_bias, gdn_norm_w, w_br_gdn, w_br_att, w_out, ln1_g, ln1_b, peer_wq, peer_k1, peer_k2, peer_u, peer_v, ln2_g, ln2_b):
    bp, sp = x_prompt.shape[:2]
    ds = x_sample.shape[1]
    past_len = page_table.shape[1] * PAGE_SIZE
    pos_p = jnp.arange(sp)
    pos_s = past_len + jnp.arange(ds)
    l = 0
    wts = (_in_proj_weights(w_in[l]), conv_w[l], a_log[l], dt_bias[l], gdn_norm_w[l], _wcast(w_br_gdn[l]), _wcast(w_br_att[l]),
           _wcast(w_out[l]), ln1_g[l], ln1_b[l], _wcast(peer_wq[l]), peer_k1[l], peer_k2[l],
           peer_u[l].astype(jnp.bfloat16).T, peer_v[l].astype(jnp.bfloat16), ln2_g[l], ln2_b[l])
    y_p, kp, vp, ip, cp, ssp = _group(
        x_prompt, pos_p, jnp.zeros((bp, CONV_W - 1, CONV_DIM), x_prompt.dtype),
        jnp.zeros((bp, H_GDN, D_HEAD, D_HEAD), jnp.float32), True, _dsa_prompt, *wts)
    attend_s = functools.partial(_dsa_sample, cache_k=cache_k, cache_v=cache_v, cache_idx_k=cache_idx_k,
                                 page_table=page_table, layer=l)
    y_s, kss, vss, iss, css, sss = _group(
        x_sample, pos_s, state_conv[l], state_ssm[l], False, attend_s, *wts)
    st = lambda a: a[None]
    return (y_p, y_s, st(kp), st(vp), st(ip), st(cp), st(ssp), st(kss), st(vss), st(iss), st(css), st(sss))
```
